```python
import math
import jax
import jax.numpy as jnp
from jax import lax
import numpy as np

D_MODEL = 2048
BATCH = 1
SEQ = 8192
DEPTH = 4

GRID_W = 64
CTX_LEN = 256
N_MIXERS = 4
MIX_WIDTH = D_MODEL
GROUP_WIDTH = MIX_WIDTH // N_MIXERS

DA_HEAD_DIM = 64
DA_HEADS = GROUP_WIDTH // (2 * DA_HEAD_DIM)
GQA_HEAD_DIM = 128
GQA_HEADS = GROUP_WIDTH // GQA_HEAD_DIM
GQA_KV_HEADS = GQA_HEADS // 2
LRU_WIDTH = GROUP_WIDTH
LRU_BLOCKS = 4
LRU_CONV = 4
RGLRU_C = 8.0
SSM_D_INNER = GROUP_WIDTH
SSM_HEAD_DIM = 64
SSM_HEADS = SSM_D_INNER // SSM_HEAD_DIM
SSM_GROUPS = 2
SSM_D_STATE = 128
SSM_CONV = 4
SSD_CHUNK = 128
D_FF = 5632
FFN_CONV = 3
Q_BLOCK = 128
ROPE_THETA = 10000.0
NORM_EPS = 1e-6

SSM_CONV_DIM = SSM_D_INNER + 2 * SSM_GROUPS * SSM_D_STATE
IN_SIZES = (
    2 * DA_HEADS * DA_HEAD_DIM,
    2 * DA_HEADS * DA_HEAD_DIM,
    2 * DA_HEADS * DA_HEAD_DIM,
    GQA_HEADS * GQA_HEAD_DIM,
    GQA_KV_HEADS * GQA_HEAD_DIM,
    GQA_KV_HEADS * GQA_HEAD_DIM,
    LRU_WIDTH,
    LRU_WIDTH,
    SSM_D_INNER,
    SSM_CONV_DIM,
    2 * SSM_HEADS,
)
IN_COLS = sum(IN_SIZES)

kernel_name = 'hybrid_parallel_group_flow_block'


def rmsnorm(x, w):
    xf = x.astype(jnp.float32)
    y = xf * lax.rsqrt(jnp.mean(xf * xf, axis=-1, keepdims=True) + NORM_EPS)
    return (y * w.astype(jnp.float32)).astype(x.dtype)


def modulate(h, shift, scale):
    return h * (1 + scale) + shift


def modulation(cond, w_mod, b_mod):
    m = jax.nn.silu(cond) @ w_mod + b_mod
    return jnp.split(m, 6, axis=-1)


def split_columns(p):
    offs = np.cumsum(IN_SIZES)[:-1].tolist()
    return jnp.split(p, offs, axis=-1)


def to_heads(t, n_heads):
    b, s, hd = t.shape
    return t.reshape(b, s, n_heads, hd // n_heads).transpose(0, 2, 1, 3)


def merge_heads(t):
    b, h, s, d = t.shape
    return t.transpose(0, 2, 1, 3).reshape(b, s, h * d)


def rev(t, backward):
    return t[:, ::-1] if backward else t


def dwconv_centred(x, w, b):
    width = w.shape[0]
    length = x.shape[1]
    xp = jnp.pad(x, ((0, 0), (width // 2, width - 1 - width // 2), (0, 0)))
    return b + sum(w[k] * xp[:, k:k + length] for k in range(width))


def rope_1d(x, pos):
    half = x.shape[-1] // 2
    freqs = jnp.power(ROPE_THETA, -jnp.arange(half, dtype=jnp.float32) / half)
    ang = pos.astype(jnp.float32)[:, None] * freqs[None, :]
    cos = jnp.cos(ang).astype(x.dtype)
    sin = jnp.sin(ang).astype(x.dtype)
    x1, x2 = x[..., :half], x[..., half:]
    return jnp.concatenate([x1 * cos - x2 * sin, x2 * cos + x1 * sin], axis=-1)


def axial_rope(x, row, col):
    d = x.shape[-1]
    return jnp.concatenate([rope_1d(x[..., :d // 2], row), rope_1d(x[..., d // 2:], col)], axis=-1)


def sweep_query_blocks(fn, *qs):
    b, h, s, _ = qs[0].shape
    nb = s // Q_BLOCK
    blocks = tuple(q.reshape(b, h, nb, Q_BLOCK, q.shape[-1]).transpose(2, 0, 1, 3, 4) for q in qs)
    out = lax.map(lambda qb: fn(*qb), blocks)
    return out.transpose(1, 2, 0, 3, 4).reshape(b, out.shape[2], s, out.shape[-1])


def diff_attention(q1, q2, k1, k2, v, lam):
    scale = q1.shape[-1] ** -0.5

    def block(qb1, qb2):
        s1 = jnp.einsum('bhqd,bhtd->bhqt', qb1, k1).astype(jnp.float32) * scale
        s2 = jnp.einsum('bhqd,bhtd->bhqt', qb2, k2).astype(jnp.float32) * scale
        p = jax.nn.softmax(s1, axis=-1) - lam * jax.nn.softmax(s2, axis=-1)
        return jnp.einsum('bhqt,bhtv->bhqv', p.astype(v.dtype), v)

    return sweep_query_blocks(block, q1, q2)


def gqa_attention(q, k, v):
    b, hq, _, d = q.shape
    hkv = k.shape[1]
    g = hq // hkv
    scale = d ** -0.5

    def block(qb):
        nq = qb.shape[2]
        qg = qb.reshape(b, hkv, g, nq, d)
        s = jnp.einsum('bkgqd,bktd->bkgqt', qg, k).astype(jnp.float32) * scale
        p = jax.nn.softmax(s, axis=-1)
        o = jnp.einsum('bkgqt,bktd->bkgqd', p.astype(v.dtype), v)
        return o.reshape(b, hq, nq, d)

    return sweep_query_blocks(block, q)


def diff_attention_mixer(q_c, k_c, v_c, q_l, k_l, v_l, row, col, lam_vecs, subln_w, lam_init, need_ctx):
    lv = lam_vecs.astype(jnp.float32)
    lam = jnp.exp(jnp.sum(lv[0] * lv[1])) - jnp.exp(jnp.sum(lv[2] * lv[3])) + lam_init
    qc, kc, vc = to_heads(q_c, 2 * DA_HEADS), to_heads(k_c, 2 * DA_HEADS), to_heads(v_c, DA_HEADS)
    ql = axial_rope(to_heads(q_l, 2 * DA_HEADS), row, col)
    kl = axial_rope(to_heads(k_l, 2 * DA_HEADS), row, col)
    vl = to_heads(v_l, DA_HEADS)
    k_all = jnp.concatenate([kc, kl], axis=2)
    v_all = jnp.concatenate([vc, vl], axis=2)

    def finish(o):
        return merge_heads(rmsnorm(o, subln_w) * (1 - lam_init))

    y_l = finish(diff_attention(ql[:, 0::2], ql[:, 1::2], k_all[:, 0::2], k_all[:, 1::2], v_all, lam))
    y_c = finish(diff_attention(qc[:, 0::2], qc[:, 1::2], kc[:, 0::2], kc[:, 1::2], vc, lam)) if need_ctx else None
    return y_c, y_l


def gqa_mixer(q_c, k_c, v_c, q_l, k_l, v_l, row, col, q_norm, k_norm, need_ctx):
    qc = rmsnorm(to_heads(q_c, GQA_HEADS), q_norm)
    kc = rmsnorm(to_heads(k_c, GQA_KV_HEADS), k_norm)
    vc = to_heads(v_c, GQA_KV_HEADS)
    ql = axial_rope(rmsnorm(to_heads(q_l, GQA_HEADS), q_norm), row, col)
    kl = axial_rope(rmsnorm(to_heads(k_l, GQA_KV_HEADS), k_norm), row, col)
    vl = to_heads(v_l, GQA_KV_HEADS)
    y_l = merge_heads(gqa_attention(ql, jnp.concatenate([kc, kl], axis=2), jnp.concatenate([vc, vl], axis=2)))
    y_c = merge_heads(gqa_attention(qc, kc, vc)) if need_ctx else None
    return y_c, y_l


def rglru_scan(x, w_gates, b_gates, lam, h0):
    b, length, width = x.shape
    xb = x.reshape(b, length, LRU_BLOCKS, width // LRU_BLOCKS)
    gates = jnp.einsum('blhi,ghij->gblhj', xb, w_gates).reshape(2, b, length, width) + b_gates[:, None, None, :]
    r = jax.nn.sigmoid(gates[0].astype(jnp.float32))
    i = jax.nn.sigmoid(gates[1].astype(jnp.float32))
    log_a = RGLRU_C * r * jax.nn.log_sigmoid(lam.astype(jnp.float32))
    a = jnp.exp(log_a)
    u = jnp.sqrt(-jnp.expm1(2.0 * log_a)) * (i * x.astype(jnp.float32))
    u = u.at[:, 0].add(a[:, 0] * h0.astype(jnp.float32))
    _, h = lax.associative_scan(lambda e, f: (e[0] * f[0], f[0] * e[1] + f[1]), (a, u), axis=1)
    return h.astype(x.dtype)


def rglru_mixer(x_c, g_c, x_l, g_l, conv_w, conv_b, w_gates, b_gates, lam, need_ctx):
    xc = dwconv_centred(x_c, conv_w, conv_b)
    xl = dwconv_centred(x_l, conv_w, conv_b)
    hs_c, hs_l = [], []
    for d in range(2):
        bwd = d == 1
        h_c = rglru_scan(rev(xc, bwd), w_gates[d], b_gates[d], lam[d], jnp.zeros((xc.shape[0], LRU_WIDTH), xc.dtype))
        h_l = rglru_scan(rev(xl, bwd), w_gates[d], b_gates[d], lam[d], h_c[:, -1])
        hs_c.append(rev(h_c, bwd))
        hs_l.append(rev(h_l, bwd))
    y_l = (hs_l[0] + hs_l[1]) * jax.nn.gelu(g_l)
    y_c = (hs_c[0] + hs_c[1]) * jax.nn.gelu(g_c) if need_ctx else None
    return y_c, y_l


def segsum(x):
    t = x.shape[-1]
    xx = jnp.broadcast_to(x[..., :, None], x.shape + (t,))
    strict = jnp.tril(jnp.ones((t, t), dtype=bool), -1)
    out = jnp.cumsum(jnp.where(strict, xx, 0.0), axis=-2)
    return jnp.where(jnp.tril(jnp.ones((t, t), dtype=bool)), out, -jnp.inf)


def ssd_chunked(x, dt, a, bm, cm, h0):
    b, length, h, p = x.shape
    n = bm.shape[-1]
    nc = length // SSD_CHUNK
    f32 = jnp.float32
    xs = (x.astype(f32) * dt[..., None]).reshape(b, nc, SSD_CHUNK, h, p)
    bc = bm.astype(f32).reshape(b, nc, SSD_CHUNK, h, n)
    cc = cm.astype(f32).reshape(b, nc, SSD_CHUNK, h, n)
    adt = (dt * a).reshape(b, nc, SSD_CHUNK, h).transpose(0, 3, 1, 2)
    a_cum = jnp.cumsum(adt, axis=-1)
    l_in = jnp.exp(segsum(adt))
    y_diag = jnp.einsum('bclhn,bcshn,bhcls,bcshp->bclhp', cc, bc, l_in, xs)
    decay_in = jnp.exp(a_cum[..., -1:] - a_cum)
    states = jnp.einsum('bclhn,bhcl,bclhp->bchpn', bc, decay_in, xs)
    states = jnp.concatenate([h0.astype(f32)[:, None], states], axis=1)
    decay_chunk = jnp.exp(segsum(jnp.pad(a_cum[..., -1], ((0, 0), (0, 0), (1, 0)))))
    states = jnp.einsum('bhzc,bchpn->bzhpn', decay_chunk, states)
    y_off = jnp.einsum('bclhn,bchpn,bhcl->bclhp', cc, states[:, :-1], jnp.exp(a_cum))
    y = (y_diag + y_off).reshape(b, length, h, p)
    return y.astype(x.dtype), states[:, -1]


def ssd_mixer(z_c, xbc_c, dt_c, z_l, xbc_l, dt_l, conv_w, conv_b, dt_bias, a_log, d_skip, norm_w, need_ctx):
    rep = SSM_HEADS // SSM_GROUPS

    def prep(xbc):
        xbc = jax.nn.silu(dwconv_centred(xbc, conv_w, conv_b))
        b, length, _ = xbc.shape
        xs, bm, cm = jnp.split(xbc, [SSM_D_INNER, SSM_D_INNER + SSM_GROUPS * SSM_D_STATE], axis=-1)
        xs = xs.reshape(b, length, SSM_HEADS, SSM_HEAD_DIM)
        bm = jnp.repeat(bm.reshape(b, length, SSM_GROUPS, SSM_D_STATE), rep, axis=2)
        cm = jnp.repeat(cm.reshape(b, length, SSM_GROUPS, SSM_D_STATE), rep, axis=2)
        return xs, bm, cm

    xc, bc, cc = prep(xbc_c)
    xl, bl, cl = prep(xbc_l)
    ys_c, ys_l = [], []
    for d in range(2):
        bwd = d == 1
        a = -jnp.exp(a_log[d].astype(jnp.float32))
        dtc = jax.nn.softplus(dt_c[..., d * SSM_HEADS:(d + 1) * SSM_HEADS].astype(jnp.float32) + dt_bias[d])
        dtl = jax.nn.softplus(dt_l[..., d * SSM_HEADS:(d + 1) * SSM_HEADS].astype(jnp.float32) + dt_bias[d])
        h0 = jnp.zeros((xc.shape[0], SSM_HEADS, SSM_HEAD_DIM, SSM_D_STATE), jnp.float32)
        yc, hc = ssd_chunked(rev(xc, bwd), rev(dtc, bwd), a, rev(bc, bwd), rev(cc, bwd), h0)
        yl, _ = ssd_chunked(rev(xl, bwd), rev(dtl, bwd), a, rev(bl, bwd), rev(cl, bwd), hc)
        ys_c.append(rev(yc, bwd))
        ys_l.append(rev(yl, bwd))

    def finish(y, xs, z):
        y = y + d_skip[:, None] * xs
        y = y.reshape(y.shape[0], y.shape[1], SSM_D_INNER)
        return rmsnorm(y * jax.nn.silu(z), norm_w)

    y_l = finish(ys_l[0] + ys_l[1], xl, z_l)
    y_c = finish(ys_c[0] + ys_c[1], xc, z_c) if need_ctx else None
    return y_c, y_l


def token_mixers(p_c, p_l, row, col, lam_init, need_ctx, da_lambda, da_subln, gqa_q_norm, gqa_k_norm,
                 lru_conv_w, lru_conv_b, lru_w_gates, lru_b_gates, lru_lambda,
                 ssm_conv_w, ssm_conv_b, ssm_dt_bias, ssm_a_log, ssm_d, ssm_norm):
    pc = split_columns(p_c)
    pl = split_columns(p_l)
    ya_c, ya_l = diff_attention_mixer(pc[0], pc[1], pc[2], pl[0], pl[1], pl[2], row, col,
                                      da_lambda, da_subln, lam_init, need_ctx)
    yb_c, yb_l = gqa_mixer(pc[3], pc[4], pc[5], pl[3], pl[4], pl[5], row, col, gqa_q_norm, gqa_k_norm, need_ctx)
    yc_c, yc_l = rglru_mixer(pc[6], pc[7], pl[6], pl[7], lru_conv_w, lru_conv_b, lru_w_gates, lru_b_gates,
                             lru_lambda, need_ctx)
    yd_c, yd_l = ssd_mixer(pc[8], pc[9], pc[10], pl[8], pl[9], pl[10], ssm_conv_w, ssm_conv_b,
                           ssm_dt_bias, ssm_a_log, ssm_d, ssm_norm, need_ctx)
    y_l = jnp.concatenate([ya_l, yb_l, yc_l, yd_l], axis=-1)
    y_c = jnp.concatenate([ya_c, yb_c, yc_c, yd_c], axis=-1) if need_ctx else None
    return y_c, y_l


def conv_ffn(h, w_up, conv_w, conv_b, w_down):
    gate, up = jnp.split(h @ w_up, 2, axis=-1)
    gate = dwconv_centred(gate, conv_w, conv_b)
    return (jax.nn.silu(gate) * up) @ w_down


def setup_inputs(seed: int = 0) -> dict:
    key = jax.random.key(seed)
    ks = jax.random.split(key, 32)
    f32 = jnp.float32

    def nrm(k, shape, scale):
        return scale * jax.random.normal(k, shape, f32)

    def gain(k, shape):
        return 1.0 + 0.1 * jax.random.normal(k, shape, f32)

    bw = LRU_WIDTH // LRU_BLOCKS
    u = jax.random.uniform(ks[17], (DEPTH, 2, LRU_WIDTH), f32, 0.9, 0.999)
    a_base = u ** (1.0 / RGLRU_C)
    dt0 = jnp.exp(jax.random.uniform(ks[22], (DEPTH, 2, SSM_HEADS), f32, math.log(1e-3), math.log(1e-1)))
    return {
        'x': nrm(ks[0], (BATCH, SEQ, D_MODEL), 1.0),
        'c': nrm(ks[1], (BATCH, D_MODEL), 1.0),
        'ctx': nrm(ks[2], (BATCH, CTX_LEN, D_MODEL), 1.0),
        'c_ctx': nrm(ks[3], (D_MODEL,), 1.0),
        'w_mod': nrm(ks[4], (DEPTH, D_MODEL, 6 * D_MODEL), 0.5 * D_MODEL ** -0.5),
        'b_mod': nrm(ks[5], (DEPTH, 6 * D_MODEL), 0.01),
        'mix_norm': gain(ks[6], (DEPTH, D_MODEL)),
        'ffn_norm': gain(ks[7], (DEPTH, D_MODEL)),
        'w_in': nrm(ks[8], (DEPTH, D_MODEL, IN_COLS), D_MODEL ** -0.5),
        'w_out': nrm(ks[9], (DEPTH, MIX_WIDTH, D_MODEL), MIX_WIDTH ** -0.5),
        'da_lambda': nrm(ks[10], (DEPTH, 4, DA_HEAD_DIM), 0.1),
        'da_subln': gain(ks[11], (DEPTH, 2 * DA_HEAD_DIM)),
        'gqa_q_norm': gain(ks[12], (DEPTH, GQA_HEAD_DIM)),
        'gqa_k_norm': gain(ks[13], (DEPTH, GQA_HEAD_DIM)),
        'lru_conv_w': nrm(ks[14], (DEPTH, LRU_CONV, LRU_WIDTH), LRU_CONV ** -0.5),
        'lru_conv_b': nrm(ks[15], (DEPTH, LRU_WIDTH), 0.01),
        'lru_w_gates': nrm(ks[16], (DEPTH, 2, 2, LRU_BLOCKS, bw, bw), bw ** -0.5),
        'lru_b_gates': nrm(ks[18], (DEPTH, 2, 2, LRU_WIDTH), 0.01),
        'lru_lambda': jnp.log(a_base) - jnp.log1p(-a_base),
        'ssm_conv_w': nrm(ks[19], (DEPTH, SSM_CONV, SSM_CONV_DIM), SSM_CONV ** -0.5),
        'ssm_conv_b': nrm(ks[20], (DEPTH, SSM_CONV_DIM), 0.01),
        'ssm_dt_bias': dt0 + jnp.log(-jnp.expm1(-dt0)),
        'ssm_a_log': jnp.log(jax.random.uniform(ks[21], (DEPTH, 2, SSM_HEADS), f32, 1.0, 16.0)),
        'ssm_d': gain(ks[23], (DEPTH, SSM_HEADS)),
        'ssm_norm': gain(ks[24], (DEPTH, SSM_D_INNER)),
        'ffn_w_up': nrm(ks[25], (DEPTH, D_MODEL, 2 * D_FF), D_MODEL ** -0.5),
        'ffn_conv_w': nrm(ks[26], (DEPTH, FFN_CONV, D_FF), FFN_CONV ** -0.5),
        'ffn_conv_b': nrm(ks[27], (DEPTH, D_FF), 0.01),
        'ffn_w_down': nrm(ks[28], (DEPTH, D_FF, D_MODEL), D_FF ** -0.5),
        'final_norm': gain(ks[29], (D_MODEL,)),
    }


def reference(x, c, ctx, c_ctx, w_mod, b_mod, mix_norm, ffn_norm, w_in, w_out, da_lambda, da_subln,
              gqa_q_norm, gqa_k_norm, lru_conv_w, lru_conv_b, lru_w_gates, lru_b_gates, lru_lambda,
              ssm_conv_w, ssm_conv_b, ssm_dt_bias, ssm_a_log, ssm_d, ssm_norm,
              ffn_w_up, ffn_conv_w, ffn_conv_b, ffn_w_down, final_norm):
    n_tok = x.shape[1]
    ROWS = n_tok // GRID_W
    row = jnp.repeat(jnp.arange(ROWS, dtype=jnp.int32), GRID_W)
    col = jnp.tile(jnp.arange(GRID_W, dtype=jnp.int32), ROWS)

    for layer in range(DEPTH):
        need_ctx = layer < DEPTH - 1
        lam_init = 0.8 - 0.6 * math.exp(-0.3 * layer)
        sh_m, sc_m, g_m, sh_f, sc_f, g_f = [t[:, None, :] for t in modulation(c, w_mod[layer], b_mod[layer])]
        csh_m, csc_m, cg_m, csh_f, csc_f, cg_f = modulation(c_ctx, w_mod[layer], b_mod[layer])

        p_l = modulate(rmsnorm(x, mix_norm[layer]), sh_m, sc_m) @ w_in[layer]
        p_c = modulate(rmsnorm(ctx, mix_norm[layer]), csh_m, csc_m) @ w_in[layer]
        y_c, y_l = token_mixers(p_c, p_l, row, col, lam_init, need_ctx,
                                da_lambda[layer], da_subln[layer], gqa_q_norm[layer], gqa_k_norm[layer],
                                lru_conv_w[layer], lru_conv_b[layer], lru_w_gates[layer], lru_b_gates[layer],
                                lru_lambda[layer], ssm_conv_w[layer], ssm_conv_b[layer], ssm_dt_bias[layer],
                                ssm_a_log[layer], ssm_d[layer], ssm_norm[layer])
        x = x + g_m * (y_l @ w_out[layer])
        x = x + g_f * conv_ffn(modulate(rmsnorm(x, ffn_norm[layer]), sh_f, sc_f),
                               ffn_w_up[layer], ffn_conv_w[layer], ffn_conv_b[layer], ffn_w_down[layer])
        if need_ctx:
            ctx = ctx + cg_m * (y_c @ w_out[layer])
            ctx = ctx + cg_f * conv_ffn(modulate(rmsnorm(ctx, ffn_norm[layer]), csh_f, csc_f),
                                        ffn_w_up[layer], ffn_conv_w[layer], ffn_conv_b[layer], ffn_w_down[layer])

    return rmsnorm(x, final_norm)
```

```python
import functools
import math

import jax
import jax.numpy as jnp
from jax import lax
from jax.experimental import pallas as pl
from jax.experimental.pallas import tpu as pltpu

F32 = jnp.float32
BF16 = jnp.bfloat16

D_MODEL = 2048
DEPTH = 4
GRID_W = 64
GROUP_WIDTH = 512
DA_HEAD_DIM = 64
DA_HEADS = 4
GQA_HEAD_DIM = 128
GQA_HEADS = 4
GQA_KV_HEADS = 2
LRU_WIDTH = 512
LRU_BLOCKS = 4
RGLRU_C = 8.0
SSM_D_INNER = 512
SSM_HEAD_DIM = 64
SSM_HEADS = 8
SSM_GROUPS = 2
SSM_D_STATE = 128
SSD_CHUNK = 128
D_FF = 5632
ROPE_THETA = 10000.0
NORM_EPS = 1e-6

LANES = 128
HALO = 16
IN_COLS = 5136
IN_COLS_PAD = 5376
ATTN_COLS = 2560
QKV_COLS = 3072
VMEM_LIMIT = 56 * 1024 * 1024


def _cparams(*sem):
    return pltpu.CompilerParams(dimension_semantics=sem, vmem_limit_bytes=VMEM_LIMIT)


def _silu(x):
    return x * jax.nn.sigmoid(x)


def _norm_mod(x, nw, shift, scale):
    y = x * lax.rsqrt(jnp.mean(x * x, axis=-1, keepdims=True) + NORM_EPS) * nw
    return y * (1.0 + scale) + shift


def _mod_kernel(c_ref, w_ref, b_ref, o_ref):
    s = _silu(c_ref[...]).astype(BF16)
    o_ref[...] = jnp.dot(s, w_ref[...].astype(BF16), preferred_element_type=F32) + b_ref[...]


def modulation_all(cond8, w_mod, b_mod):
    depth, d, n = w_mod.shape
    tn = 1024
    return pl.pallas_call(
        _mod_kernel,
        grid=(depth, n // tn),
        in_specs=[
            pl.BlockSpec((8, d), lambda l, j: (0, 0)),
            pl.BlockSpec((None, d, tn), lambda l, j: (l, 0, j)),
            pl.BlockSpec((None, 1, tn), lambda l, j: (l, 0, j)),
        ],
        out_specs=pl.BlockSpec((None, 8, tn), lambda l, j: (l, 0, j)),
        out_shape=jax.ShapeDtypeStruct((depth, 8, n), F32),
        compiler_params=_cparams("parallel", "parallel"),
        name="modulation",
    )(cond8, w_mod, b_mod.reshape(depth, 1, n))


def _inproj_kernel(x_ref, nw_ref, sh_ref, sc_ref, w_ref, o_ref, xn_ref):
    @pl.when(pl.program_id(1) == 0)
    def _():
        xn_ref[...] = _norm_mod(x_ref[...], nw_ref[...], sh_ref[...], sc_ref[...]).astype(BF16)

    o_ref[...] = jnp.dot(xn_ref[...], w_ref[...], preferred_element_type=F32)


def in_proj(x, nw, shift, scale, w_bf):
    m, d = x.shape
    n = w_bf.shape[1]
    tm = min(m, 1024)
    tn = 768
    vec = pl.BlockSpec((1, d), lambda i, j: (0, 0))
    return pl.pallas_call(
        _inproj_kernel,
        grid=(m // tm, n // tn),
        in_specs=[pl.BlockSpec((tm, d), lambda i, j: (i, 0)), vec, vec, vec,
                  pl.BlockSpec((d, tn), lambda i, j: (0, j))],
        out_specs=pl.BlockSpec((tm, tn), lambda i, j: (i, j)),
        out_shape=jax.ShapeDtypeStruct((m, n), F32),
        scratch_shapes=[pltpu.VMEM((tm, d), BF16)],
        compiler_params=_cparams("parallel", "arbitrary"),
        name="in_proj",
    )(x, nw, shift, scale, w_bf)


def _resid_kernel(y_ref, w_ref, x_ref, g_ref, o_ref):
    o_ref[...] = x_ref[...] + g_ref[...] * jnp.dot(y_ref[...], w_ref[...], preferred_element_type=F32)


def resid_proj(y_bf, w_bf, x, gate):
    m, k = y_bf.shape
    n = w_bf.shape[1]
    tm = min(m, 512)
    tn = 1024
    return pl.pallas_call(
        _resid_kernel,
        grid=(n // tn, m // tm),
        in_specs=[pl.BlockSpec((tm, k), lambda j, i: (i, 0)),
                  pl.BlockSpec((k, tn), lambda j, i: (0, j)),
                  pl.BlockSpec((tm, tn), lambda j, i: (i, j)),
                  pl.BlockSpec((1, tn), lambda j, i: (0, j))],
        out_specs=pl.BlockSpec((tm, tn), lambda j, i: (i, j)),
        out_shape=jax.ShapeDtypeStruct((m, n), F32),
        compiler_params=_cparams("parallel", "parallel"),
        name="resid_proj",
    )(y_bf, w_bf, x, gate)


def _ffn_up_kernel(xp_ref, x_ref, xx_ref, nw_ref, sh_ref, sc_ref, wg_ref, wu_ref, cw_ref, cb_ref,
                   o_ref, xn_ref, g_ref, *, tm, n_row_blocks):
    i = pl.program_id(0)

    @pl.when(pl.program_id(1) == 0)
    def _():
        def nm(v):
            return _norm_mod(v, nw_ref[...], sh_ref[...], sc_ref[...])

        prev = jnp.where(i == 0, 0.0, nm(xp_ref[...]))
        nxt = jnp.where(i == n_row_blocks - 1, 0.0, nm(xx_ref[...]))
        xn_ref[pl.ds(0, HALO), :] = prev.astype(BF16)
        xn_ref[pl.ds(HALO, tm), :] = nm(x_ref[...]).astype(BF16)
        xn_ref[pl.ds(HALO + tm, HALO), :] = nxt.astype(BF16)

    g_ref[...] = jnp.dot(xn_ref[...], wg_ref[...], preferred_element_type=F32)
    up = jnp.dot(xn_ref[pl.ds(HALO, tm), :], wu_ref[...], preferred_element_type=F32)
    cw = cw_ref[...]
    gc = (cb_ref[...] + cw[0:1] * g_ref[pl.ds(HALO - 1, tm), :] + cw[1:2] * g_ref[pl.ds(HALO, tm), :]
          + cw[2:3] * g_ref[pl.ds(HALO + 1, tm), :])
    o_ref[...] = (_silu(gc) * up).astype(BF16)


def ffn_up(x, nw, shift, scale, w_up_bf, conv_w, conv_b):
    m, d = x.shape
    dff = w_up_bf.shape[1] // 2
    tm = min(m, 1024)
    tn = 512
    nrb = m // tm
    hb = tm // HALO
    nhb = m // HALO
    vec = pl.BlockSpec((1, d), lambda i, j: (0, 0))
    return pl.pallas_call(
        functools.partial(_ffn_up_kernel, tm=tm, n_row_blocks=nrb),
        grid=(nrb, dff // tn),
        in_specs=[pl.BlockSpec((HALO, d), lambda i, j: (jnp.maximum(i * hb - 1, 0), 0)),
                  pl.BlockSpec((tm, d), lambda i, j: (i, 0)),
                  pl.BlockSpec((HALO, d), lambda i, j: (jnp.minimum((i + 1) * hb, nhb - 1), 0)),
                  vec, vec, vec,
                  pl.BlockSpec((d, tn), lambda i, j: (0, j)),
                  pl.BlockSpec((d, tn), lambda i, j: (0, j + dff // tn)),
                  pl.BlockSpec((3, tn), lambda i, j: (0, j)),
                  pl.BlockSpec((1, tn), lambda i, j: (0, j))],
        out_specs=pl.BlockSpec((tm, tn), lambda i, j: (i, j)),
        out_shape=jax.ShapeDtypeStruct((m, dff), BF16),
        scratch_shapes=[pltpu.VMEM((tm + 2 * HALO, d), BF16), pltpu.VMEM((tm + 2 * HALO, tn), F32)],
        compiler_params=_cparams("parallel", "arbitrary"),
        name="ffn_up",
    )(x, x, x, nw, shift, scale, w_up_bf, w_up_bf, conv_w, conv_b.reshape(1, dff))


def _rmsnorm_kernel(x_ref, w_ref, o_ref):
    x = x_ref[...]
    o_ref[...] = x * lax.rsqrt(jnp.mean(x * x, axis=-1, keepdims=True) + NORM_EPS) * w_ref[...]


def final_rmsnorm(x, w):
    m, d = x.shape
    tm = min(m, 512)
    return pl.pallas_call(
        _rmsnorm_kernel,
        grid=(m // tm,),
        in_specs=[pl.BlockSpec((tm, d), lambda i: (i, 0)), pl.BlockSpec((1, d), lambda i: (0, 0))],
        out_specs=pl.BlockSpec((tm, d), lambda i: (i, 0)),
        out_shape=jax.ShapeDtypeStruct((m, d), F32),
        compiler_params=_cparams("parallel"),
        name="final_norm",
    )(x, w.reshape(1, d))


def _rope_tables(n_tok, head_dim):
    q = head_dim // 4
    t = jnp.arange(n_tok, dtype=jnp.int32)
    row = (t // GRID_W).astype(F32)[:, None]
    col = (t % GRID_W).astype(F32)[:, None]
    lane = jnp.arange(LANES, dtype=jnp.int32)
    u = lane % head_dim
    region = u // q
    freqs = jnp.power(ROPE_THETA, -(u % q).astype(F32) / q)[None, :]
    ang = jnp.where((region < 2)[None, :], row, col) * freqs
    cos, sin = jnp.cos(ang), jnp.sin(ang)
    first = (region % 2 == 0)[None, :]
    return cos, jnp.where(first, -sin, 0.0), jnp.where(first, 0.0, sin)


def _prep_kernel(p_ref, cd_ref, ad_ref, bd_ref, cg_ref, ag_ref, bg_ref, qn_ref, kn_ref, o_ref, *, rope):
    lane = lax.broadcasted_iota(jnp.int32, (1, LANES), 1)
    lo = lane < DA_HEAD_DIM

    def blk(b):
        return p_ref[:, b * LANES:(b + 1) * LANES]

    def put(b, v):
        o_ref[:, b * LANES:(b + 1) * LANES] = v.astype(BF16)

    def rot(x, cos, sa, sb, quarter):
        if not rope:
            return x
        return (x * cos[...] + pltpu.roll(x, LANES - quarter, 1) * sa[...]
                + pltpu.roll(x, quarter, 1) * sb[...])

    def rms(x, w):
        return x * lax.rsqrt(jnp.mean(x * x, axis=-1, keepdims=True) + NORM_EPS) * w[...]

    qd = DA_HEAD_DIM // 4
    qg = GQA_HEAD_DIM // 4
    for h in range(DA_HEADS):
        q = rot(blk(h), cd_ref, ad_ref, bd_ref, qd) * (DA_HEAD_DIM ** -0.5)
        put(h, jnp.where(lo, q, 0.0))
        put(4 + h, jnp.where(lo, 0.0, q))
        put(8 + h, rot(blk(4 + h), cd_ref, ad_ref, bd_ref, qd))
        put(12 + h, blk(8 + h))
    for h in range(GQA_HEADS):
        q = rot(rms(blk(12 + h), qn_ref), cg_ref, ag_ref, bg_ref, qg) * (GQA_HEAD_DIM ** -0.5)
        put(16 + h, q)
    for h in range(GQA_KV_HEADS):
        put(20 + h, rot(rms(blk(16 + h), kn_ref), cg_ref, ag_ref, bg_ref, qg))
        put(22 + h, blk(18 + h))


def attn_prep(p, tables, q_norm, k_norm, rope):
    m = p.shape[0]
    tm = min(m, 512)
    tab = pl.BlockSpec((tm, LANES), lambda i: (i, 0))
    vec = pl.BlockSpec((1, LANES), lambda i: (0, 0))
    return pl.pallas_call(
        functools.partial(_prep_kernel, rope=rope),
        grid=(m // tm,),
        in_specs=[pl.BlockSpec((tm, ATTN_COLS), lambda i: (i, 0))] + [tab] * 6 + [vec, vec],
        out_specs=pl.BlockSpec((tm, QKV_COLS), lambda i: (i, 0)),
        out_shape=jax.ShapeDtypeStruct((m, QKV_COLS), BF16),
        compiler_params=_cparams("parallel"),
        name="attn_prep",
    )(p, *tables, q_norm.reshape(1, LANES), k_norm.reshape(1, LANES))


def _attn_kernel(*refs, tk, n_kv, diff, lam_init):
    if diff:
        qa_ref, qb_ref, k_ref, v_ref, lam_ref, sw_ref, o_ref = refs
    else:
        qa_ref, qb_ref, k_ref, v_ref, o_ref = refs
    qs = (qa_ref[...], qb_ref[...])
    tq = qs[0].shape[0]

    def body(c, carry):
        off = pl.multiple_of(c * tk, tk)
        ks = k_ref[pl.ds(off, tk), :]
        vs = v_ref[pl.ds(off, tk), :]
        out = []
        for q, (m, l, acc) in zip(qs, carry):
            s = lax.dot_general(q, ks, (((1,), (1,)), ((), ())), preferred_element_type=F32)
            m_new = jnp.maximum(m, jnp.max(s, axis=-1, keepdims=True))
            alpha = jnp.exp(m - m_new)
            p = jnp.exp(s - m_new)
            l = alpha * l + jnp.sum(p, axis=-1, keepdims=True)
            acc = alpha * acc + jnp.dot(p.astype(BF16), vs, preferred_element_type=F32)
            out.append((m_new, l, acc))
        return tuple(out)

    init = tuple((jnp.full((tq, 1), -1e30, F32), jnp.zeros((tq, 1), F32), jnp.zeros((tq, LANES), F32))
                 for _ in range(2))
    (_, la, acca), (_, lb, accb) = lax.fori_loop(0, n_kv, body, init)
    oa = acca / la
    ob = accb / lb
    if diff:
        lv = lam_ref[...]
        lam = (jnp.exp(jnp.sum(lv[0:1] * lv[1:2], keepdims=True))
               - jnp.exp(jnp.sum(lv[2:3] * lv[3:4], keepdims=True)) + lam_init)
        o = oa - lam * ob
        o = o * lax.rsqrt(jnp.mean(o * o, axis=-1, keepdims=True) + NORM_EPS) * sw_ref[...]
        o_ref[...] = (o * (1.0 - lam_init)).astype(BF16)
    else:
        o_ref[:, 0:LANES] = oa.astype(BF16)
        o_ref[:, LANES:2 * LANES] = ob.astype(BF16)


def _kv_tile(t):
    for tk in (768, 512, 256, 128):
        if t % tk == 0:
            return tk
    raise ValueError(f"unsupported key count {t}")


def diff_attention(qkv_q, qkv_kv, da_lambda, subln_w, lam_init):
    sq, t = qkv_q.shape[0], qkv_kv.shape[0]
    tq = min(sq, 256)
    tk = _kv_tile(t)
    return pl.pallas_call(
        functools.partial(_attn_kernel, tk=tk, n_kv=t // tk, diff=True, lam_init=lam_init),
        grid=(DA_HEADS, sq // tq),
        in_specs=[pl.BlockSpec((tq, LANES), lambda h, i: (i, h)),
                  pl.BlockSpec((tq, LANES), lambda h, i: (i, 4 + h)),
                  pl.BlockSpec((t, LANES), lambda h, i: (0, 8 + h)),
                  pl.BlockSpec((t, LANES), lambda h, i: (0, 12 + h)),
                  pl.BlockSpec((4, DA_HEAD_DIM), lambda h, i: (0, 0)),
                  pl.BlockSpec((1, LANES), lambda h, i: (0, 0))],
        out_specs=pl.BlockSpec((tq, LANES), lambda h, i: (i, h)),
        out_shape=jax.ShapeDtypeStruct((sq, GROUP_WIDTH), BF16),
        compiler_params=_cparams("parallel", "parallel"),
        name="diff_attn",
    )(qkv_q, qkv_q, qkv_kv, qkv_kv, da_lambda, subln_w.reshape(1, LANES))


def gqa_attention(qkv_q, qkv_kv):
    sq, t = qkv_q.shape[0], qkv_kv.shape[0]
    tq = min(sq, 256)
    tk = _kv_tile(t)
    return pl.pallas_call(
        functools.partial(_attn_kernel, tk=tk, n_kv=t // tk, diff=False, lam_init=0.0),
        grid=(GQA_KV_HEADS, sq // tq),
        in_specs=[pl.BlockSpec((tq, LANES), lambda h, i: (i, 16 + 2 * h)),
                  pl.BlockSpec((tq, LANES), lambda h, i: (i, 17 + 2 * h)),
                  pl.BlockSpec((t, LANES), lambda h, i: (0, 20 + h)),
                  pl.BlockSpec((t, LANES), lambda h, i: (0, 22 + h))],
        out_specs=pl.BlockSpec((tq, 2 * LANES), lambda h, i: (i, h)),
        out_shape=jax.ShapeDtypeStruct((sq, GROUP_WIDTH), BF16),
        compiler_params=_cparams("parallel", "parallel"),
        name="gqa_attn",
    )(qkv_q, qkv_q, qkv_kv, qkv_kv)


CONV_CHUNK = 256


def _conv_chunk(x_ref, base, n_rows, chunk, cw, cb):
    cur = x_ref[pl.ds(base, chunk), :]
    prev = x_ref[pl.ds(pl.multiple_of(jnp.maximum(base - 8, 0), 8), 8), :]
    nxt = x_ref[pl.ds(pl.multiple_of(jnp.minimum(base + chunk, n_rows - 8), 8), 8), :]
    prev = jnp.where(base == 0, 0.0, prev)
    nxt = jnp.where(base + chunk >= n_rows, 0.0, nxt)
    cat = jnp.concatenate([prev, cur, nxt], axis=0)
    n = chunk + 16
    xm2 = pltpu.roll(cat, 2, 0)[8:8 + chunk]
    xm1 = pltpu.roll(cat, 1, 0)[8:8 + chunk]
    xp1 = pltpu.roll(cat, n - 1, 0)[8:8 + chunk]
    return cb + cw[0:1] * xm2 + cw[1:2] * xm1 + cw[2:3] * cur + cw[3:4] * xp1


def _dwconv_kernel(x_ref, w_ref, b_ref, o_ref, *, n_rows, chunk, act):
    cw = w_ref[...]
    cb = b_ref[...]

    def body(c, _):
        base = pl.multiple_of(c * chunk, chunk)
        y = _conv_chunk(x_ref, base, n_rows, chunk, cw, cb)
        o_ref[pl.ds(base, chunk), :] = _silu(y) if act else y
        return 0

    lax.fori_loop(0, n_rows // chunk, body, 0)


def dwconv(p, col0, width, conv_w, conv_b, act):
    n_rows = p.shape[0]
    chunk = min(n_rows, CONV_CHUNK)
    b0 = col0 // LANES
    return pl.pallas_call(
        functools.partial(_dwconv_kernel, n_rows=n_rows, chunk=chunk, act=act),
        grid=(width // LANES,),
        in_specs=[pl.BlockSpec((n_rows, LANES), lambda j: (0, b0 + j)),
                  pl.BlockSpec((4, LANES), lambda j: (0, j)),
                  pl.BlockSpec((1, LANES), lambda j: (0, j))],
        out_specs=pl.BlockSpec((n_rows, LANES), lambda j: (0, j)),
        out_shape=jax.ShapeDtypeStruct((n_rows, width), F32),
        compiler_params=_cparams("parallel"),
        name="dwconv",
    )(p, conv_w, conv_b.reshape(1, width))


LRU_CHUNK = 256


def _lru_scan_chunk(a, u, h_in, reverse):
    n = a.shape[0]
    row = lax.broadcasted_iota(jnp.int32, (n, 1), 0)
    k = 1
    while k < n:
        if reverse:
            keep = row < n - k
            a_s = jnp.where(keep, pltpu.roll(a, n - k, 0), 1.0)
            u_s = jnp.where(keep, pltpu.roll(u, n - k, 0), 0.0)
        else:
            keep = row >= k
            a_s = jnp.where(keep, pltpu.roll(a, k, 0), 1.0)
            u_s = jnp.where(keep, pltpu.roll(u, k, 0), 0.0)
        u = u + a * u_s
        a = a * a_s
        k *= 2
    return u + a * h_in


def _lru_kernel(xc_ref, g_ref, w_ref, b_ref, lam_ref, h0_ref, y_ref, hT_ref, hf_ref, *, n_rows, chunk):
    n_chunks = n_rows // chunk

    def direction(d, reverse, finish):
        w_r = w_ref[d, 0].astype(BF16)
        w_i = w_ref[d, 1].astype(BF16)
        b_r = b_ref[d, 0:1, :]
        b_i = b_ref[d, 1:2, :]
        lam = lam_ref[d:d + 1, :]
        log_sig = jnp.minimum(lam, 0.0) - jnp.log1p(jnp.exp(-jnp.abs(lam)))

        def body(c, h):
            cc = (n_chunks - 1 - c) if reverse else c
            base = pl.multiple_of(cc * chunk, chunk)
            x = xc_ref[pl.ds(base, chunk), :]
            xb = x.astype(BF16)
            r = jax.nn.sigmoid(jnp.dot(xb, w_r, preferred_element_type=F32) + b_r)
            i = jax.nn.sigmoid(jnp.dot(xb, w_i, preferred_element_type=F32) + b_i)
            log_a = RGLRU_C * r * log_sig
            a = jnp.exp(log_a)
            u = jnp.sqrt(1.0 - a * a) * (i * x)
            hs = _lru_scan_chunk(a, u, h, reverse)
            finish(base, hs)
            return hs[0:1] if reverse else hs[chunk - 1:chunk]

        return lax.fori_loop(0, n_chunks, body, h0_ref[d:d + 1, :])

    def keep_fwd(base, hs):
        hf_ref[pl.ds(base, chunk), :] = hs

    def emit(base, hs):
        g = g_ref[pl.ds(base, chunk), :]
        y_ref[pl.ds(base, chunk), :] = ((hf_ref[pl.ds(base, chunk), :] + hs) * jax.nn.gelu(g)).astype(BF16)

    hT_ref[0:1, :] = direction(0, False, keep_fwd)
    hT_ref[1:2, :] = direction(1, True, emit)


def rglru(xc, p, g_col0, w_gates, b_gates, lam, h0):
    n_rows = xc.shape[0]
    chunk = min(n_rows, LRU_CHUNK)
    gb = g_col0 // LANES
    return pl.pallas_call(
        functools.partial(_lru_kernel, n_rows=n_rows, chunk=chunk),
        grid=(LRU_BLOCKS,),
        in_specs=[pl.BlockSpec((n_rows, LANES), lambda j: (0, j)),
                  pl.BlockSpec((n_rows, LANES), lambda j: (0, gb + j)),
                  pl.BlockSpec((2, 2, None, LANES, LANES), lambda j: (0, 0, j, 0, 0)),
                  pl.BlockSpec((2, 2, LANES), lambda j: (0, 0, j)),
                  pl.BlockSpec((2, LANES), lambda j: (0, j)),
                  pl.BlockSpec((2, LANES), lambda j: (0, j))],
        out_specs=[pl.BlockSpec((n_rows, LANES), lambda j: (0, j)),
                   pl.BlockSpec((2, LANES), lambda j: (0, j))],
        out_shape=[jax.ShapeDtypeStruct((n_rows, LRU_WIDTH), BF16),
                   jax.ShapeDtypeStruct((2, LRU_WIDTH), F32)],
        scratch_shapes=[pltpu.VMEM((n_rows, LANES), F32)],
        compiler_params=_cparams("parallel"),
        name="rglru",
    )(xc, p, w_gates, b_gates, lam, h0)


def _ssd_direction(xs, bm, cm, dt_raw, dtT_raw, bias_row, bias_col, a_row, a_col, expand, st_ref, reverse):
    q = SSD_CHUNK
    hi = lax.Precision.HIGHEST
    li = lax.broadcasted_iota(jnp.int32, (q, q), 0)
    si = lax.broadcasted_iota(jnp.int32, (q, q), 1)
    causal = (si >= li) if reverse else (si <= li)
    tri = causal.astype(F32)
    dt = jax.nn.softplus(dt_raw + bias_row)
    dtT = jax.nn.softplus(dtT_raw + bias_col)
    adt = dt * a_row
    adtT = dtT * a_col
    cum = jnp.dot(tri, adt, preferred_element_type=F32, precision=hi)
    cumT = lax.dot_general(adtT, tri, (((1,), (1,)), ((), ())), preferred_element_type=F32,
                           precision=hi)
    edge = cum[0:1] if reverse else cum[q - 1:q]

    def ex(v):
        return jnp.dot(v, expand, preferred_element_type=F32, precision=hi)

    cum_e = ex(cum)
    edge_e = ex(edge)
    xdt = xs * ex(dt)
    xdd = (xdt * jnp.exp(edge_e - cum_e)).astype(BF16)
    xdt_b = xdt.astype(BF16)
    grow = jnp.exp(cum_e)
    ys = []
    for g in range(SSM_GROUPS):
        gsl = slice(g * 256, (g + 1) * 256)
        b_g = bm[:, g * SSM_D_STATE:(g + 1) * SSM_D_STATE]
        c_g = cm[:, g * SSM_D_STATE:(g + 1) * SSM_D_STATE].astype(BF16)
        b_gt = b_g.T.astype(BF16)
        gram = jnp.dot(c_g, b_gt, preferred_element_type=F32)
        st = st_ref[:, gsl]
        y_off = grow[:, gsl] * jnp.dot(c_g, st.astype(BF16), preferred_element_type=F32)
        st_ref[:, gsl] = jnp.exp(edge_e[:, gsl]) * st + jnp.dot(b_gt, xdd[:, gsl], preferred_element_type=F32)
        for hh in range(SSM_HEADS // SSM_GROUPS):
            h = g * (SSM_HEADS // SSM_GROUPS) + hh
            seg = cum[:, h:h + 1] - cumT[h:h + 1, :]
            decay = jnp.exp(jnp.where(causal, seg, -jnp.inf))
            m = (gram * decay).astype(BF16)
            hs = slice(h * SSM_HEAD_DIM, (h + 1) * SSM_HEAD_DIM)
            y_d = jnp.dot(m, xdt_b[:, hs], preferred_element_type=F32)
            ys.append(y_d + y_off[:, hh * SSM_HEAD_DIM:(hh + 1) * SSM_HEAD_DIM])
    return jnp.concatenate(ys, axis=1)


def _ssd_kernel(xf_ref, bf_ref, cf_ref, dtf_ref, dtTf_ref, xb_ref, bb_ref, cb_ref, dtb_ref, dtTb_ref,
                bias_ref, biasT_ref, alog_ref, alogT_ref, st0_ref, yf_ref, yb_ref, stT_ref, st_ref):
    c = pl.program_id(0)

    @pl.when(c == 0)
    def _():
        st_ref[...] = st0_ref[...]

    hrow = lax.broadcasted_iota(jnp.int32, (SSM_HEADS, SSM_D_INNER), 0)
    hcol = lax.broadcasted_iota(jnp.int32, (SSM_HEADS, SSM_D_INNER), 1) // SSM_HEAD_DIM
    expand = (hrow == hcol).astype(F32)
    a_row = -jnp.exp(alog_ref[...])
    a_col = -jnp.exp(alogT_ref[...])
    yf_ref[...] = _ssd_direction(xf_ref[...], bf_ref[...], cf_ref[...], dtf_ref[:, 0:SSM_HEADS],
                                 dtTf_ref[0:SSM_HEADS, :], bias_ref[0:1, :], biasT_ref[:, 0:1],
                                 a_row[0:1, :], a_col[:, 0:1], expand, st_ref.at[0], False)
    yb_ref[...] = _ssd_direction(xb_ref[...], bb_ref[...], cb_ref[...], dtb_ref[:, SSM_HEADS:2 * SSM_HEADS],
                                 dtTb_ref[SSM_HEADS:2 * SSM_HEADS, :], bias_ref[1:2, :], biasT_ref[:, 1:2],
                                 a_row[1:2, :], a_col[:, 1:2], expand, st_ref.at[1], True)

    @pl.when(c == pl.num_programs(0) - 1)
    def _():
        stT_ref[...] = st_ref[...]


def ssd_scan(xbc, dt, dtT, dt_bias, a_log, st0):
    n_rows = xbc.shape[0]
    q = SSD_CHUNK
    nc = n_rows // q
    xb = SSM_D_INNER // q
    fwd = lambda c: c
    bwd = lambda c: nc - 1 - c

    def specs(ix):
        return [pl.BlockSpec((q, SSM_D_INNER), lambda c: (ix(c), 0)),
                pl.BlockSpec((q, 2 * SSM_D_STATE), lambda c: (ix(c), 2)),
                pl.BlockSpec((q, 2 * SSM_D_STATE), lambda c: (ix(c), 3)),
                pl.BlockSpec((q, 2 * SSM_HEADS), lambda c: (ix(c), 0)),
                pl.BlockSpec((2 * SSM_HEADS, q), lambda c: (0, ix(c)))]

    small = lambda shape: pl.BlockSpec(shape, lambda c: (0,) * len(shape))
    st_shape = (2, SSM_D_STATE, SSM_D_INNER)
    return pl.pallas_call(
        _ssd_kernel,
        grid=(nc,),
        in_specs=specs(fwd) + specs(bwd) + [small((2, SSM_HEADS)), small((SSM_HEADS, 2)),
                                            small((2, SSM_HEADS)), small((SSM_HEADS, 2)), small(st_shape)],
        out_specs=[pl.BlockSpec((q, SSM_D_INNER), lambda c: (c, 0)),
                   pl.BlockSpec((q, SSM_D_INNER), lambda c: (nc - 1 - c, 0)),
                   small(st_shape)],
        out_shape=[jax.ShapeDtypeStruct((n_rows, SSM_D_INNER), F32),
                   jax.ShapeDtypeStruct((n_rows, SSM_D_INNER), F32),
                   jax.ShapeDtypeStruct(st_shape, F32)],
        scratch_shapes=[pltpu.VMEM(st_shape, F32)],
        compiler_params=_cparams("arbitrary"),
        name="ssd_scan",
    )(xbc, xbc, xbc, dt, dtT, xbc, xbc, xbc, dt, dtT, dt_bias, dt_bias.T, a_log, a_log.T, st0)


def _ssd_finish_kernel(yf_ref, yb_ref, xs_ref, z_ref, d_ref, nw_ref, o_ref):
    hrow = lax.broadcasted_iota(jnp.int32, (SSM_HEADS, SSM_D_INNER), 0)
    hcol = lax.broadcasted_iota(jnp.int32, (SSM_HEADS, SSM_D_INNER), 1) // SSM_HEAD_DIM
    d_e = jnp.sum(jnp.where(hrow == hcol, d_ref[...], 0.0), axis=0, keepdims=True)
    y = yf_ref[...] + yb_ref[...] + d_e * xs_ref[...]
    y = y * _silu(z_ref[...])
    o_ref[...] = (y * lax.rsqrt(jnp.mean(y * y, axis=-1, keepdims=True) + NORM_EPS) * nw_ref[...]).astype(BF16)


def ssd_finish(yf, yb, xbc, p, z_col0, d_skip, norm_w):
    n_rows = yf.shape[0]
    tm = min(n_rows, 512)
    zb = z_col0 // SSM_D_INNER
    blk = lambda cb: pl.BlockSpec((tm, SSM_D_INNER), lambda i: (i, cb))
    return pl.pallas_call(
        _ssd_finish_kernel,
        grid=(n_rows // tm,),
        in_specs=[blk(0), blk(0), blk(0), blk(zb),
                  pl.BlockSpec((SSM_HEADS, 1), lambda i: (0, 0)),
                  pl.BlockSpec((1, SSM_D_INNER), lambda i: (0, 0))],
        out_specs=blk(0),
        out_shape=jax.ShapeDtypeStruct((n_rows, SSM_D_INNER), BF16),
        compiler_params=_cparams("parallel"),
        name="ssd_finish",
    )(yf, yb, xbc, p, d_skip.reshape(SSM_HEADS, 1), norm_w.reshape(1, SSM_D_INNER))


LRU_X_COL = 2560
LRU_G_COL = 3072
SSM_Z_COL = 3584
SSM_XBC_COL = 4096
SSM_DT_COL = 5120


def _mixers(p, qkv_q, qkv_kv, lw, lam_init, lru_h0, ssd_st0):
    ya = diff_attention(qkv_q, qkv_kv, lw["da_lambda"], lw["da_subln"], lam_init)
    yb = gqa_attention(qkv_q, qkv_kv)
    xc = dwconv(p, LRU_X_COL, LRU_WIDTH, lw["lru_conv_w"], lw["lru_conv_b"], act=False)
    yc, lru_hT = rglru(xc, p, LRU_G_COL, lw["lru_w_gates"], lw["lru_b_gates"], lw["lru_lambda"], lru_h0)
    xbc = dwconv(p, SSM_XBC_COL, 2 * SSM_D_INNER, lw["ssm_conv_w"], lw["ssm_conv_b"], act=True)
    dt = p[:, SSM_DT_COL:SSM_DT_COL + 2 * SSM_HEADS]
    yf, ybk, ssd_stT = ssd_scan(xbc, dt, dt.T, lw["ssm_dt_bias"], lw["ssm_a_log"], ssd_st0)
    yd = ssd_finish(yf, ybk, xbc, p, SSM_Z_COL, lw["ssm_d"], lw["ssm_norm"])
    return jnp.concatenate([ya, yb, yc, yd], axis=-1), lru_hT, ssd_stT


def kernel(x, c, ctx, c_ctx, w_mod, b_mod, mix_norm, ffn_norm, w_in, w_out, da_lambda, da_subln, gqa_q_norm,
           gqa_k_norm, lru_conv_w, lru_conv_b, lru_w_gates, lru_b_gates, lru_lambda, ssm_conv_w, ssm_conv_b,
           ssm_dt_bias, ssm_a_log, ssm_d, ssm_norm, ffn_w_up, ffn_conv_w, ffn_conv_b, ffn_w_down, final_norm):
    depth = w_mod.shape[0]
    d = x.shape[-1]
    xl = x[0]
    xc = ctx[0]
    n_lat, n_ctx = xl.shape[0], xc.shape[0]

    cond8 = jnp.zeros((8, d), F32).at[0].set(c[0]).at[1].set(c_ctx)
    mods = modulation_all(cond8, w_mod, b_mod)
    tabs_l = _rope_tables(n_lat, DA_HEAD_DIM) + _rope_tables(n_lat, GQA_HEAD_DIM)
    tabs_c = _rope_tables(n_ctx, DA_HEAD_DIM) + _rope_tables(n_ctx, GQA_HEAD_DIM)
    lru_zero = jnp.zeros((2, LRU_WIDTH), F32)
    ssd_zero = jnp.zeros((2, SSM_D_STATE, SSM_D_INNER), F32)

    for layer in range(depth):
        need_ctx = layer < depth - 1
        lam_init = 0.8 - 0.6 * math.exp(-0.3 * layer)
        ml = [mods[layer, 0:1, k * d:(k + 1) * d] for k in range(6)]
        mc = [mods[layer, 1:2, k * d:(k + 1) * d] for k in range(6)]
        lw = dict(da_lambda=da_lambda[layer], da_subln=da_subln[layer],
                  lru_conv_w=lru_conv_w[layer], lru_conv_b=lru_conv_b[layer], lru_w_gates=lru_w_gates[layer],
                  lru_b_gates=lru_b_gates[layer], lru_lambda=lru_lambda[layer], ssm_conv_w=ssm_conv_w[layer],
                  ssm_conv_b=ssm_conv_b[layer], ssm_dt_bias=ssm_dt_bias[layer], ssm_a_log=ssm_a_log[layer],
                  ssm_d=ssm_d[layer], ssm_norm=ssm_norm[layer])
        w_in_bf = jnp.pad(w_in[layer].astype(BF16), ((0, 0), (0, IN_COLS_PAD - IN_COLS)))
        w_out_bf = w_out[layer].astype(BF16)
        w_up_bf = ffn_w_up[layer].astype(BF16)
        w_down_bf = ffn_w_down[layer].astype(BF16)
        nw_m = mix_norm[layer].reshape(1, d)
        nw_f = ffn_norm[layer].reshape(1, d)

        p_c = in_proj(xc, nw_m, mc[0], mc[1], w_in_bf)
        p_l = in_proj(xl, nw_m, ml[0], ml[1], w_in_bf)
        qkv_c = attn_prep(p_c, tabs_c, gqa_q_norm[layer], gqa_k_norm[layer], rope=False)
        qkv_l = attn_prep(p_l, tabs_l, gqa_q_norm[layer], gqa_k_norm[layer], rope=True)
        qkv_all = jnp.concatenate([qkv_c, qkv_l], axis=0)

        y_c, lru_h, ssd_st = _mixers(p_c, qkv_c, qkv_c, lw, lam_init, lru_zero, ssd_zero)
        y_l, _, _ = _mixers(p_l, qkv_l, qkv_all, lw, lam_init, lru_h, ssd_st)

        xl = resid_proj(y_l, w_out_bf, xl, ml[2])
        act = ffn_up(xl, nw_f, ml[3], ml[4], w_up_bf, ffn_conv_w[layer], ffn_conv_b[layer])
        xl = resid_proj(act, w_down_bf, xl, ml[5])
        if need_ctx:
            xc = resid_proj(y_c, w_out_bf, xc, mc[2])
            act = ffn_up(xc, nw_f, mc[3], mc[4], w_up_bf, ffn_conv_w[layer], ffn_conv_b[layer])
            xc = resid_proj(act, w_down_bf, xc, mc[5])

    return final_rmsnorm(xl, final_norm)[None]
```

```python
import functools
import math

import jax
import jax.numpy as jnp
from jax import lax
from jax.experimental import pallas as pl
from jax.experimental.pallas import tpu as pltpu

F32 = jnp.float32
BF16 = jnp.bfloat16

D_MODEL = 2048
DEPTH = 4
GRID_W = 64
GROUP_WIDTH = 512
DA_HEAD_DIM = 64
DA_HEADS = 4
GQA_HEAD_DIM = 128
GQA_HEADS = 4
GQA_KV_HEADS = 2
LRU_WIDTH = 512
LRU_BLOCKS = 4
RGLRU_C = 8.0
SSM_D_INNER = 512
SSM_HEAD_DIM = 64
SSM_HEADS = 8
SSM_GROUPS = 2
SSM_D_STATE = 128
SSD_CHUNK = 128
D_FF = 5632
ROPE_THETA = 10000.0
NORM_EPS = 1e-6
LOG2E = math.log2(math.e)

LANES = 128
HALO = 16
IN_COLS = 5136
IN_COLS_PAD = 5376
ATTN_COLS = 2560
QKV_COLS = 3072
VMEM_LIMIT = 56 * 1024 * 1024


def _cparams(*sem):
    return pltpu.CompilerParams(dimension_semantics=sem, vmem_limit_bytes=VMEM_LIMIT)


def _silu(x):
    return x * jax.nn.sigmoid(x)


def _norm_mod(x, nw, shift, scale):
    y = x * lax.rsqrt(jnp.mean(x * x, axis=-1, keepdims=True) + NORM_EPS) * nw
    return y * (1.0 + scale) + shift


def _mod_kernel(c_ref, w_ref, b_ref, o_ref):
    s = _silu(c_ref[...]).astype(BF16)
    o_ref[...] = jnp.dot(s, w_ref[...].astype(BF16), preferred_element_type=F32) + b_ref[...]


def modulation_all(cond8, w_mod, b_mod):
    depth, d, n = w_mod.shape
    tn = 1024
    return pl.pallas_call(
        _mod_kernel,
        grid=(depth, n // tn),
        in_specs=[
            pl.BlockSpec((8, d), lambda l, j: (0, 0)),
            pl.BlockSpec((None, d, tn), lambda l, j: (l, 0, j)),
            pl.BlockSpec((None, 1, tn), lambda l, j: (l, 0, j)),
        ],
        out_specs=pl.BlockSpec((None, 8, tn), lambda l, j: (l, 0, j)),
        out_shape=jax.ShapeDtypeStruct((depth, 8, n), F32),
        compiler_params=_cparams("parallel", "parallel"),
        name="modulation",
    )(cond8, w_mod, b_mod.reshape(depth, 1, n))


def _inproj_kernel(x_ref, nw_ref, sh_ref, sc_ref, w_ref, o_ref, xn_ref):
    @pl.when(pl.program_id(1) == 0)
    def _():
        xn_ref[...] = _norm_mod(x_ref[...], nw_ref[...], sh_ref[...], sc_ref[...]).astype(BF16)

    o_ref[...] = jnp.dot(xn_ref[...], w_ref[...], preferred_element_type=F32)


def in_proj(x, nw, shift, scale, w_bf):
    m, d = x.shape
    n = w_bf.shape[1]
    tm = min(m, 1024)
    tn = 768
    vec = pl.BlockSpec((1, d), lambda i, j: (0, 0))
    return pl.pallas_call(
        _inproj_kernel,
        grid=(m // tm, n // tn),
        in_specs=[pl.BlockSpec((tm, d), lambda i, j: (i, 0)), vec, vec, vec,
                  pl.BlockSpec((d, tn), lambda i, j: (0, j))],
        out_specs=pl.BlockSpec((tm, tn), lambda i, j: (i, j)),
        out_shape=jax.ShapeDtypeStruct((m, n), F32),
        scratch_shapes=[pltpu.VMEM((tm, d), BF16)],
        compiler_params=_cparams("parallel", "arbitrary"),
        name="in_proj",
    )(x, nw, shift, scale, w_bf)


def _resid_kernel(y_ref, w_ref, x_ref, g_ref, o_ref):
    o_ref[...] = x_ref[...] + g_ref[...] * jnp.dot(y_ref[...], w_ref[...], preferred_element_type=F32)


def resid_proj(y_bf, w_bf, x, gate):
    m, k = y_bf.shape
    n = w_bf.shape[1]
    tm = min(m, 512)
    tn = 1024
    return pl.pallas_call(
        _resid_kernel,
        grid=(n // tn, m // tm),
        in_specs=[pl.BlockSpec((tm, k), lambda j, i: (i, 0)),
                  pl.BlockSpec((k, tn), lambda j, i: (0, j)),
                  pl.BlockSpec((tm, tn), lambda j, i: (i, j)),
                  pl.BlockSpec((1, tn), lambda j, i: (0, j))],
        out_specs=pl.BlockSpec((tm, tn), lambda j, i: (i, j)),
        out_shape=jax.ShapeDtypeStruct((m, n), F32),
        compiler_params=_cparams("parallel", "parallel"),
        name="resid_proj",
    )(y_bf, w_bf, x, gate)


def _ffn_up_kernel(xp_ref, x_ref, xx_ref, nw_ref, sh_ref, sc_ref, wg_ref, wu_ref, cw_ref, cb_ref,
                   o_ref, xn_ref, g_ref, *, tm, n_row_blocks):
    i = pl.program_id(0)

    @pl.when(pl.program_id(1) == 0)
    def _():
        def nm(v):
            return _norm_mod(v, nw_ref[...], sh_ref[...], sc_ref[...])

        prev = jnp.where(i == 0, 0.0, nm(xp_ref[...]))
        nxt = jnp.where(i == n_row_blocks - 1, 0.0, nm(xx_ref[...]))
        xn_ref[pl.ds(0, HALO), :] = prev.astype(BF16)
        xn_ref[pl.ds(HALO, tm), :] = nm(x_ref[...]).astype(BF16)
        xn_ref[pl.ds(HALO + tm, HALO), :] = nxt.astype(BF16)

    g_ref[...] = jnp.dot(xn_ref[...], wg_ref[...], preferred_element_type=F32)
    up = jnp.dot(xn_ref[pl.ds(HALO, tm), :], wu_ref[...], preferred_element_type=F32)
    cw = cw_ref[...]
    gc = (cb_ref[...] + cw[0:1] * g_ref[pl.ds(HALO - 1, tm), :] + cw[1:2] * g_ref[pl.ds(HALO, tm), :]
          + cw[2:3] * g_ref[pl.ds(HALO + 1, tm), :])
    o_ref[...] = (_silu(gc) * up).astype(BF16)


def ffn_up(x, nw, shift, scale, w_up_bf, conv_w, conv_b):
    m, d = x.shape
    dff = w_up_bf.shape[1] // 2
    tm = min(m, 1024)
    tn = 512
    nrb = m // tm
    hb = tm // HALO
    nhb = m // HALO
    vec = pl.BlockSpec((1, d), lambda i, j: (0, 0))
    return pl.pallas_call(
        functools.partial(_ffn_up_kernel, tm=tm, n_row_blocks=nrb),
        grid=(nrb, dff // tn),
        in_specs=[pl.BlockSpec((HALO, d), lambda i, j: (jnp.maximum(i * hb - 1, 0), 0)),
                  pl.BlockSpec((tm, d), lambda i, j: (i, 0)),
                  pl.BlockSpec((HALO, d), lambda i, j: (jnp.minimum((i + 1) * hb, nhb - 1), 0)),
                  vec, vec, vec,
                  pl.BlockSpec((d, tn), lambda i, j: (0, j)),
                  pl.BlockSpec((d, tn), lambda i, j: (0, j + dff // tn)),
                  pl.BlockSpec((3, tn), lambda i, j: (0, j)),
                  pl.BlockSpec((1, tn), lambda i, j: (0, j))],
        out_specs=pl.BlockSpec((tm, tn), lambda i, j: (i, j)),
        out_shape=jax.ShapeDtypeStruct((m, dff), BF16),
        scratch_shapes=[pltpu.VMEM((tm + 2 * HALO, d), BF16), pltpu.VMEM((tm + 2 * HALO, tn), F32)],
        compiler_params=_cparams("parallel", "arbitrary"),
        name="ffn_up",
    )(x, x, x, nw, shift, scale, w_up_bf, w_up_bf, conv_w, conv_b.reshape(1, dff))


def _rmsnorm_kernel(x_ref, w_ref, o_ref):
    x = x_ref[...]
    o_ref[...] = x * lax.rsqrt(jnp.mean(x * x, axis=-1, keepdims=True) + NORM_EPS) * w_ref[...]


def final_rmsnorm(x, w):
    m, d = x.shape
    tm = min(m, 512)
    return pl.pallas_call(
        _rmsnorm_kernel,
        grid=(m // tm,),
        in_specs=[pl.BlockSpec((tm, d), lambda i: (i, 0)), pl.BlockSpec((1, d), lambda i: (0, 0))],
        out_specs=pl.BlockSpec((tm, d), lambda i: (i, 0)),
        out_shape=jax.ShapeDtypeStruct((m, d), F32),
        compiler_params=_cparams("parallel"),
        name="final_norm",
    )(x, w.reshape(1, d))


def _rope_tables(n_tok, head_dim):
    q = head_dim // 4
    t = jnp.arange(n_tok, dtype=jnp.int32)
    row = (t // GRID_W).astype(F32)[:, None]
    col = (t % GRID_W).astype(F32)[:, None]
    lane = jnp.arange(LANES, dtype=jnp.int32)
    u = lane % head_dim
    region = u // q
    freqs = jnp.power(ROPE_THETA, -(u % q).astype(F32) / q)[None, :]
    ang = jnp.where((region < 2)[None, :], row, col) * freqs
    cos, sin = jnp.cos(ang), jnp.sin(ang)
    first = (region % 2 == 0)[None, :]
    return cos, jnp.where(first, -sin, 0.0), jnp.where(first, 0.0, sin)


def _prep_kernel(p_ref, cd_ref, ad_ref, bd_ref, cg_ref, ag_ref, bg_ref, qn_ref, kn_ref, o_ref, *, rope):
    lane = lax.broadcasted_iota(jnp.int32, (1, LANES), 1)
    lo = lane < DA_HEAD_DIM

    def blk(b):
        return p_ref[:, b * LANES:(b + 1) * LANES]

    def put(b, v):
        o_ref[:, b * LANES:(b + 1) * LANES] = v.astype(BF16)

    def rot(x, cos, sa, sb, quarter):
        if not rope:
            return x
        return (x * cos[...] + pltpu.roll(x, LANES - quarter, 1) * sa[...]
                + pltpu.roll(x, quarter, 1) * sb[...])

    def rms(x, w):
        return x * lax.rsqrt(jnp.mean(x * x, axis=-1, keepdims=True) + NORM_EPS) * w[...]

    qd = DA_HEAD_DIM // 4
    qg = GQA_HEAD_DIM // 4
    for h in range(DA_HEADS):
        q = rot(blk(h), cd_ref, ad_ref, bd_ref, qd) * (DA_HEAD_DIM ** -0.5 * LOG2E)
        put(h, jnp.where(lo, q, 0.0))
        put(4 + h, jnp.where(lo, 0.0, q))
        put(8 + h, rot(blk(4 + h), cd_ref, ad_ref, bd_ref, qd))
        put(12 + h, blk(8 + h))
    for h in range(GQA_HEADS):
        q = rot(rms(blk(12 + h), qn_ref), cg_ref, ag_ref, bg_ref, qg) * (GQA_HEAD_DIM ** -0.5 * LOG2E)
        put(16 + h, q)
    for h in range(GQA_KV_HEADS):
        put(20 + h, rot(rms(blk(16 + h), kn_ref), cg_ref, ag_ref, bg_ref, qg))
        put(22 + h, blk(18 + h))


def attn_prep(p, tables, q_norm, k_norm, rope):
    m = p.shape[0]
    tm = min(m, 512)
    tab = pl.BlockSpec((tm, LANES), lambda i: (i, 0))
    vec = pl.BlockSpec((1, LANES), lambda i: (0, 0))
    return pl.pallas_call(
        functools.partial(_prep_kernel, rope=rope),
        grid=(m // tm,),
        in_specs=[pl.BlockSpec((tm, ATTN_COLS), lambda i: (i, 0))] + [tab] * 6 + [vec, vec],
        out_specs=pl.BlockSpec((tm, QKV_COLS), lambda i: (i, 0)),
        out_shape=jax.ShapeDtypeStruct((m, QKV_COLS), BF16),
        compiler_params=_cparams("parallel"),
        name="attn_prep",
    )(p, *tables, q_norm.reshape(1, LANES), k_norm.reshape(1, LANES))


def _attn_kernel(*refs, tk, n_kv, diff, lam_init):
    if diff:
        qa_ref, qb_ref, k_ref, v_ref, lam_ref, sw_ref, o_ref, s_ref, m_ref, l_ref, acc_ref = refs
    else:
        qa_ref, qb_ref, k_ref, v_ref, o_ref, s_ref, m_ref, l_ref, acc_ref = refs
    tq = qa_ref.shape[0]
    n_tiles = tk // LANES
    q2 = jnp.concatenate([qa_ref[...], qb_ref[...]], axis=0)

    def lane_tiles(x):
        return [x[:, t * LANES:(t + 1) * LANES] for t in range(n_tiles)]

    m_ref[...] = jnp.full((2 * tq, LANES), -jnp.inf, F32)

    def score_chunk(c, _):
        ks = k_ref[pl.ds(pl.multiple_of(c * tk, tk), tk), :]
        s = lax.dot_general(q2, ks, (((1,), (1,)), ((), ())), preferred_element_type=F32)
        s_ref[c] = s
        m_ref[...] = functools.reduce(jnp.maximum, lane_tiles(s), m_ref[...])
        return 0

    lax.fori_loop(0, n_kv, score_chunk, 0)
    m_ref[...] = jnp.broadcast_to(jnp.max(m_ref[...], axis=-1, keepdims=True), (2 * tq, LANES))

    l_ref[...] = jnp.zeros((2 * tq, LANES), F32)
    acc_ref[...] = jnp.zeros((2 * tq, LANES), F32)

    def value_chunk(c, _):
        m = m_ref[...]
        ps = [jnp.exp2(st - m) for st in lane_tiles(s_ref[c])]
        l_ref[...] += functools.reduce(jnp.add, ps)
        p = jnp.concatenate([pt.astype(BF16) for pt in ps], axis=1)
        vs = v_ref[pl.ds(pl.multiple_of(c * tk, tk), tk), :]
        acc_ref[...] += jnp.dot(p, vs, preferred_element_type=F32)
        return 0

    lax.fori_loop(0, n_kv, value_chunk, 0)
    o = acc_ref[...] / jnp.sum(l_ref[...], axis=-1, keepdims=True)
    oa = o[0:tq]
    ob = o[tq:2 * tq]
    if diff:
        lv = lam_ref[...]
        lam = (jnp.exp(jnp.sum(lv[0:1] * lv[1:2], keepdims=True))
               - jnp.exp(jnp.sum(lv[2:3] * lv[3:4], keepdims=True)) + lam_init)
        o = oa - lam * ob
        o = o * lax.rsqrt(jnp.mean(o * o, axis=-1, keepdims=True) + NORM_EPS) * sw_ref[...]
        o_ref[...] = (o * (1.0 - lam_init)).astype(BF16)
    else:
        o_ref[:, 0:LANES] = oa.astype(BF16)
        o_ref[:, LANES:2 * LANES] = ob.astype(BF16)


ATTN_TQ = 256
ATTN_TK = (2816, 768, 512, 256, 128)


def _kv_tile(t):
    for tk in ATTN_TK:
        if t % tk == 0:
            return tk
    raise ValueError(f"unsupported key count {t}")


def _attn_scratch(tq, t, tk):
    vec = pltpu.VMEM((2 * tq, LANES), F32)
    return [pltpu.VMEM((t // tk, 2 * tq, tk), F32), vec, vec, vec]


def diff_attention(qkv_q, qkv_kv, da_lambda, subln_w, lam_init):
    sq, t = qkv_q.shape[0], qkv_kv.shape[0]
    tq = min(sq, ATTN_TQ)
    tk = _kv_tile(t)
    return pl.pallas_call(
        functools.partial(_attn_kernel, tk=tk, n_kv=t // tk, diff=True, lam_init=lam_init),
        grid=(DA_HEADS, sq // tq),
        in_specs=[pl.BlockSpec((tq, LANES), lambda h, i: (i, h)),
                  pl.BlockSpec((tq, LANES), lambda h, i: (i, 4 + h)),
                  pl.BlockSpec((t, LANES), lambda h, i: (0, 8 + h)),
                  pl.BlockSpec((t, LANES), lambda h, i: (0, 12 + h)),
                  pl.BlockSpec((4, DA_HEAD_DIM), lambda h, i: (0, 0)),
                  pl.BlockSpec((1, LANES), lambda h, i: (0, 0))],
        out_specs=pl.BlockSpec((tq, LANES), lambda h, i: (i, h)),
        out_shape=jax.ShapeDtypeStruct((sq, GROUP_WIDTH), BF16),
        scratch_shapes=_attn_scratch(tq, t, tk),
        compiler_params=_cparams("parallel", "parallel"),
        name="diff_attn",
    )(qkv_q, qkv_q, qkv_kv, qkv_kv, da_lambda, subln_w.reshape(1, LANES))


def gqa_attention(qkv_q, qkv_kv):
    sq, t = qkv_q.shape[0], qkv_kv.shape[0]
    tq = min(sq, ATTN_TQ)
    tk = _kv_tile(t)
    return pl.pallas_call(
        functools.partial(_attn_kernel, tk=tk, n_kv=t // tk, diff=False, lam_init=0.0),
        grid=(GQA_KV_HEADS, sq // tq),
        in_specs=[pl.BlockSpec((tq, LANES), lambda h, i: (i, 16 + 2 * h)),
                  pl.BlockSpec((tq, LANES), lambda h, i: (i, 17 + 2 * h)),
                  pl.BlockSpec((t, LANES), lambda h, i: (0, 20 + h)),
                  pl.BlockSpec((t, LANES), lambda h, i: (0, 22 + h))],
        out_specs=pl.BlockSpec((tq, 2 * LANES), lambda h, i: (i, h)),
        out_shape=jax.ShapeDtypeStruct((sq, GROUP_WIDTH), BF16),
        scratch_shapes=_attn_scratch(tq, t, tk),
        compiler_params=_cparams("parallel", "parallel"),
        name="gqa_attn",
    )(qkv_q, qkv_q, qkv_kv, qkv_kv)


CONV_CHUNK = 256


def _conv_chunk(x_ref, base, n_rows, chunk, cw, cb):
    cur = x_ref[pl.ds(base, chunk), :]
    prev = x_ref[pl.ds(pl.multiple_of(jnp.maximum(base - 8, 0), 8), 8), :]
    nxt = x_ref[pl.ds(pl.multiple_of(jnp.minimum(base + chunk, n_rows - 8), 8), 8), :]
    prev = jnp.where(base == 0, 0.0, prev)
    nxt = jnp.where(base + chunk >= n_rows, 0.0, nxt)
    cat = jnp.concatenate([prev, cur, nxt], axis=0)
    n = chunk + 16
    xm2 = pltpu.roll(cat, 2, 0)[8:8 + chunk]
    xm1 = pltpu.roll(cat, 1, 0)[8:8 + chunk]
    xp1 = pltpu.roll(cat, n - 1, 0)[8:8 + chunk]
    return cb + cw[0:1] * xm2 + cw[1:2] * xm1 + cw[2:3] * cur + cw[3:4] * xp1


def _dwconv_kernel(x_ref, w_ref, b_ref, o_ref, *, n_rows, chunk, act):
    cw = w_ref[...]
    cb = b_ref[...]

    def body(c, _):
        base = pl.multiple_of(c * chunk, chunk)
        y = _conv_chunk(x_ref, base, n_rows, chunk, cw, cb)
        o_ref[pl.ds(base, chunk), :] = _silu(y) if act else y
        return 0

    lax.fori_loop(0, n_rows // chunk, body, 0)


def dwconv(p, col0, width, conv_w, conv_b, act):
    n_rows = p.shape[0]
    chunk = min(n_rows, CONV_CHUNK)
    b0 = col0 // LANES
    return pl.pallas_call(
        functools.partial(_dwconv_kernel, n_rows=n_rows, chunk=chunk, act=act),
        grid=(width // LANES,),
        in_specs=[pl.BlockSpec((n_rows, LANES), lambda j: (0, b0 + j)),
                  pl.BlockSpec((4, LANES), lambda j: (0, j)),
                  pl.BlockSpec((1, LANES), lambda j: (0, j))],
        out_specs=pl.BlockSpec((n_rows, LANES), lambda j: (0, j)),
        out_shape=jax.ShapeDtypeStruct((n_rows, width), F32),
        compiler_params=_cparams("parallel"),
        name="dwconv",
    )(p, conv_w, conv_b.reshape(1, width))


LRU_CHUNK = 256


def _lru_scan_chunk(a, u, h_in, reverse):
    n = a.shape[0]
    row = lax.broadcasted_iota(jnp.int32, (n, 1), 0)
    k = 1
    while k < n:
        if reverse:
            keep = row < n - k
            a_s = jnp.where(keep, pltpu.roll(a, n - k, 0), 1.0)
            u_s = jnp.where(keep, pltpu.roll(u, n - k, 0), 0.0)
        else:
            keep = row >= k
            a_s = jnp.where(keep, pltpu.roll(a, k, 0), 1.0)
            u_s = jnp.where(keep, pltpu.roll(u, k, 0), 0.0)
        u = u + a * u_s
        a = a * a_s
        k *= 2
    return u + a * h_in


def _lru_kernel(xc_ref, g_ref, w_ref, b_ref, lam_ref, h0_ref, y_ref, hT_ref, hf_ref, *, n_rows, chunk):
    n_chunks = n_rows // chunk

    def direction(d, reverse, finish):
        w_r = w_ref[d, 0].astype(BF16)
        w_i = w_ref[d, 1].astype(BF16)
        b_r = b_ref[d, 0:1, :]
        b_i = b_ref[d, 1:2, :]
        lam = lam_ref[d:d + 1, :]
        log_sig = jnp.minimum(lam, 0.0) - jnp.log1p(jnp.exp(-jnp.abs(lam)))

        def body(c, h):
            cc = (n_chunks - 1 - c) if reverse else c
            base = pl.multiple_of(cc * chunk, chunk)
            x = xc_ref[pl.ds(base, chunk), :]
            xb = x.astype(BF16)
            r = jax.nn.sigmoid(jnp.dot(xb, w_r, preferred_element_type=F32) + b_r)
            i = jax.nn.sigmoid(jnp.dot(xb, w_i, preferred_element_type=F32) + b_i)
            log_a = RGLRU_C * r * log_sig
            a = jnp.exp(log_a)
            u = jnp.sqrt(1.0 - a * a) * (i * x)
            hs = _lru_scan_chunk(a, u, h, reverse)
            finish(base, hs)
            return hs[0:1] if reverse else hs[chunk - 1:chunk]

        return lax.fori_loop(0, n_chunks, body, h0_ref[d:d + 1, :])

    def keep_fwd(base, hs):
        hf_ref[pl.ds(base, chunk), :] = hs

    def emit(base, hs):
        g = g_ref[pl.ds(base, chunk), :]
        y_ref[pl.ds(base, chunk), :] = ((hf_ref[pl.ds(base, chunk), :] + hs) * jax.nn.gelu(g)).astype(BF16)

    hT_ref[0:1, :] = direction(0, False, keep_fwd)
    hT_ref[1:2, :] = direction(1, True, emit)


def rglru(xc, p, g_col0, w_gates, b_gates, lam, h0):
    n_rows = xc.shape[0]
    chunk = min(n_rows, LRU_CHUNK)
    gb = g_col0 // LANES
    return pl.pallas_call(
        functools.partial(_lru_kernel, n_rows=n_rows, chunk=chunk),
        grid=(LRU_BLOCKS,),
        in_specs=[pl.BlockSpec((n_rows, LANES), lambda j: (0, j)),
                  pl.BlockSpec((n_rows, LANES), lambda j: (0, gb + j)),
                  pl.BlockSpec((2, 2, None, LANES, LANES), lambda j: (0, 0, j, 0, 0)),
                  pl.BlockSpec((2, 2, LANES), lambda j: (0, 0, j)),
                  pl.BlockSpec((2, LANES), lambda j: (0, j)),
                  pl.BlockSpec((2, LANES), lambda j: (0, j))],
        out_specs=[pl.BlockSpec((n_rows, LANES), lambda j: (0, j)),
                   pl.BlockSpec((2, LANES), lambda j: (0, j))],
        out_shape=[jax.ShapeDtypeStruct((n_rows, LRU_WIDTH), BF16),
                   jax.ShapeDtypeStruct((2, LRU_WIDTH), F32)],
        scratch_shapes=[pltpu.VMEM((n_rows, LANES), F32)],
        compiler_params=_cparams("parallel"),
        name="rglru",
    )(xc, p, w_gates, b_gates, lam, h0)


def _ssd_direction(xs, bm, cm, dt_raw, dtT_raw, bias_row, bias_col, a_row, a_col, expand, st_ref, reverse):
    q = SSD_CHUNK
    hi = lax.Precision.HIGHEST
    li = lax.broadcasted_iota(jnp.int32, (q, q), 0)
    si = lax.broadcasted_iota(jnp.int32, (q, q), 1)
    causal = (si >= li) if reverse else (si <= li)
    tri = causal.astype(F32)
    dt = jax.nn.softplus(dt_raw + bias_row)
    dtT = jax.nn.softplus(dtT_raw + bias_col)
    adt = dt * a_row
    adtT = dtT * a_col
    cum = jnp.dot(tri, adt, preferred_element_type=F32, precision=hi)
    cumT = lax.dot_general(adtT, tri, (((1,), (1,)), ((), ())), preferred_element_type=F32,
                           precision=hi)
    edge = cum[0:1] if reverse else cum[q - 1:q]

    def ex(v):
        return jnp.dot(v, expand, preferred_element_type=F32, precision=hi)

    cum_e = ex(cum)
    edge_e = ex(edge)
    xdt = xs * ex(dt)
    xdd = (xdt * jnp.exp(edge_e - cum_e)).astype(BF16)
    xdt_b = xdt.astype(BF16)
    grow = jnp.exp(cum_e)
    ys = []
    for g in range(SSM_GROUPS):
        gsl = slice(g * 256, (g + 1) * 256)
        b_g = bm[:, g * SSM_D_STATE:(g + 1) * SSM_D_STATE]
        c_g = cm[:, g * SSM_D_STATE:(g + 1) * SSM_D_STATE].astype(BF16)
        b_gt = b_g.T.astype(BF16)
        gram = jnp.dot(c_g, b_gt, preferred_element_type=F32)
        st = st_ref[:, gsl]
        y_off = grow[:, gsl] * jnp.dot(c_g, st.astype(BF16), preferred_element_type=F32)
        st_ref[:, gsl] = jnp.exp(edge_e[:, gsl]) * st + jnp.dot(b_gt, xdd[:, gsl], preferred_element_type=F32)
        for hh in range(SSM_HEADS // SSM_GROUPS):
            h = g * (SSM_HEADS // SSM_GROUPS) + hh
            seg = cum[:, h:h + 1] - cumT[h:h + 1, :]
            decay = jnp.exp(jnp.where(causal, seg, -jnp.inf))
            m = (gram * decay).astype(BF16)
            hs = slice(h * SSM_HEAD_DIM, (h + 1) * SSM_HEAD_DIM)
            y_d = jnp.dot(m, xdt_b[:, hs], preferred_element_type=F32)
            ys.append(y_d + y_off[:, hh * SSM_HEAD_DIM:(hh + 1) * SSM_HEAD_DIM])
    return jnp.concatenate(ys, axis=1)


def _ssd_kernel(xf_ref, bf_ref, cf_ref, dtf_ref, dtTf_ref, xb_ref, bb_ref, cb_ref, dtb_ref, dtTb_ref,
                bias_ref, biasT_ref, alog_ref, alogT_ref, st0_ref, yf_ref, yb_ref, stT_ref, st_ref):
    c = pl.program_id(0)

    @pl.when(c == 0)
    def _():
        st_ref[...] = st0_ref[...]

    hrow = lax.broadcasted_iota(jnp.int32, (SSM_HEADS, SSM_D_INNER), 0)
    hcol = lax.broadcasted_iota(jnp.int32, (SSM_HEADS, SSM_D_INNER), 1) // SSM_HEAD_DIM
    expand = (hrow == hcol).astype(F32)
    a_row = -jnp.exp(alog_ref[...])
    a_col = -jnp.exp(alogT_ref[...])
    yf_ref[...] = _ssd_direction(xf_ref[...], bf_ref[...], cf_ref[...], dtf_ref[:, 0:SSM_HEADS],
                                 dtTf_ref[0:SSM_HEADS, :], bias_ref[0:1, :], biasT_ref[:, 0:1],
                                 a_row[0:1, :], a_col[:, 0:1], expand, st_ref.at[0], False)
    yb_ref[...] = _ssd_direction(xb_ref[...], bb_ref[...], cb_ref[...], dtb_ref[:, SSM_HEADS:2 * SSM_HEADS],
                                 dtTb_ref[SSM_HEADS:2 * SSM_HEADS, :], bias_ref[1:2, :], biasT_ref[:, 1:2],
                                 a_row[1:2, :], a_col[:, 1:2], expand, st_ref.at[1], True)

    @pl.when(c == pl.num_programs(0) - 1)
    def _():
        stT_ref[...] = st_ref[...]


def ssd_scan(xbc, dt, dtT, dt_bias, a_log, st0):
    n_rows = xbc.shape[0]
    q = SSD_CHUNK
    nc = n_rows // q
    xb = SSM_D_INNER // q
    fwd = lambda c: c
    bwd = lambda c: nc - 1 - c

    def specs(ix):
        return [pl.BlockSpec((q, SSM_D_INNER), lambda c: (ix(c), 0)),
                pl.BlockSpec((q, 2 * SSM_D_STATE), lambda c: (ix(c), 2)),
                pl.BlockSpec((q, 2 * SSM_D_STATE), lambda c: (ix(c), 3)),
                pl.BlockSpec((q, 2 * SSM_HEADS), lambda c: (ix(c), 0)),
                pl.BlockSpec((2 * SSM_HEADS, q), lambda c: (0, ix(c)))]

    small = lambda shape: pl.BlockSpec(shape, lambda c: (0,) * len(shape))
    st_shape = (2, SSM_D_STATE, SSM_D_INNER)
    return pl.pallas_call(
        _ssd_kernel,
        grid=(nc,),
        in_specs=specs(fwd) + specs(bwd) + [small((2, SSM_HEADS)), small((SSM_HEADS, 2)),
                                            small((2, SSM_HEADS)), small((SSM_HEADS, 2)), small(st_shape)],
        out_specs=[pl.BlockSpec((q, SSM_D_INNER), lambda c: (c, 0)),
                   pl.BlockSpec((q, SSM_D_INNER), lambda c: (nc - 1 - c, 0)),
                   small(st_shape)],
        out_shape=[jax.ShapeDtypeStruct((n_rows, SSM_D_INNER), F32),
                   jax.ShapeDtypeStruct((n_rows, SSM_D_INNER), F32),
                   jax.ShapeDtypeStruct(st_shape, F32)],
        scratch_shapes=[pltpu.VMEM(st_shape, F32)],
        compiler_params=_cparams("arbitrary"),
        name="ssd_scan",
    )(xbc, xbc, xbc, dt, dtT, xbc, xbc, xbc, dt, dtT, dt_bias, dt_bias.T, a_log, a_log.T, st0)


def _ssd_finish_kernel(yf_ref, yb_ref, xs_ref, z_ref, d_ref, nw_ref, o_ref):
    hrow = lax.broadcasted_iota(jnp.int32, (SSM_HEADS, SSM_D_INNER), 0)
    hcol = lax.broadcasted_iota(jnp.int32, (SSM_HEADS, SSM_D_INNER), 1) // SSM_HEAD_DIM
    d_e = jnp.sum(jnp.where(hrow == hcol, d_ref[...], 0.0), axis=0, keepdims=True)
    y = yf_ref[...] + yb_ref[...] + d_e * xs_ref[...]
    y = y * _silu(z_ref[...])
    o_ref[...] = (y * lax.rsqrt(jnp.mean(y * y, axis=-1, keepdims=True) + NORM_EPS) * nw_ref[...]).astype(BF16)


def ssd_finish(yf, yb, xbc, p, z_col0, d_skip, norm_w):
    n_rows = yf.shape[0]
    tm = min(n_rows, 512)
    zb = z_col0 // SSM_D_INNER
    blk = lambda cb: pl.BlockSpec((tm, SSM_D_INNER), lambda i: (i, cb))
    return pl.pallas_call(
        _ssd_finish_kernel,
        grid=(n_rows // tm,),
        in_specs=[blk(0), blk(0), blk(0), blk(zb),
                  pl.BlockSpec((SSM_HEADS, 1), lambda i: (0, 0)),
                  pl.BlockSpec((1, SSM_D_INNER), lambda i: (0, 0))],
        out_specs=blk(0),
        out_shape=jax.ShapeDtypeStruct((n_rows, SSM_D_INNER), BF16),
        compiler_params=_cparams("parallel"),
        name="ssd_finish",
    )(yf, yb, xbc, p, d_skip.reshape(SSM_HEADS, 1), norm_w.reshape(1, SSM_D_INNER))


LRU_X_COL = 2560
LRU_G_COL = 3072
SSM_Z_COL = 3584
SSM_XBC_COL = 4096
SSM_DT_COL = 5120


def _mixers(p, qkv_q, qkv_kv, lw, lam_init, lru_h0, ssd_st0):
    ya = diff_attention(qkv_q, qkv_kv, lw["da_lambda"], lw["da_subln"], lam_init)
    yb = gqa_attention(qkv_q, qkv_kv)
    xc = dwconv(p, LRU_X_COL, LRU_WIDTH, lw["lru_conv_w"], lw["lru_conv_b"], act=False)
    yc, lru_hT = rglru(xc, p, LRU_G_COL, lw["lru_w_gates"], lw["lru_b_gates"], lw["lru_lambda"], lru_h0)
    xbc = dwconv(p, SSM_XBC_COL, 2 * SSM_D_INNER, lw["ssm_conv_w"], lw["ssm_conv_b"], act=True)
    dt = p[:, SSM_DT_COL:SSM_DT_COL + 2 * SSM_HEADS]
    yf, ybk, ssd_stT = ssd_scan(xbc, dt, dt.T, lw["ssm_dt_bias"], lw["ssm_a_log"], ssd_st0)
    yd = ssd_finish(yf, ybk, xbc, p, SSM_Z_COL, lw["ssm_d"], lw["ssm_norm"])
    return jnp.concatenate([ya, yb, yc, yd], axis=-1), lru_hT, ssd_stT


def kernel(x, c, ctx, c_ctx, w_mod, b_mod, mix_norm, ffn_norm, w_in, w_out, da_lambda, da_subln, gqa_q_norm,
           gqa_k_norm, lru_conv_w, lru_conv_b, lru_w_gates, lru_b_gates, lru_lambda, ssm_conv_w, ssm_conv_b,
           ssm_dt_bias, ssm_a_log, ssm_d, ssm_norm, ffn_w_up, ffn_conv_w, ffn_conv_b, ffn_w_down, final_norm):
    depth = w_mod.shape[0]
    d = x.shape[-1]
    xl = x[0]
    xc = ctx[0]
    n_lat, n_ctx = xl.shape[0], xc.shape[0]

    cond8 = jnp.zeros((8, d), F32).at[0].set(c[0]).at[1].set(c_ctx)
    mods = modulation_all(cond8, w_mod, b_mod)
    tabs_l = _rope_tables(n_lat, DA_HEAD_DIM) + _rope_tables(n_lat, GQA_HEAD_DIM)
    tabs_c = _rope_tables(n_ctx, DA_HEAD_DIM) + _rope_tables(n_ctx, GQA_HEAD_DIM)
    lru_zero = jnp.zeros((2, LRU_WIDTH), F32)
    ssd_zero = jnp.zeros((2, SSM_D_STATE, SSM_D_INNER), F32)

    for layer in range(depth):
        need_ctx = layer < depth - 1
        lam_init = 0.8 - 0.6 * math.exp(-0.3 * layer)
        ml = [mods[layer, 0:1, k * d:(k + 1) * d] for k in range(6)]
        mc = [mods[layer, 1:2, k * d:(k + 1) * d] for k in range(6)]
        lw = dict(da_lambda=da_lambda[layer], da_subln=da_subln[layer],
                  lru_conv_w=lru_conv_w[layer], lru_conv_b=lru_conv_b[layer], lru_w_gates=lru_w_gates[layer],
                  lru_b_gates=lru_b_gates[layer], lru_lambda=lru_lambda[layer], ssm_conv_w=ssm_conv_w[layer],
                  ssm_conv_b=ssm_conv_b[layer], ssm_dt_bias=ssm_dt_bias[layer], ssm_a_log=ssm_a_log[layer],
                  ssm_d=ssm_d[layer], ssm_norm=ssm_norm[layer])
        w_in_bf = jnp.pad(w_in[layer].astype(BF16), ((0, 0), (0, IN_COLS_PAD - IN_COLS)))
        w_out_bf = w_out[layer].astype(BF16)
        w_up_bf = ffn_w_up[layer].astype(BF16)
        w_down_bf = ffn_w_down[layer].astype(BF16)
        nw_m = mix_norm[layer].reshape(1, d)
        nw_f = ffn_norm[layer].reshape(1, d)

        p_c = in_proj(xc, nw_m, mc[0], mc[1], w_in_bf)
        p_l = in_proj(xl, nw_m, ml[0], ml[1], w_in_bf)
        qkv_c = attn_prep(p_c, tabs_c, gqa_q_norm[layer], gqa_k_norm[layer], rope=False)
        qkv_l = attn_prep(p_l, tabs_l, gqa_q_norm[layer], gqa_k_norm[layer], rope=True)
        qkv_all = jnp.concatenate([qkv_c, qkv_l], axis=0)

        y_c, lru_h, ssd_st = _mixers(p_c, qkv_c, qkv_c, lw, lam_init, lru_zero, ssd_zero)
        y_l, _, _ = _mixers(p_l, qkv_l, qkv_all, lw, lam_init, lru_h, ssd_st)

        xl = resid_proj(y_l, w_out_bf, xl, ml[2])
        act = ffn_up(xl, nw_f, ml[3], ml[4], w_up_bf, ffn_conv_w[layer], ffn_conv_b[layer])
        xl = resid_proj(act, w_down_bf, xl, ml[5])
        if need_ctx:
            xc = resid_proj(y_c, w_out_bf, xc, mc[2])
            act = ffn_up(xc, nw_f, mc[3], mc[4], w_up_bf, ffn_conv_w[layer], ffn_conv_b[layer])
            xc = resid_proj(act, w_down_bf, xc, mc[5])

    return final_rmsnorm(xl, final_norm)[None]
```

```python
import functools
import math

import jax
import jax.numpy as jnp
from jax import lax
from jax.experimental import pallas as pl
from jax.experimental.pallas import tpu as pltpu

F32 = jnp.float32
BF16 = jnp.bfloat16

D_MODEL = 2048
DEPTH = 4
GRID_W = 64
GROUP_WIDTH = 512
DA_HEAD_DIM = 64
DA_HEADS = 4
GQA_HEAD_DIM = 128
GQA_HEADS = 4
GQA_KV_HEADS = 2
LRU_WIDTH = 512
LRU_BLOCKS = 4
RGLRU_C = 8.0
SSM_D_INNER = 512
SSM_HEAD_DIM = 64
SSM_HEADS = 8
SSM_GROUPS = 2
SSM_D_STATE = 128
SSD_CHUNK = 128
D_FF = 5632
ROPE_THETA = 10000.0
NORM_EPS = 1e-6
LOG2E = math.log2(math.e)

LANES = 128
SUBLANES = 8
HALO = 16
IN_COLS = 5136
ATTN_COLS = 2560
QKV_COLS = 3072
VMEM_LIMIT = 56 * 1024 * 1024


def _cparams(*sem):
    return pltpu.CompilerParams(dimension_semantics=sem, vmem_limit_bytes=VMEM_LIMIT)


def _silu(x):
    return x * jax.nn.sigmoid(x)


def _norm_mod(x, nw, shift, scale):
    y = x * lax.rsqrt(jnp.mean(x * x, axis=-1, keepdims=True) + NORM_EPS) * nw
    return y * (1.0 + scale) + shift


def _mod_kernel(c_ref, w_ref, b_ref, o_ref):
    s = _silu(c_ref[...]).astype(BF16)
    o_ref[...] = jnp.dot(s, w_ref[...].astype(BF16), preferred_element_type=F32) + b_ref[...]


def modulation_all(cond8, w_mod, b_mod):
    depth, d, n = w_mod.shape
    tn = 1024
    return pl.pallas_call(
        _mod_kernel,
        grid=(depth, n // tn),
        in_specs=[
            pl.BlockSpec((8, d), lambda l, j: (0, 0)),
            pl.BlockSpec((None, d, tn), lambda l, j: (l, 0, j)),
            pl.BlockSpec((None, 1, tn), lambda l, j: (l, 0, j)),
        ],
        out_specs=pl.BlockSpec((None, 8, tn), lambda l, j: (l, 0, j)),
        out_shape=jax.ShapeDtypeStruct((depth, 8, n), F32),
        compiler_params=_cparams("parallel", "parallel"),
        name="modulation",
    )(cond8, w_mod, b_mod.reshape(depth, 1, n))


IN_MAIN_COLS = 5120
IN_TN = 512


def _inproj_kernel(x_ref, nw_ref, sh_ref, sc_ref, w_ref, wdt_ref, o_ref, dt_ref, xn_ref):
    @pl.when(pl.program_id(1) == 0)
    def _():
        xn_ref[...] = _norm_mod(x_ref[...], nw_ref[...], sh_ref[...], sc_ref[...]).astype(BF16)
        dt_ref[...] = jnp.dot(xn_ref[...], wdt_ref[...], preferred_element_type=F32)

    o_ref[...] = jnp.dot(xn_ref[...], w_ref[...].astype(BF16), preferred_element_type=F32)


def in_proj(x, nw, shift, scale, w_in, w_dt_bf):
    m, d = x.shape
    tm = min(m, 1024)
    vec = pl.BlockSpec((1, d), lambda i, j: (0, 0))
    return pl.pallas_call(
        _inproj_kernel,
        grid=(m // tm, IN_MAIN_COLS // IN_TN),
        in_specs=[pl.BlockSpec((tm, d), lambda i, j: (i, 0)), vec, vec, vec,
                  pl.BlockSpec((d, IN_TN), lambda i, j: (0, j)),
                  pl.BlockSpec((d, LANES), lambda i, j: (0, 0))],
        out_specs=[pl.BlockSpec((tm, IN_TN), lambda i, j: (i, j)),
                   pl.BlockSpec((tm, LANES), lambda i, j: (i, 0))],
        out_shape=[jax.ShapeDtypeStruct((m, IN_MAIN_COLS), F32), jax.ShapeDtypeStruct((m, LANES), F32)],
        scratch_shapes=[pltpu.VMEM((tm, d), BF16)],
        compiler_params=_cparams("parallel", "arbitrary"),
        name="in_proj",
    )(x, nw, shift, scale, w_in, w_dt_bf)


def _resid_kernel(*refs, n_y):
    y_refs = refs[:n_y]
    w_ref, x_ref, g_ref, o_ref, wb_ref = refs[n_y:]

    @pl.when(pl.program_id(1) == 0)
    def _():
        wb_ref[...] = w_ref[...].astype(BF16)

    kk = wb_ref.shape[0] // n_y
    acc = functools.reduce(jnp.add, [
        jnp.dot(y_refs[a][...], wb_ref[a * kk:(a + 1) * kk, :], preferred_element_type=F32) for a in range(n_y)])
    o_ref[...] = x_ref[...] + g_ref[...] * acc


def resid_proj(ys, w, x, gate):
    m = x.shape[0]
    k, n = w.shape
    kk = k // len(ys)
    tm = min(m, 512)
    tn = 1024 if k <= 2048 else 512
    return pl.pallas_call(
        functools.partial(_resid_kernel, n_y=len(ys)),
        grid=(n // tn, m // tm),
        in_specs=[pl.BlockSpec((tm, kk), lambda j, i: (i, 0)) for _ in ys]
                 + [pl.BlockSpec((k, tn), lambda j, i: (0, j)),
                    pl.BlockSpec((tm, tn), lambda j, i: (i, j)),
                    pl.BlockSpec((1, tn), lambda j, i: (0, j))],
        out_specs=pl.BlockSpec((tm, tn), lambda j, i: (i, j)),
        out_shape=jax.ShapeDtypeStruct((m, n), F32),
        scratch_shapes=[pltpu.VMEM((k, tn), BF16)],
        compiler_params=_cparams("parallel", "arbitrary"),
        name="resid_proj",
    )(*ys, w, x, gate)


def _ffn_up_kernel(xp_ref, x_ref, xx_ref, nw_ref, sh_ref, sc_ref, wg_ref, wu_ref, cw_ref, cb_ref,
                   o_ref, xn_ref, g_ref, wb_ref, *, tm, n_row_blocks):
    i = pl.program_id(0)
    wb_ref[0] = wg_ref[...].astype(BF16)
    wb_ref[1] = wu_ref[...].astype(BF16)

    @pl.when(pl.program_id(1) == 0)
    def _():
        def nm(v):
            return _norm_mod(v, nw_ref[...], sh_ref[...], sc_ref[...])

        prev = jnp.where(i == 0, 0.0, nm(xp_ref[...]))
        nxt = jnp.where(i == n_row_blocks - 1, 0.0, nm(xx_ref[...]))
        xn_ref[pl.ds(0, HALO), :] = prev.astype(BF16)
        xn_ref[pl.ds(HALO, tm), :] = nm(x_ref[...]).astype(BF16)
        xn_ref[pl.ds(HALO + tm, HALO), :] = nxt.astype(BF16)

    g_ref[...] = jnp.dot(xn_ref[...], wb_ref[0], preferred_element_type=F32)
    up = jnp.dot(xn_ref[pl.ds(HALO, tm), :], wb_ref[1], preferred_element_type=F32)
    cw = cw_ref[...]
    gc = (cb_ref[...] + cw[0:1] * g_ref[pl.ds(HALO - 1, tm), :] + cw[1:2] * g_ref[pl.ds(HALO, tm), :]
          + cw[2:3] * g_ref[pl.ds(HALO + 1, tm), :])
    o_ref[...] = (_silu(gc) * up).astype(BF16)


def ffn_up(x, nw, shift, scale, w_up, conv_w, conv_b):
    m, d = x.shape
    dff = w_up.shape[1] // 2
    tm = min(m, 1024)
    tn = 512
    nrb = m // tm
    hb = tm // HALO
    nhb = m // HALO
    vec = pl.BlockSpec((1, d), lambda i, j: (0, 0))
    return pl.pallas_call(
        functools.partial(_ffn_up_kernel, tm=tm, n_row_blocks=nrb),
        grid=(nrb, dff // tn),
        in_specs=[pl.BlockSpec((HALO, d), lambda i, j: (jnp.maximum(i * hb - 1, 0), 0)),
                  pl.BlockSpec((tm, d), lambda i, j: (i, 0)),
                  pl.BlockSpec((HALO, d), lambda i, j: (jnp.minimum((i + 1) * hb, nhb - 1), 0)),
                  vec, vec, vec,
                  pl.BlockSpec((d, tn), lambda i, j: (0, j)),
                  pl.BlockSpec((d, tn), lambda i, j: (0, j + dff // tn)),
                  pl.BlockSpec((3, tn), lambda i, j: (0, j)),
                  pl.BlockSpec((1, tn), lambda i, j: (0, j))],
        out_specs=pl.BlockSpec((tm, tn), lambda i, j: (i, j)),
        out_shape=jax.ShapeDtypeStruct((m, dff), BF16),
        scratch_shapes=[pltpu.VMEM((tm + 2 * HALO, d), BF16), pltpu.VMEM((tm + 2 * HALO, tn), F32),
                        pltpu.VMEM((2, d, tn), BF16)],
        compiler_params=_cparams("parallel", "arbitrary"),
        name="ffn_up",
    )(x, x, x, nw, shift, scale, w_up, w_up, conv_w, conv_b.reshape(1, dff))


def _rmsnorm_kernel(x_ref, w_ref, o_ref):
    x = x_ref[...]
    o_ref[...] = x * lax.rsqrt(jnp.mean(x * x, axis=-1, keepdims=True) + NORM_EPS) * w_ref[...]


def final_rmsnorm(x, w):
    m, d = x.shape
    tm = min(m, 512)
    return pl.pallas_call(
        _rmsnorm_kernel,
        grid=(m // tm,),
        in_specs=[pl.BlockSpec((tm, d), lambda i: (i, 0)), pl.BlockSpec((1, d), lambda i: (0, 0))],
        out_specs=pl.BlockSpec((tm, d), lambda i: (i, 0)),
        out_shape=jax.ShapeDtypeStruct((m, d), F32),
        compiler_params=_cparams("parallel"),
        name="final_norm",
    )(x, w.reshape(1, d))


def _rope_tables(n_tok, head_dim):
    q = head_dim // 4
    t = jnp.arange(n_tok, dtype=jnp.int32)
    row = (t // GRID_W).astype(F32)[:, None]
    col = (t % GRID_W).astype(F32)[:, None]
    lane = jnp.arange(LANES, dtype=jnp.int32)
    u = lane % head_dim
    region = u // q
    freqs = jnp.power(ROPE_THETA, -(u % q).astype(F32) / q)[None, :]
    ang = jnp.where((region < 2)[None, :], row, col) * freqs
    cos, sin = jnp.cos(ang), jnp.sin(ang)
    first = (region % 2 == 0)[None, :]
    return cos, jnp.where(first, -sin, 0.0), jnp.where(first, 0.0, sin)


def _prep_kernel(p_ref, cd_ref, ad_ref, bd_ref, cg_ref, ag_ref, bg_ref, qn_ref, kn_ref, o_ref, *, rope):
    lane = lax.broadcasted_iota(jnp.int32, (1, LANES), 1)
    lo = lane < DA_HEAD_DIM

    def blk(b):
        return p_ref[:, b * LANES:(b + 1) * LANES]

    def put(b, v):
        o_ref[:, b * LANES:(b + 1) * LANES] = v.astype(BF16)

    def rot(x, cos, sa, sb, quarter):
        if not rope:
            return x
        return (x * cos[...] + pltpu.roll(x, LANES - quarter, 1) * sa[...]
                + pltpu.roll(x, quarter, 1) * sb[...])

    def rms(x, w):
        return x * lax.rsqrt(jnp.mean(x * x, axis=-1, keepdims=True) + NORM_EPS) * w[...]

    qd = DA_HEAD_DIM // 4
    qg = GQA_HEAD_DIM // 4
    for h in range(DA_HEADS):
        q = rot(blk(h), cd_ref, ad_ref, bd_ref, qd) * (DA_HEAD_DIM ** -0.5 * LOG2E)
        put(h, jnp.where(lo, q, 0.0))
        put(4 + h, jnp.where(lo, 0.0, q))
        put(8 + h, rot(blk(4 + h), cd_ref, ad_ref, bd_ref, qd))
        put(12 + h, blk(8 + h))
    for h in range(GQA_HEADS):
        q = rot(rms(blk(12 + h), qn_ref), cg_ref, ag_ref, bg_ref, qg) * (GQA_HEAD_DIM ** -0.5 * LOG2E)
        put(16 + h, q)
    for h in range(GQA_KV_HEADS):
        put(20 + h, rot(rms(blk(16 + h), kn_ref), cg_ref, ag_ref, bg_ref, qg))
        put(22 + h, blk(18 + h))


def attn_prep(p, tables, q_norm, k_norm, rope):
    m = p.shape[0]
    tm = min(m, 512)
    tab = pl.BlockSpec((tm, LANES), lambda i: (i, 0))
    vec = pl.BlockSpec((1, LANES), lambda i: (0, 0))
    return pl.pallas_call(
        functools.partial(_prep_kernel, rope=rope),
        grid=(m // tm,),
        in_specs=[pl.BlockSpec((tm, ATTN_COLS), lambda i: (i, 0))] + [tab] * 6 + [vec, vec],
        out_specs=pl.BlockSpec((tm, QKV_COLS), lambda i: (i, 0)),
        out_shape=jax.ShapeDtypeStruct((m, QKV_COLS), BF16),
        compiler_params=_cparams("parallel"),
        name="attn_prep",
    )(p, *tables, q_norm.reshape(1, LANES), k_norm.reshape(1, LANES))


def _attn_kernel(*refs, tk, n_kv, diff, lam_init):
    refs = list(refs)
    qa_ref, qb_ref, kc_ref, vc_ref = refs[:4]
    del refs[:4]
    if n_kv:
        kl_ref, vl_ref = refs[:2]
        del refs[:2]
    if diff:
        lam_ref, sw_ref = refs[:2]
        del refs[:2]
    o_ref = refs.pop(0)
    if n_kv:
        s_ref = refs.pop(0)
    sc_ref, m_ref, l_ref, acc_ref = refs
    tq = qa_ref.shape[0]
    nt = (((1,), (1,)), ((), ()))
    q2 = jnp.concatenate([qa_ref[...], qb_ref[...]], axis=0)

    def lane_tiles(x):
        return [x[:, t * LANES:(t + 1) * LANES] for t in range(x.shape[1] // LANES)]

    sc = lax.dot_general(q2, kc_ref[...], nt, preferred_element_type=F32)
    sc_ref[...] = sc
    m_ref[...] = functools.reduce(jnp.maximum, lane_tiles(sc))

    def score_chunk(c, _):
        ks = kl_ref[pl.ds(pl.multiple_of(c * tk, tk), tk), :]
        s = lax.dot_general(q2, ks, nt, preferred_element_type=F32)
        s_ref[c] = s
        m_ref[...] = functools.reduce(jnp.maximum, lane_tiles(s), m_ref[...])
        return 0

    if n_kv:
        lax.fori_loop(0, n_kv, score_chunk, 0)
    m_ref[...] = jnp.broadcast_to(jnp.max(m_ref[...], axis=-1, keepdims=True), (2 * tq, LANES))

    def weigh(s, vs):
        m = m_ref[...]
        ps = [jnp.exp2(st - m) for st in lane_tiles(s)]
        p = jnp.concatenate([pt.astype(BF16) for pt in ps], axis=1)
        return functools.reduce(jnp.add, ps), jnp.dot(p, vs, preferred_element_type=F32)

    l_ref[...], acc_ref[...] = weigh(sc_ref[...], vc_ref[...])

    def value_chunk(c, _):
        lsum, pv = weigh(s_ref[c], vl_ref[pl.ds(pl.multiple_of(c * tk, tk), tk), :])
        l_ref[...] += lsum
        acc_ref[...] += pv
        return 0

    if n_kv:
        lax.fori_loop(0, n_kv, value_chunk, 0)
    o = acc_ref[...] / jnp.sum(l_ref[...], axis=-1, keepdims=True)
    oa = o[0:tq]
    ob = o[tq:2 * tq]
    if diff:
        lv = lam_ref[...]
        lam = (jnp.exp(jnp.sum(lv[0:1] * lv[1:2], keepdims=True))
               - jnp.exp(jnp.sum(lv[2:3] * lv[3:4], keepdims=True)) + lam_init)
        o = oa - lam * ob
        o = o * lax.rsqrt(jnp.mean(o * o, axis=-1, keepdims=True) + NORM_EPS) * sw_ref[...]
        o_ref[...] = (o * (1.0 - lam_init)).astype(BF16)
    else:
        o_ref[:, 0:LANES] = oa.astype(BF16)
        o_ref[:, LANES:2 * LANES] = ob.astype(BF16)


ATTN_TQ = 256
ATTN_TK = (4096, 2048, 1024, 512, 256, 128)


def _attention(qkv_q, qkv_c, qkv_l, extra, extra_specs, cols, n_heads, out_width, name, **static):
    qa_col, qb_col, k_col, v_col = cols
    sq, n_ctx = qkv_q.shape[0], qkv_c.shape[0]
    tq = min(sq, ATTN_TQ)
    n_lat = 0 if qkv_l is None else qkv_l.shape[0]
    tk = next((t for t in ATTN_TK if n_lat and n_lat % t == 0), 0)
    n_kv = n_lat // tk if n_lat else 0

    def kv_spec(rows, col):
        return pl.BlockSpec((rows, LANES), lambda h, i: (0, col(h)))

    in_specs = [pl.BlockSpec((tq, LANES), lambda h, i: (i, qa_col(h))),
                pl.BlockSpec((tq, LANES), lambda h, i: (i, qb_col(h))),
                kv_spec(n_ctx, k_col), kv_spec(n_ctx, v_col)]
    args = [qkv_q, qkv_q, qkv_c, qkv_c]
    vec = pltpu.VMEM((2 * tq, LANES), F32)
    scratch = [pltpu.VMEM((2 * tq, n_ctx), F32), vec, vec, vec]
    if n_kv:
        in_specs += [kv_spec(n_lat, k_col), kv_spec(n_lat, v_col)]
        args += [qkv_l, qkv_l]
        scratch = [pltpu.VMEM((n_kv, 2 * tq, tk), F32)] + scratch
    return pl.pallas_call(
        functools.partial(_attn_kernel, tk=tk, n_kv=n_kv, **static),
        grid=(n_heads, sq // tq),
        in_specs=in_specs + extra_specs,
        out_specs=pl.BlockSpec((tq, out_width), lambda h, i: (i, h)),
        out_shape=jax.ShapeDtypeStruct((sq, GROUP_WIDTH), BF16),
        scratch_shapes=scratch,
        compiler_params=_cparams("parallel", "parallel"),
        name=name,
    )(*args, *extra)


def diff_attention(qkv_q, qkv_c, qkv_l, da_lambda, subln_w, lam_init):
    cols = (lambda h: h, lambda h: 4 + h, lambda h: 8 + h, lambda h: 12 + h)
    extra_specs = [pl.BlockSpec((4, DA_HEAD_DIM), lambda h, i: (0, 0)), pl.BlockSpec((1, LANES), lambda h, i: (0, 0))]
    return _attention(qkv_q, qkv_c, qkv_l, [da_lambda, subln_w.reshape(1, LANES)], extra_specs, cols,
                      DA_HEADS, LANES, "diff_attn", diff=True, lam_init=lam_init)


def gqa_attention(qkv_q, qkv_c, qkv_l):
    cols = (lambda h: 16 + 2 * h, lambda h: 17 + 2 * h, lambda h: 20 + h, lambda h: 22 + h)
    return _attention(qkv_q, qkv_c, qkv_l, [], [], cols, GQA_KV_HEADS, 2 * LANES, "gqa_attn",
                      diff=False, lam_init=0.0)


CONV_CHUNK = 256


def _conv_chunk(x_ref, base, n_rows, chunk, cw, cb):
    cur = x_ref[pl.ds(base, chunk), :]
    prev = x_ref[pl.ds(pl.multiple_of(jnp.maximum(base - 8, 0), 8), 8), :]
    nxt = x_ref[pl.ds(pl.multiple_of(jnp.minimum(base + chunk, n_rows - 8), 8), 8), :]
    prev = jnp.where(base == 0, 0.0, prev)
    nxt = jnp.where(base + chunk >= n_rows, 0.0, nxt)
    cat = jnp.concatenate([prev, cur, nxt], axis=0)
    n = chunk + 16
    xm2 = pltpu.roll(cat, 2, 0)[8:8 + chunk]
    xm1 = pltpu.roll(cat, 1, 0)[8:8 + chunk]
    xp1 = pltpu.roll(cat, n - 1, 0)[8:8 + chunk]
    return cb + cw[0:1] * xm2 + cw[1:2] * xm1 + cw[2:3] * cur + cw[3:4] * xp1


def _dwconv_kernel(x_ref, w_ref, b_ref, o_ref, *, n_rows, chunk, act):
    cw = w_ref[...]
    cb = b_ref[...]

    def body(c, _):
        base = pl.multiple_of(c * chunk, chunk)
        y = _conv_chunk(x_ref, base, n_rows, chunk, cw, cb)
        o_ref[pl.ds(base, chunk), :] = _silu(y) if act else y
        return 0

    lax.fori_loop(0, n_rows // chunk, body, 0)


def dwconv(p, col0, width, conv_w, conv_b, act):
    n_rows = p.shape[0]
    chunk = min(n_rows, CONV_CHUNK)
    b0 = col0 // LANES
    return pl.pallas_call(
        functools.partial(_dwconv_kernel, n_rows=n_rows, chunk=chunk, act=act),
        grid=(width // LANES,),
        in_specs=[pl.BlockSpec((n_rows, LANES), lambda j: (0, b0 + j)),
                  pl.BlockSpec((4, LANES), lambda j: (0, j)),
                  pl.BlockSpec((1, LANES), lambda j: (0, j))],
        out_specs=pl.BlockSpec((n_rows, LANES), lambda j: (0, j)),
        out_shape=jax.ShapeDtypeStruct((n_rows, width), F32),
        compiler_params=_cparams("parallel"),
        name="dwconv",
    )(p, conv_w, conv_b.reshape(1, width))


LRU_CHUNK = 256


def _lru_scan_chunk(a, u, h_in, reverse):
    n = a.shape[0]
    row = lax.broadcasted_iota(jnp.int32, (n, 1), 0)
    for k in (1, 2, 4):
        if reverse:
            keep = row < n - k
            a_s = jnp.where(keep, pltpu.roll(a, n - k, 0), 1.0)
            u_s = jnp.where(keep, pltpu.roll(u, n - k, 0), 0.0)
        else:
            keep = row >= k
            a_s = jnp.where(keep, pltpu.roll(a, k, 0), 1.0)
            u_s = jnp.where(keep, pltpu.roll(u, k, 0), 0.0)
        u = u + a * u_s
        a = a * a_s
    n_tiles = n // SUBLANES
    order = range(n_tiles - 1, -1, -1) if reverse else range(n_tiles)
    tiles = [None] * n_tiles
    h = h_in
    for i in order:
        sl = slice(i * SUBLANES, (i + 1) * SUBLANES)
        h = u[sl] + a[sl] * h
        tiles[i] = h
    return jnp.concatenate(tiles, axis=0)


def _lru_kernel(xc_ref, g_ref, w_ref, b_ref, lam_ref, h0_ref, y_ref, hT_ref, hf_ref, *, n_rows, chunk):
    n_chunks = n_rows // chunk

    def direction(d, reverse, finish):
        w_r = w_ref[d, 0].astype(BF16)
        w_i = w_ref[d, 1].astype(BF16)
        b_r = b_ref[d, 0:1, :]
        b_i = b_ref[d, 1:2, :]
        lam = lam_ref[d:d + 1, :]
        log_sig = jnp.minimum(lam, 0.0) - jnp.log1p(jnp.exp(-jnp.abs(lam)))

        def body(c, h):
            cc = (n_chunks - 1 - c) if reverse else c
            base = pl.multiple_of(cc * chunk, chunk)
            x = xc_ref[pl.ds(base, chunk), :]
            xb = x.astype(BF16)
            r = jax.nn.sigmoid(jnp.dot(xb, w_r, preferred_element_type=F32) + b_r)
            i = jax.nn.sigmoid(jnp.dot(xb, w_i, preferred_element_type=F32) + b_i)
            log_a = RGLRU_C * r * log_sig
            a = jnp.exp(log_a)
            u = jnp.sqrt(1.0 - a * a) * (i * x)
            hs = _lru_scan_chunk(a, u, h, reverse)
            finish(base, hs)
            return hs[0:1] if reverse else hs[chunk - 1:chunk]

        return lax.fori_loop(0, n_chunks, body, h0_ref[d:d + 1, :])

    def keep_fwd(base, hs):
        hf_ref[pl.ds(base, chunk), :] = hs

    def emit(base, hs):
        g = g_ref[pl.ds(base, chunk), :]
        y_ref[pl.ds(base, chunk), :] = ((hf_ref[pl.ds(base, chunk), :] + hs) * jax.nn.gelu(g)).astype(BF16)

    hT_ref[0:1, :] = direction(0, False, keep_fwd)
    hT_ref[1:2, :] = direction(1, True, emit)


def rglru(xc, p, g_col0, w_gates, b_gates, lam, h0):
    n_rows = xc.shape[0]
    chunk = min(n_rows, LRU_CHUNK)
    gb = g_col0 // LANES
    return pl.pallas_call(
        functools.partial(_lru_kernel, n_rows=n_rows, chunk=chunk),
        grid=(LRU_BLOCKS,),
        in_specs=[pl.BlockSpec((n_rows, LANES), lambda j: (0, j)),
                  pl.BlockSpec((n_rows, LANES), lambda j: (0, gb + j)),
                  pl.BlockSpec((2, 2, None, LANES, LANES), lambda j: (0, 0, j, 0, 0)),
                  pl.BlockSpec((2, 2, LANES), lambda j: (0, 0, j)),
                  pl.BlockSpec((2, LANES), lambda j: (0, j)),
                  pl.BlockSpec((2, LANES), lambda j: (0, j))],
        out_specs=[pl.BlockSpec((n_rows, LANES), lambda j: (0, j)),
                   pl.BlockSpec((2, LANES), lambda j: (0, j))],
        out_shape=[jax.ShapeDtypeStruct((n_rows, LRU_WIDTH), BF16),
                   jax.ShapeDtypeStruct((2, LRU_WIDTH), F32)],
        scratch_shapes=[pltpu.VMEM((n_rows, LANES), F32)],
        compiler_params=_cparams("parallel"),
        name="rglru",
    )(xc, p, w_gates, b_gates, lam, h0)


def _ssd_direction(xs, bm, cm, dt_raw, dtT_raw, bias_row, bias_col, a_row, a_col, st_ref, reverse):
    q = SSD_CHUNK
    hi = lax.Precision.HIGHEST
    li = lax.broadcasted_iota(jnp.int32, (q, q), 0)
    si = lax.broadcasted_iota(jnp.int32, (q, q), 1)
    causal = (si >= li) if reverse else (si <= li)
    tri = causal.astype(F32)
    dt = jax.nn.softplus(dt_raw + bias_row)
    dtT = jax.nn.softplus(dtT_raw + bias_col)
    adt = dt * a_row
    adtT = dtT * a_col
    cum = jnp.dot(tri, adt, preferred_element_type=F32, precision=hi)
    cumT = lax.dot_general(adtT, tri, (((1,), (1,)), ((), ())), preferred_element_type=F32,
                           precision=hi)
    edge = cum[0:1] if reverse else cum[q - 1:q]

    def ex(v):
        r = v.shape[0]
        first = lax.broadcasted_iota(jnp.int32, (r, LANES), 1) < SSM_HEAD_DIM
        return jnp.concatenate(
            [jnp.where(first, jnp.broadcast_to(v[:, 2 * j:2 * j + 1], (r, LANES)),
                       jnp.broadcast_to(v[:, 2 * j + 1:2 * j + 2], (r, LANES)))
             for j in range(SSM_HEADS // 2)], axis=1)

    cum_e = ex(cum)
    edge_e = ex(edge)
    xdt = xs * ex(dt)
    xdd = (xdt * jnp.exp(edge_e - cum_e)).astype(BF16)
    xdt_b = xdt.astype(BF16)
    grow = jnp.exp(cum_e)
    ys = []
    for g in range(SSM_GROUPS):
        gsl = slice(g * 256, (g + 1) * 256)
        b_g = bm[:, g * SSM_D_STATE:(g + 1) * SSM_D_STATE]
        c_g = cm[:, g * SSM_D_STATE:(g + 1) * SSM_D_STATE].astype(BF16)
        b_gt = b_g.T.astype(BF16)
        gram = jnp.dot(c_g, b_gt, preferred_element_type=F32)
        st = st_ref[:, gsl]
        y_off = grow[:, gsl] * jnp.dot(c_g, st.astype(BF16), preferred_element_type=F32)
        st_ref[:, gsl] = jnp.exp(edge_e[:, gsl]) * st + jnp.dot(b_gt, xdd[:, gsl], preferred_element_type=F32)
        for hh in range(SSM_HEADS // SSM_GROUPS):
            h = g * (SSM_HEADS // SSM_GROUPS) + hh
            seg = cum[:, h:h + 1] - cumT[h:h + 1, :]
            decay = jnp.exp(jnp.where(causal, seg, -jnp.inf))
            m = (gram * decay).astype(BF16)
            hs = slice(h * SSM_HEAD_DIM, (h + 1) * SSM_HEAD_DIM)
            y_d = jnp.dot(m, xdt_b[:, hs], preferred_element_type=F32)
            ys.append(y_d + y_off[:, hh * SSM_HEAD_DIM:(hh + 1) * SSM_HEAD_DIM])
    return jnp.concatenate(ys, axis=1)


def _ssd_kernel(xf_ref, bf_ref, cf_ref, dtf_ref, dtTf_ref, xb_ref, bb_ref, cb_ref, dtb_ref, dtTb_ref,
                bias_ref, biasT_ref, alog_ref, alogT_ref, st0_ref, yf_ref, yb_ref, stT_ref, st_ref):
    c = pl.program_id(0)

    @pl.when(c == 0)
    def _():
        st_ref[...] = st0_ref[...]

    a_row = -jnp.exp(alog_ref[...])
    a_col = -jnp.exp(alogT_ref[...])
    yf_ref[...] = _ssd_direction(xf_ref[...], bf_ref[...], cf_ref[...], dtf_ref[:, 0:SSM_HEADS],
                                 dtTf_ref[0:SSM_HEADS, :], bias_ref[0:1, :], biasT_ref[:, 0:1],
                                 a_row[0:1, :], a_col[:, 0:1], st_ref.at[0], False)
    yb_ref[...] = _ssd_direction(xb_ref[...], bb_ref[...], cb_ref[...], dtb_ref[:, SSM_HEADS:2 * SSM_HEADS],
                                 dtTb_ref[SSM_HEADS:2 * SSM_HEADS, :], bias_ref[1:2, :], biasT_ref[:, 1:2],
                                 a_row[1:2, :], a_col[:, 1:2], st_ref.at[1], True)

    @pl.when(c == pl.num_programs(0) - 1)
    def _():
        stT_ref[...] = st_ref[...]


def ssd_scan(xbc, dt, dtT, dt_bias, a_log, st0):
    n_rows = xbc.shape[0]
    q = SSD_CHUNK
    nc = n_rows // q
    xb = SSM_D_INNER // q
    fwd = lambda c: c
    bwd = lambda c: nc - 1 - c

    def specs(ix):
        return [pl.BlockSpec((q, SSM_D_INNER), lambda c: (ix(c), 0)),
                pl.BlockSpec((q, 2 * SSM_D_STATE), lambda c: (ix(c), 2)),
                pl.BlockSpec((q, 2 * SSM_D_STATE), lambda c: (ix(c), 3)),
                pl.BlockSpec((q, 2 * SSM_HEADS), lambda c: (ix(c), 0)),
                pl.BlockSpec((2 * SSM_HEADS, q), lambda c: (0, ix(c)))]

    small = lambda shape: pl.BlockSpec(shape, lambda c: (0,) * len(shape))
    st_shape = (2, SSM_D_STATE, SSM_D_INNER)
    return pl.pallas_call(
        _ssd_kernel,
        grid=(nc,),
        in_specs=specs(fwd) + specs(bwd) + [small((2, SSM_HEADS)), small((SSM_HEADS, 2)),
                                            small((2, SSM_HEADS)), small((SSM_HEADS, 2)), small(st_shape)],
        out_specs=[pl.BlockSpec((q, SSM_D_INNER), lambda c: (c, 0)),
                   pl.BlockSpec((q, SSM_D_INNER), lambda c: (nc - 1 - c, 0)),
                   small(st_shape)],
        out_shape=[jax.ShapeDtypeStruct((n_rows, SSM_D_INNER), F32),
                   jax.ShapeDtypeStruct((n_rows, SSM_D_INNER), F32),
                   jax.ShapeDtypeStruct(st_shape, F32)],
        scratch_shapes=[pltpu.VMEM(st_shape, F32)],
        compiler_params=_cparams("arbitrary"),
        name="ssd_scan",
    )(xbc, xbc, xbc, dt, dtT, xbc, xbc, xbc, dt, dtT, dt_bias, dt_bias.T, a_log, a_log.T, st0)


def _ssd_finish_kernel(yf_ref, yb_ref, xs_ref, z_ref, d_ref, nw_ref, o_ref):
    hrow = lax.broadcasted_iota(jnp.int32, (SSM_HEADS, SSM_D_INNER), 0)
    hcol = lax.broadcasted_iota(jnp.int32, (SSM_HEADS, SSM_D_INNER), 1) // SSM_HEAD_DIM
    d_e = jnp.sum(jnp.where(hrow == hcol, d_ref[...], 0.0), axis=0, keepdims=True)
    y = yf_ref[...] + yb_ref[...] + d_e * xs_ref[...]
    y = y * _silu(z_ref[...])
    o_ref[...] = (y * lax.rsqrt(jnp.mean(y * y, axis=-1, keepdims=True) + NORM_EPS) * nw_ref[...]).astype(BF16)


def ssd_finish(yf, yb, xbc, p, z_col0, d_skip, norm_w):
    n_rows = yf.shape[0]
    tm = min(n_rows, 512)
    zb = z_col0 // SSM_D_INNER
    blk = lambda cb: pl.BlockSpec((tm, SSM_D_INNER), lambda i: (i, cb))
    return pl.pallas_call(
        _ssd_finish_kernel,
        grid=(n_rows // tm,),
        in_specs=[blk(0), blk(0), blk(0), blk(zb),
                  pl.BlockSpec((SSM_HEADS, 1), lambda i: (0, 0)),
                  pl.BlockSpec((1, SSM_D_INNER), lambda i: (0, 0))],
        out_specs=blk(0),
        out_shape=jax.ShapeDtypeStruct((n_rows, SSM_D_INNER), BF16),
        compiler_params=_cparams("parallel"),
        name="ssd_finish",
    )(yf, yb, xbc, p, d_skip.reshape(SSM_HEADS, 1), norm_w.reshape(1, SSM_D_INNER))


LRU_X_COL = 2560
LRU_G_COL = 3072
SSM_Z_COL = 3584
SSM_XBC_COL = 4096


def _mixers(p, dt_pad, qkv_q, qkv_c, qkv_l, lw, lam_init, lru_h0, ssd_st0):
    ya = diff_attention(qkv_q, qkv_c, qkv_l, lw["da_lambda"], lw["da_subln"], lam_init)
    yb = gqa_attention(qkv_q, qkv_c, qkv_l)
    xc = dwconv(p, LRU_X_COL, LRU_WIDTH, lw["lru_conv_w"], lw["lru_conv_b"], act=False)
    yc, lru_hT = rglru(xc, p, LRU_G_COL, lw["lru_w_gates"], lw["lru_b_gates"], lw["lru_lambda"], lru_h0)
    xbc = dwconv(p, SSM_XBC_COL, 2 * SSM_D_INNER, lw["ssm_conv_w"], lw["ssm_conv_b"], act=True)
    dt = dt_pad[:, :2 * SSM_HEADS]
    yf, ybk, ssd_stT = ssd_scan(xbc, dt, dt.T, lw["ssm_dt_bias"], lw["ssm_a_log"], ssd_st0)
    yd = ssd_finish(yf, ybk, xbc, p, SSM_Z_COL, lw["ssm_d"], lw["ssm_norm"])
    return [ya, yb, yc, yd], lru_hT, ssd_stT


def kernel(x, c, ctx, c_ctx, w_mod, b_mod, mix_norm, ffn_norm, w_in, w_out, da_lambda, da_subln, gqa_q_norm,
           gqa_k_norm, lru_conv_w, lru_conv_b, lru_w_gates, lru_b_gates, lru_lambda, ssm_conv_w, ssm_conv_b,
           ssm_dt_bias, ssm_a_log, ssm_d, ssm_norm, ffn_w_up, ffn_conv_w, ffn_conv_b, ffn_w_down, final_norm):
    depth = w_mod.shape[0]
    d = x.shape[-1]
    xl = x[0]
    xc = ctx[0]
    n_lat, n_ctx = xl.shape[0], xc.shape[0]

    cond8 = jnp.zeros((8, d), F32).at[0].set(c[0]).at[1].set(c_ctx)
    mods = modulation_all(cond8, w_mod, b_mod)
    tabs_l = _rope_tables(n_lat, DA_HEAD_DIM) + _rope_tables(n_lat, GQA_HEAD_DIM)
    tabs_c = _rope_tables(n_ctx, DA_HEAD_DIM) + _rope_tables(n_ctx, GQA_HEAD_DIM)
    lru_zero = jnp.zeros((2, LRU_WIDTH), F32)
    ssd_zero = jnp.zeros((2, SSM_D_STATE, SSM_D_INNER), F32)

    for layer in range(depth):
        need_ctx = layer < depth - 1
        lam_init = 0.8 - 0.6 * math.exp(-0.3 * layer)
        ml = [mods[layer, 0:1, k * d:(k + 1) * d] for k in range(6)]
        mc = [mods[layer, 1:2, k * d:(k + 1) * d] for k in range(6)]
        lw = dict(da_lambda=da_lambda[layer], da_subln=da_subln[layer],
                  lru_conv_w=lru_conv_w[layer], lru_conv_b=lru_conv_b[layer], lru_w_gates=lru_w_gates[layer],
                  lru_b_gates=lru_b_gates[layer], lru_lambda=lru_lambda[layer], ssm_conv_w=ssm_conv_w[layer],
                  ssm_conv_b=ssm_conv_b[layer], ssm_dt_bias=ssm_dt_bias[layer], ssm_a_log=ssm_a_log[layer],
                  ssm_d=ssm_d[layer], ssm_norm=ssm_norm[layer])
        w_dt = jnp.pad(w_in[layer][:, IN_MAIN_COLS:], ((0, 0), (0, LANES - (IN_COLS - IN_MAIN_COLS)))).astype(BF16)
        w_up = ffn_w_up[layer]
        nw_m = mix_norm[layer].reshape(1, d)
        nw_f = ffn_norm[layer].reshape(1, d)

        p_c, dt_c = in_proj(xc, nw_m, mc[0], mc[1], w_in[layer], w_dt)
        p_l, dt_l = in_proj(xl, nw_m, ml[0], ml[1], w_in[layer], w_dt)
        qkv_c = attn_prep(p_c, tabs_c, gqa_q_norm[layer], gqa_k_norm[layer], rope=False)
        qkv_l = attn_prep(p_l, tabs_l, gqa_q_norm[layer], gqa_k_norm[layer], rope=True)

        y_c, lru_h, ssd_st = _mixers(p_c, dt_c, qkv_c, qkv_c, None, lw, lam_init, lru_zero, ssd_zero)
        y_l, _, _ = _mixers(p_l, dt_l, qkv_l, qkv_c, qkv_l, lw, lam_init, lru_h, ssd_st)

        xl = resid_proj(y_l, w_out[layer], xl, ml[2])
        act = ffn_up(xl, nw_f, ml[3], ml[4], w_up, ffn_conv_w[layer], ffn_conv_b[layer])
        xl = resid_proj([act], ffn_w_down[layer], xl, ml[5])
        if need_ctx:
            xc = resid_proj(y_c, w_out[layer], xc, mc[2])
            act = ffn_up(xc, nw_f, mc[3], mc[4], w_up, ffn_conv_w[layer], ffn_conv_b[layer])
            xc = resid_proj([act], ffn_w_down[layer], xc, mc[5])

    return final_rmsnorm(xl, final_norm)[None]
```

```python
import functools
import math

import jax
import jax.numpy as jnp
from jax import lax
from jax.experimental import pallas as pl
from jax.experimental.pallas import tpu as pltpu

F32 = jnp.float32
BF16 = jnp.bfloat16

D_MODEL = 2048
DEPTH = 4
GRID_W = 64
GROUP_WIDTH = 512
DA_HEAD_DIM = 64
DA_HEADS = 4
GQA_HEAD_DIM = 128
GQA_HEADS = 4
GQA_KV_HEADS = 2
LRU_WIDTH = 512
LRU_BLOCKS = 4
RGLRU_C = 8.0
SSM_D_INNER = 512
SSM_HEAD_DIM = 64
SSM_HEADS = 8
SSM_GROUPS = 2
SSM_D_STATE = 128
SSD_CHUNK = 128
D_FF = 5632
ROPE_THETA = 10000.0
NORM_EPS = 1e-6
LOG2E = math.log2(math.e)

LANES = 128
SUBLANES = 8
HALO = 16
IN_COLS = 5136
ATTN_COLS = 2560
QKV_COLS = 3072
VMEM_LIMIT = 56 * 1024 * 1024


def _cparams(*sem):
    return pltpu.CompilerParams(dimension_semantics=sem, vmem_limit_bytes=VMEM_LIMIT)


def _silu(x):
    return x * jax.nn.sigmoid(x)


def _norm_mod(x, nw, shift, scale):
    y = x * lax.rsqrt(jnp.mean(x * x, axis=-1, keepdims=True) + NORM_EPS) * nw
    return y * (1.0 + scale) + shift


def _mod_kernel(c_ref, w_ref, b_ref, o_ref):
    s = _silu(c_ref[...]).astype(BF16)
    o_ref[...] = jnp.dot(s, w_ref[...].astype(BF16), preferred_element_type=F32) + b_ref[...]


def modulation_all(cond8, w_mod, b_mod):
    depth, d, n = w_mod.shape
    tn = 1024
    return pl.pallas_call(
        _mod_kernel,
        grid=(depth, n // tn),
        in_specs=[
            pl.BlockSpec((8, d), lambda l, j: (0, 0)),
            pl.BlockSpec((None, d, tn), lambda l, j: (l, 0, j)),
            pl.BlockSpec((None, 1, tn), lambda l, j: (l, 0, j)),
        ],
        out_specs=pl.BlockSpec((None, 8, tn), lambda l, j: (l, 0, j)),
        out_shape=jax.ShapeDtypeStruct((depth, 8, n), F32),
        compiler_params=_cparams("parallel", "parallel"),
        name="modulation",
    )(cond8, w_mod, b_mod.reshape(depth, 1, n))


IN_MAIN_COLS = 5120
IN_TN = 1024


def _adaln_kernel(x_ref, nw_ref, sh_ref, sc_ref, o_ref):
    o_ref[...] = _norm_mod(x_ref[...], nw_ref[...], sh_ref[...], sc_ref[...]).astype(BF16)


def adaln(x, nw, shift, scale):
    m, d = x.shape
    tm = min(m, 256)
    vec = pl.BlockSpec((1, d), lambda i: (0, 0))
    return pl.pallas_call(
        _adaln_kernel,
        grid=(m // tm,),
        in_specs=[pl.BlockSpec((tm, d), lambda i: (i, 0)), vec, vec, vec],
        out_specs=pl.BlockSpec((tm, d), lambda i: (i, 0)),
        out_shape=jax.ShapeDtypeStruct((m, d), BF16),
        compiler_params=_cparams("parallel"),
        name="adaln",
    )(x, nw, shift, scale)


def _inproj_kernel(xn_ref, w_ref, o_ref, wb_ref):
    @pl.when(pl.program_id(1) == 0)
    def _():
        wb_ref[...] = w_ref[...].astype(BF16)

    o_ref[...] = jnp.dot(xn_ref[...], wb_ref[...], preferred_element_type=F32)


def in_proj(xn, w_in_all, layer):
    m, d = xn.shape
    tm = min(m, 1024)
    return pl.pallas_call(
        _inproj_kernel,
        grid=(IN_MAIN_COLS // IN_TN, m // tm),
        in_specs=[pl.BlockSpec((tm, d), lambda j, i: (i, 0)),
                  pl.BlockSpec((None, d, IN_TN), lambda j, i: (layer, 0, j))],
        out_specs=pl.BlockSpec((tm, IN_TN), lambda j, i: (i, j)),
        out_shape=jax.ShapeDtypeStruct((m, IN_MAIN_COLS), F32),
        scratch_shapes=[pltpu.VMEM((d, IN_TN), BF16)],
        compiler_params=_cparams("parallel", "arbitrary"),
        name="in_proj",
    )(xn, w_in_all)


def _dtproj_kernel(xn_ref, w_ref, o_ref):
    o_ref[...] = jnp.dot(xn_ref[...], w_ref[...], preferred_element_type=F32)


def dt_proj(xn, w_dt_bf):
    m, d = xn.shape
    tm = min(m, 1024)
    return pl.pallas_call(
        _dtproj_kernel,
        grid=(m // tm,),
        in_specs=[pl.BlockSpec((tm, d), lambda i: (i, 0)), pl.BlockSpec((d, LANES), lambda i: (0, 0))],
        out_specs=pl.BlockSpec((tm, LANES), lambda i: (i, 0)),
        out_shape=jax.ShapeDtypeStruct((m, LANES), F32),
        compiler_params=_cparams("parallel"),
        name="dt_proj",
    )(xn, w_dt_bf)


def _resid_kernel(*refs, n_y):
    y_refs = refs[:n_y]
    w_ref, x_ref, g_ref, o_ref, wb_ref = refs[n_y:]

    @pl.when(pl.program_id(1) == 0)
    def _():
        wb_ref[...] = w_ref[...].astype(BF16)

    kk = wb_ref.shape[0] // n_y
    acc = functools.reduce(jnp.add, [
        jnp.dot(y_refs[a][...], wb_ref[a * kk:(a + 1) * kk, :], preferred_element_type=F32) for a in range(n_y)])
    o_ref[...] = x_ref[...] + g_ref[...] * acc


def resid_proj(ys, w_all, layer, x, gate):
    m = x.shape[0]
    _, k, n = w_all.shape
    kk = k // len(ys)
    tm = min(m, 512)
    tn = 1024 if k <= 2048 else 512
    return pl.pallas_call(
        functools.partial(_resid_kernel, n_y=len(ys)),
        grid=(n // tn, m // tm),
        in_specs=[pl.BlockSpec((tm, kk), lambda j, i: (i, 0)) for _ in ys]
                 + [pl.BlockSpec((None, k, tn), lambda j, i: (layer, 0, j)),
                    pl.BlockSpec((tm, tn), lambda j, i: (i, j)),
                    pl.BlockSpec((1, tn), lambda j, i: (0, j))],
        out_specs=pl.BlockSpec((tm, tn), lambda j, i: (i, j)),
        out_shape=jax.ShapeDtypeStruct((m, n), F32),
        scratch_shapes=[pltpu.VMEM((k, tn), BF16)],
        compiler_params=_cparams("parallel", "arbitrary"),
        name="resid_proj",
    )(*ys, w_all, x, gate)


def _ffn_up_kernel(xp_ref, x_ref, xx_ref, wg_ref, wu_ref, cw_ref, cb_ref, o_ref, xn_ref, g_ref, wb_ref,
                   *, tm, n_row_blocks):
    i = pl.program_id(1)

    @pl.when(i == 0)
    def _():
        wb_ref[0] = wg_ref[...].astype(BF16)
        wb_ref[1] = wu_ref[...].astype(BF16)

    xn_ref[pl.ds(0, HALO), :] = jnp.where(i == 0, jnp.zeros_like(xp_ref), xp_ref[...])
    xn_ref[pl.ds(HALO, tm), :] = x_ref[...]
    xn_ref[pl.ds(HALO + tm, HALO), :] = jnp.where(i == n_row_blocks - 1, jnp.zeros_like(xx_ref), xx_ref[...])
    g_ref[...] = jnp.dot(xn_ref[...], wb_ref[0], preferred_element_type=F32)
    up = jnp.dot(x_ref[...], wb_ref[1], preferred_element_type=F32)
    cw = cw_ref[...]
    gc = (cb_ref[...] + cw[0:1] * g_ref[pl.ds(HALO - 1, tm), :] + cw[1:2] * g_ref[pl.ds(HALO, tm), :]
          + cw[2:3] * g_ref[pl.ds(HALO + 1, tm), :])
    o_ref[...] = (_silu(gc) * up).astype(BF16)


def ffn_up(xn, w_up_all, layer, conv_w, conv_b):
    m, d = xn.shape
    dff = w_up_all.shape[2] // 2
    tm = min(m, 1024)
    tn = 512
    nrb = m // tm
    hb = tm // HALO
    nhb = m // HALO
    return pl.pallas_call(
        functools.partial(_ffn_up_kernel, tm=tm, n_row_blocks=nrb),
        grid=(dff // tn, nrb),
        in_specs=[pl.BlockSpec((HALO, d), lambda j, i: (jnp.maximum(i * hb - 1, 0), 0)),
                  pl.BlockSpec((tm, d), lambda j, i: (i, 0)),
                  pl.BlockSpec((HALO, d), lambda j, i: (jnp.minimum((i + 1) * hb, nhb - 1), 0)),
                  pl.BlockSpec((None, d, tn), lambda j, i: (layer, 0, j)),
                  pl.BlockSpec((None, d, tn), lambda j, i: (layer, 0, j + dff // tn)),
                  pl.BlockSpec((3, tn), lambda j, i: (0, j)),
                  pl.BlockSpec((1, tn), lambda j, i: (0, j))],
        out_specs=pl.BlockSpec((tm, tn), lambda j, i: (i, j)),
        out_shape=jax.ShapeDtypeStruct((m, dff), BF16),
        scratch_shapes=[pltpu.VMEM((tm + 2 * HALO, d), BF16), pltpu.VMEM((tm + 2 * HALO, tn), F32),
                        pltpu.VMEM((2, d, tn), BF16)],
        compiler_params=_cparams("parallel", "arbitrary"),
        name="ffn_up",
    )(xn, xn, xn, w_up_all, w_up_all, conv_w, conv_b.reshape(1, dff))


def _rmsnorm_kernel(x_ref, w_ref, o_ref):
    x = x_ref[...]
    o_ref[...] = x * lax.rsqrt(jnp.mean(x * x, axis=-1, keepdims=True) + NORM_EPS) * w_ref[...]


def final_rmsnorm(x, w):
    m, d = x.shape
    tm = min(m, 512)
    return pl.pallas_call(
        _rmsnorm_kernel,
        grid=(m // tm,),
        in_specs=[pl.BlockSpec((tm, d), lambda i: (i, 0)), pl.BlockSpec((1, d), lambda i: (0, 0))],
        out_specs=pl.BlockSpec((tm, d), lambda i: (i, 0)),
        out_shape=jax.ShapeDtypeStruct((m, d), F32),
        compiler_params=_cparams("parallel"),
        name="final_norm",
    )(x, w.reshape(1, d))


def _rope_tables(n_tok, head_dim):
    q = head_dim // 4
    lane = jnp.arange(LANES, dtype=jnp.int32)
    u = lane % head_dim
    region = u // q
    freqs = jnp.power(ROPE_THETA, -(u % q).astype(F32) / q)[None, :]
    ang_r = jnp.arange(n_tok // GRID_W, dtype=jnp.int32).astype(F32)[:, None] * freqs
    ang_c = jnp.arange(GRID_W, dtype=jnp.int32).astype(F32)[:, None] * freqs
    by_row = (region < 2)[None, None, :]

    def per_token(f):
        return jnp.where(by_row, f(ang_r)[:, None, :], f(ang_c)[None, :, :]).reshape(n_tok, LANES)

    cos, sin = per_token(jnp.cos), per_token(jnp.sin)
    first = (region % 2 == 0)[None, :]
    return cos, jnp.where(first, -sin, 0.0), jnp.where(first, 0.0, sin)


def _prep_kernel(p_ref, cd_ref, ad_ref, bd_ref, cg_ref, ag_ref, bg_ref, qn_ref, kn_ref, o_ref, *, rope):
    lane = lax.broadcasted_iota(jnp.int32, (1, LANES), 1)
    lo = lane < DA_HEAD_DIM

    def blk(b):
        return p_ref[:, b * LANES:(b + 1) * LANES]

    def put(b, v):
        o_ref[:, b * LANES:(b + 1) * LANES] = v.astype(BF16)

    def rot(x, cos, sa, sb, quarter):
        if not rope:
            return x
        return (x * cos[...] + pltpu.roll(x, LANES - quarter, 1) * sa[...]
                + pltpu.roll(x, quarter, 1) * sb[...])

    def rms(x, w):
        return x * lax.rsqrt(jnp.mean(x * x, axis=-1, keepdims=True) + NORM_EPS) * w[...]

    qd = DA_HEAD_DIM // 4
    qg = GQA_HEAD_DIM // 4
    for h in range(DA_HEADS):
        q = rot(blk(h), cd_ref, ad_ref, bd_ref, qd) * (DA_HEAD_DIM ** -0.5 * LOG2E)
        put(h, jnp.where(lo, q, 0.0))
        put(4 + h, jnp.where(lo, 0.0, q))
        put(8 + h, rot(blk(4 + h), cd_ref, ad_ref, bd_ref, qd))
        put(12 + h, blk(8 + h))
    for h in range(GQA_HEADS):
        q = rot(rms(blk(12 + h), qn_ref), cg_ref, ag_ref, bg_ref, qg) * (GQA_HEAD_DIM ** -0.5 * LOG2E)
        put(16 + h, q)
    for h in range(GQA_KV_HEADS):
        put(20 + h, rot(rms(blk(16 + h), kn_ref), cg_ref, ag_ref, bg_ref, qg))
        put(22 + h, blk(18 + h))


def attn_prep(p, tables, q_norm, k_norm, rope):
    m = p.shape[0]
    tm = min(m, 512)
    tab = pl.BlockSpec((tm, LANES), lambda i: (i, 0))
    vec = pl.BlockSpec((1, LANES), lambda i: (0, 0))
    return pl.pallas_call(
        functools.partial(_prep_kernel, rope=rope),
        grid=(m // tm,),
        in_specs=[pl.BlockSpec((tm, ATTN_COLS), lambda i: (i, 0))] + [tab] * 6 + [vec, vec],
        out_specs=pl.BlockSpec((tm, QKV_COLS), lambda i: (i, 0)),
        out_shape=jax.ShapeDtypeStruct((m, QKV_COLS), BF16),
        compiler_params=_cparams("parallel"),
        name="attn_prep",
    )(p, *tables, q_norm.reshape(1, LANES), k_norm.reshape(1, LANES))


def _attn_kernel(*refs, tk, n_kv, diff, lam_init):
    refs = list(refs)
    qa_ref, qb_ref, kc_ref, vc_ref = refs[:4]
    del refs[:4]
    if n_kv:
        kl_ref, vl_ref = refs[:2]
        del refs[:2]
    if diff:
        lam_ref, sw_ref = refs[:2]
        del refs[:2]
    o_ref = refs.pop(0)
    if n_kv:
        s_ref = refs.pop(0)
    sc_ref, m_ref, l_ref, acc_ref = refs
    tq = qa_ref.shape[0]
    nt = (((1,), (1,)), ((), ()))
    q2 = jnp.concatenate([qa_ref[...], qb_ref[...]], axis=0)

    def lane_tiles(x):
        return [x[:, t * LANES:(t + 1) * LANES] for t in range(x.shape[1] // LANES)]

    sc = lax.dot_general(q2, kc_ref[...], nt, preferred_element_type=F32)
    sc_ref[...] = sc
    m_ref[...] = functools.reduce(jnp.maximum, lane_tiles(sc))

    def score_chunk(c, _):
        ks = kl_ref[pl.ds(pl.multiple_of(c * tk, tk), tk), :]
        s = lax.dot_general(q2, ks, nt, preferred_element_type=F32)
        s_ref[c] = s
        m_ref[...] = functools.reduce(jnp.maximum, lane_tiles(s), m_ref[...])
        return 0

    if n_kv:
        lax.fori_loop(0, n_kv, score_chunk, 0)
    m_ref[...] = jnp.broadcast_to(jnp.max(m_ref[...], axis=-1, keepdims=True), (2 * tq, LANES))

    def weigh(s, vs):
        m = m_ref[...]
        ps = [jnp.exp2(st - m) for st in lane_tiles(s)]
        p = jnp.concatenate([pt.astype(BF16) for pt in ps], axis=1)
        return functools.reduce(jnp.add, ps), jnp.dot(p, vs, preferred_element_type=F32)

    l_ref[...], acc_ref[...] = weigh(sc_ref[...], vc_ref[...])

    def value_chunk(c, _):
        lsum, pv = weigh(s_ref[c], vl_ref[pl.ds(pl.multiple_of(c * tk, tk), tk), :])
        l_ref[...] += lsum
        acc_ref[...] += pv
        return 0

    if n_kv:
        lax.fori_loop(0, n_kv, value_chunk, 0)
    o = acc_ref[...] / jnp.sum(l_ref[...], axis=-1, keepdims=True)
    oa = o[0:tq]
    ob = o[tq:2 * tq]
    if diff:
        lv = lam_ref[...]
        lam = (jnp.exp(jnp.sum(lv[0:1] * lv[1:2], keepdims=True))
               - jnp.exp(jnp.sum(lv[2:3] * lv[3:4], keepdims=True)) + lam_init)
        o = oa - lam * ob
        o = o * lax.rsqrt(jnp.mean(o * o, axis=-1, keepdims=True) + NORM_EPS) * sw_ref[...]
        o_ref[...] = (o * (1.0 - lam_init)).astype(BF16)
    else:
        o_ref[:, 0:LANES] = oa.astype(BF16)
        o_ref[:, LANES:2 * LANES] = ob.astype(BF16)


ATTN_TQ = 256
ATTN_TK = (4096, 2048, 1024, 512, 256, 128)


def _attention(qkv_q, qkv_c, qkv_l, extra, extra_specs, cols, n_heads, out_width, name, **static):
    qa_col, qb_col, k_col, v_col = cols
    sq, n_ctx = qkv_q.shape[0], qkv_c.shape[0]
    tq = min(sq, ATTN_TQ)
    n_lat = 0 if qkv_l is None else qkv_l.shape[0]
    tk = next((t for t in ATTN_TK if n_lat and n_lat % t == 0), 0)
    n_kv = n_lat // tk if n_lat else 0

    def kv_spec(rows, col):
        return pl.BlockSpec((rows, LANES), lambda h, i: (0, col(h)))

    in_specs = [pl.BlockSpec((tq, LANES), lambda h, i: (i, qa_col(h))),
                pl.BlockSpec((tq, LANES), lambda h, i: (i, qb_col(h))),
                kv_spec(n_ctx, k_col), kv_spec(n_ctx, v_col)]
    args = [qkv_q, qkv_q, qkv_c, qkv_c]
    vec = pltpu.VMEM((2 * tq, LANES), F32)
    scratch = [pltpu.VMEM((2 * tq, n_ctx), F32), vec, vec, vec]
    if n_kv:
        in_specs += [kv_spec(n_lat, k_col), kv_spec(n_lat, v_col)]
        args += [qkv_l, qkv_l]
        scratch = [pltpu.VMEM((n_kv, 2 * tq, tk), F32)] + scratch
    return pl.pallas_call(
        functools.partial(_attn_kernel, tk=tk, n_kv=n_kv, **static),
        grid=(n_heads, sq // tq),
        in_specs=in_specs + extra_specs,
        out_specs=pl.BlockSpec((tq, out_width), lambda h, i: (i, h)),
        out_shape=jax.ShapeDtypeStruct((sq, GROUP_WIDTH), BF16),
        scratch_shapes=scratch,
        compiler_params=_cparams("parallel", "parallel"),
        name=name,
    )(*args, *extra)


def diff_attention(qkv_q, qkv_c, qkv_l, da_lambda, subln_w, lam_init):
    cols = (lambda h: h, lambda h: 4 + h, lambda h: 8 + h, lambda h: 12 + h)
    extra_specs = [pl.BlockSpec((4, DA_HEAD_DIM), lambda h, i: (0, 0)), pl.BlockSpec((1, LANES), lambda h, i: (0, 0))]
    return _attention(qkv_q, qkv_c, qkv_l, [da_lambda, subln_w.reshape(1, LANES)], extra_specs, cols,
                      DA_HEADS, LANES, "diff_attn", diff=True, lam_init=lam_init)


def gqa_attention(qkv_q, qkv_c, qkv_l):
    cols = (lambda h: 16 + 2 * h, lambda h: 17 + 2 * h, lambda h: 20 + h, lambda h: 22 + h)
    return _attention(qkv_q, qkv_c, qkv_l, [], [], cols, GQA_KV_HEADS, 2 * LANES, "gqa_attn",
                      diff=False, lam_init=0.0)


CONV_CHUNK = 256


def _conv_chunk(x_ref, base, n_rows, chunk, cw, cb):
    cur = x_ref[pl.ds(base, chunk), :]
    prev = x_ref[pl.ds(pl.multiple_of(jnp.maximum(base - 8, 0), 8), 8), :]
    nxt = x_ref[pl.ds(pl.multiple_of(jnp.minimum(base + chunk, n_rows - 8), 8), 8), :]
    prev = jnp.where(base == 0, 0.0, prev)
    nxt = jnp.where(base + chunk >= n_rows, 0.0, nxt)
    cat = jnp.concatenate([prev, cur, nxt], axis=0)
    n = chunk + 16
    xm2 = pltpu.roll(cat, 2, 0)[8:8 + chunk]
    xm1 = pltpu.roll(cat, 1, 0)[8:8 + chunk]
    xp1 = pltpu.roll(cat, n - 1, 0)[8:8 + chunk]
    return cb + cw[0:1] * xm2 + cw[1:2] * xm1 + cw[2:3] * cur + cw[3:4] * xp1


def _dwconv_kernel(x_ref, w_ref, b_ref, o_ref, *, n_rows, chunk, act):
    cw = w_ref[...]
    cb = b_ref[...]

    def body(c, _):
        base = pl.multiple_of(c * chunk, chunk)
        y = _conv_chunk(x_ref, base, n_rows, chunk, cw, cb)
        o_ref[pl.ds(base, chunk), :] = _silu(y) if act else y
        return 0

    lax.fori_loop(0, n_rows // chunk, body, 0)


def dwconv(p, col0, width, conv_w, conv_b, act):
    n_rows = p.shape[0]
    chunk = min(n_rows, CONV_CHUNK)
    b0 = col0 // LANES
    return pl.pallas_call(
        functools.partial(_dwconv_kernel, n_rows=n_rows, chunk=chunk, act=act),
        grid=(width // LANES,),
        in_specs=[pl.BlockSpec((n_rows, LANES), lambda j: (0, b0 + j)),
                  pl.BlockSpec((4, LANES), lambda j: (0, j)),
                  pl.BlockSpec((1, LANES), lambda j: (0, j))],
        out_specs=pl.BlockSpec((n_rows, LANES), lambda j: (0, j)),
        out_shape=jax.ShapeDtypeStruct((n_rows, width), F32),
        compiler_params=_cparams("parallel"),
        name="dwconv",
    )(p, conv_w, conv_b.reshape(1, width))


LRU_CHUNK = 256


def _lru_scan_chunk(a, u, h_in, reverse):
    n = a.shape[0]
    row = lax.broadcasted_iota(jnp.int32, (n, 1), 0)
    for k in (1, 2, 4):
        if reverse:
            keep = row < n - k
            a_s = jnp.where(keep, pltpu.roll(a, n - k, 0), 1.0)
            u_s = jnp.where(keep, pltpu.roll(u, n - k, 0), 0.0)
        else:
            keep = row >= k
            a_s = jnp.where(keep, pltpu.roll(a, k, 0), 1.0)
            u_s = jnp.where(keep, pltpu.roll(u, k, 0), 0.0)
        u = u + a * u_s
        a = a * a_s
    n_tiles = n // SUBLANES
    order = range(n_tiles - 1, -1, -1) if reverse else range(n_tiles)
    tiles = [None] * n_tiles
    h = h_in
    for i in order:
        sl = slice(i * SUBLANES, (i + 1) * SUBLANES)
        h = u[sl] + a[sl] * h
        tiles[i] = h
    return jnp.concatenate(tiles, axis=0)


def _lru_kernel(xc_ref, g_ref, w_ref, b_ref, lam_ref, h0_ref, y_ref, hT_ref, hf_ref, *, n_rows, chunk):
    n_chunks = n_rows // chunk

    def direction(d, reverse, finish):
        w_r = w_ref[d, 0].astype(BF16)
        w_i = w_ref[d, 1].astype(BF16)
        b_r = b_ref[d, 0:1, :]
        b_i = b_ref[d, 1:2, :]
        lam = lam_ref[d:d + 1, :]
        log_sig = jnp.minimum(lam, 0.0) - jnp.log1p(jnp.exp(-jnp.abs(lam)))

        def body(c, h):
            cc = (n_chunks - 1 - c) if reverse else c
            base = pl.multiple_of(cc * chunk, chunk)
            x = xc_ref[pl.ds(base, chunk), :]
            xb = x.astype(BF16)
            r = jax.nn.sigmoid(jnp.dot(xb, w_r, preferred_element_type=F32) + b_r)
            i = jax.nn.sigmoid(jnp.dot(xb, w_i, preferred_element_type=F32) + b_i)
            log_a = RGLRU_C * r * log_sig
            a = jnp.exp(log_a)
            u = jnp.sqrt(1.0 - a * a) * (i * x)
            hs = _lru_scan_chunk(a, u, h, reverse)
            finish(base, hs)
            return hs[0:1] if reverse else hs[chunk - 1:chunk]

        return lax.fori_loop(0, n_chunks, body, h0_ref[d:d + 1, :])

    def keep_fwd(base, hs):
        hf_ref[pl.ds(base, chunk), :] = hs

    def emit(base, hs):
        g = g_ref[pl.ds(base, chunk), :]
        y_ref[pl.ds(base, chunk), :] = ((hf_ref[pl.ds(base, chunk), :] + hs) * jax.nn.gelu(g)).astype(BF16)

    hT_ref[0:1, :] = direction(0, False, keep_fwd)
    hT_ref[1:2, :] = direction(1, True, emit)


def rglru(xc, p, g_col0, w_gates, b_gates, lam, h0):
    n_rows = xc.shape[0]
    chunk = min(n_rows, LRU_CHUNK)
    gb = g_col0 // LANES
    return pl.pallas_call(
        functools.partial(_lru_kernel, n_rows=n_rows, chunk=chunk),
        grid=(LRU_BLOCKS,),
        in_specs=[pl.BlockSpec((n_rows, LANES), lambda j: (0, j)),
                  pl.BlockSpec((n_rows, LANES), lambda j: (0, gb + j)),
                  pl.BlockSpec((2, 2, None, LANES, LANES), lambda j: (0, 0, j, 0, 0)),
                  pl.BlockSpec((2, 2, LANES), lambda j: (0, 0, j)),
                  pl.BlockSpec((2, LANES), lambda j: (0, j)),
                  pl.BlockSpec((2, LANES), lambda j: (0, j))],
        out_specs=[pl.BlockSpec((n_rows, LANES), lambda j: (0, j)),
                   pl.BlockSpec((2, LANES), lambda j: (0, j))],
        out_shape=[jax.ShapeDtypeStruct((n_rows, LRU_WIDTH), BF16),
                   jax.ShapeDtypeStruct((2, LRU_WIDTH), F32)],
        scratch_shapes=[pltpu.VMEM((n_rows, LANES), F32)],
        compiler_params=_cparams("parallel"),
        name="rglru",
    )(xc, p, w_gates, b_gates, lam, h0)


def _ssd_direction(xs, bm, cm, dt_raw, dtT_raw, bias_row, bias_col, a_row, a_col, st_ref, reverse):
    q = SSD_CHUNK
    hi = lax.Precision.HIGHEST
    li = lax.broadcasted_iota(jnp.int32, (q, q), 0)
    si = lax.broadcasted_iota(jnp.int32, (q, q), 1)
    causal = (si >= li) if reverse else (si <= li)
    tri = causal.astype(F32)
    dt = jax.nn.softplus(dt_raw + bias_row)
    dtT = jax.nn.softplus(dtT_raw + bias_col)
    adt = dt * a_row
    adtT = dtT * a_col
    cum = jnp.dot(tri, adt, preferred_element_type=F32, precision=hi)
    cumT = lax.dot_general(adtT, tri, (((1,), (1,)), ((), ())), preferred_element_type=F32,
                           precision=hi)
    edge = cum[0:1] if reverse else cum[q - 1:q]

    def ex(v):
        r = v.shape[0]
        first = lax.broadcasted_iota(jnp.int32, (r, LANES), 1) < SSM_HEAD_DIM
        return jnp.concatenate(
            [jnp.where(first, jnp.broadcast_to(v[:, 2 * j:2 * j + 1], (r, LANES)),
                       jnp.broadcast_to(v[:, 2 * j + 1:2 * j + 2], (r, LANES)))
             for j in range(SSM_HEADS // 2)], axis=1)

    cum_e = ex(cum)
    edge_e = ex(edge)
    xdt = xs * ex(dt)
    xdd = (xdt * jnp.exp(edge_e - cum_e)).astype(BF16)
    xdt_b = xdt.astype(BF16)
    grow = jnp.exp(cum_e)
    ys = []
    for g in range(SSM_GROUPS):
        gsl = slice(g * 256, (g + 1) * 256)
        b_g = bm[:, g * SSM_D_STATE:(g + 1) * SSM_D_STATE]
        c_g = cm[:, g * SSM_D_STATE:(g + 1) * SSM_D_STATE].astype(BF16)
        b_gt = b_g.T.astype(BF16)
        gram = jnp.dot(c_g, b_gt, preferred_element_type=F32)
        st = st_ref[:, gsl]
        y_off = grow[:, gsl] * jnp.dot(c_g, st.astype(BF16), preferred_element_type=F32)
        st_ref[:, gsl] = jnp.exp(edge_e[:, gsl]) * st + jnp.dot(b_gt, xdd[:, gsl], preferred_element_type=F32)
        for hh in range(SSM_HEADS // SSM_GROUPS):
            h = g * (SSM_HEADS // SSM_GROUPS) + hh
            seg = cum[:, h:h + 1] - cumT[h:h + 1, :]
            decay = jnp.exp(jnp.where(causal, seg, -jnp.inf))
            m = (gram * decay).astype(BF16)
            hs = slice(h * SSM_HEAD_DIM, (h + 1) * SSM_HEAD_DIM)
            y_d = jnp.dot(m, xdt_b[:, hs], preferred_element_type=F32)
            ys.append(y_d + y_off[:, hh * SSM_HEAD_DIM:(hh + 1) * SSM_HEAD_DIM])
    return jnp.concatenate(ys, axis=1)


def _ssd_kernel(xf_ref, bf_ref, cf_ref, dtf_ref, dtTf_ref, xb_ref, bb_ref, cb_ref, dtb_ref, dtTb_ref,
                bias_ref, biasT_ref, alog_ref, alogT_ref, st0_ref, yf_ref, yb_ref, stT_ref, st_ref):
    c = pl.program_id(0)

    @pl.when(c == 0)
    def _():
        st_ref[...] = st0_ref[...]

    a_row = -jnp.exp(alog_ref[...])
    a_col = -jnp.exp(alogT_ref[...])
    yf_ref[...] = _ssd_direction(xf_ref[...], bf_ref[...], cf_ref[...], dtf_ref[:, 0:SSM_HEADS],
                                 dtTf_ref[0:SSM_HEADS, :], bias_ref[0:1, :], biasT_ref[:, 0:1],
                                 a_row[0:1, :], a_col[:, 0:1], st_ref.at[0], False)
    yb_ref[...] = _ssd_direction(xb_ref[...], bb_ref[...], cb_ref[...], dtb_ref[:, SSM_HEADS:2 * SSM_HEADS],
                                 dtTb_ref[SSM_HEADS:2 * SSM_HEADS, :], bias_ref[1:2, :], biasT_ref[:, 1:2],
                                 a_row[1:2, :], a_col[:, 1:2], st_ref.at[1], True)

    @pl.when(c == pl.num_programs(0) - 1)
    def _():
        stT_ref[...] = st_ref[...]


def ssd_scan(xbc, dt, dtT, dt_bias, a_log, st0):
    n_rows = xbc.shape[0]
    q = SSD_CHUNK
    nc = n_rows // q
    xb = SSM_D_INNER // q
    fwd = lambda c: c
    bwd = lambda c: nc - 1 - c

    def specs(ix):
        return [pl.BlockSpec((q, SSM_D_INNER), lambda c: (ix(c), 0)),
                pl.BlockSpec((q, 2 * SSM_D_STATE), lambda c: (ix(c), 2)),
                pl.BlockSpec((q, 2 * SSM_D_STATE), lambda c: (ix(c), 3)),
                pl.BlockSpec((q, 2 * SSM_HEADS), lambda c: (ix(c), 0)),
                pl.BlockSpec((2 * SSM_HEADS, q), lambda c: (0, ix(c)))]

    small = lambda shape: pl.BlockSpec(shape, lambda c: (0,) * len(shape))
    st_shape = (2, SSM_D_STATE, SSM_D_INNER)
    return pl.pallas_call(
        _ssd_kernel,
        grid=(nc,),
        in_specs=specs(fwd) + specs(bwd) + [small((2, SSM_HEADS)), small((SSM_HEADS, 2)),
                                            small((2, SSM_HEADS)), small((SSM_HEADS, 2)), small(st_shape)],
        out_specs=[pl.BlockSpec((q, SSM_D_INNER), lambda c: (c, 0)),
                   pl.BlockSpec((q, SSM_D_INNER), lambda c: (nc - 1 - c, 0)),
                   small(st_shape)],
        out_shape=[jax.ShapeDtypeStruct((n_rows, SSM_D_INNER), F32),
                   jax.ShapeDtypeStruct((n_rows, SSM_D_INNER), F32),
                   jax.ShapeDtypeStruct(st_shape, F32)],
        scratch_shapes=[pltpu.VMEM(st_shape, F32)],
        compiler_params=_cparams("arbitrary"),
        name="ssd_scan",
    )(xbc, xbc, xbc, dt, dtT, xbc, xbc, xbc, dt, dtT, dt_bias, dt_bias.T, a_log, a_log.T, st0)


def _ssd_finish_kernel(yf_ref, yb_ref, xs_ref, z_ref, d_ref, nw_ref, o_ref):
    hrow = lax.broadcasted_iota(jnp.int32, (SSM_HEADS, SSM_D_INNER), 0)
    hcol = lax.broadcasted_iota(jnp.int32, (SSM_HEADS, SSM_D_INNER), 1) // SSM_HEAD_DIM
    d_e = jnp.sum(jnp.where(hrow == hcol, d_ref[...], 0.0), axis=0, keepdims=True)
    y = yf_ref[...] + yb_ref[...] + d_e * xs_ref[...]
    y = y * _silu(z_ref[...])
    o_ref[...] = (y * lax.rsqrt(jnp.mean(y * y, axis=-1, keepdims=True) + NORM_EPS) * nw_ref[...]).astype(BF16)


def ssd_finish(yf, yb, xbc, p, z_col0, d_skip, norm_w):
    n_rows = yf.shape[0]
    tm = min(n_rows, 512)
    zb = z_col0 // SSM_D_INNER
    blk = lambda cb: pl.BlockSpec((tm, SSM_D_INNER), lambda i: (i, cb))
    return pl.pallas_call(
        _ssd_finish_kernel,
        grid=(n_rows // tm,),
        in_specs=[blk(0), blk(0), blk(0), blk(zb),
                  pl.BlockSpec((SSM_HEADS, 1), lambda i: (0, 0)),
                  pl.BlockSpec((1, SSM_D_INNER), lambda i: (0, 0))],
        out_specs=blk(0),
        out_shape=jax.ShapeDtypeStruct((n_rows, SSM_D_INNER), BF16),
        compiler_params=_cparams("parallel"),
        name="ssd_finish",
    )(yf, yb, xbc, p, d_skip.reshape(SSM_HEADS, 1), norm_w.reshape(1, SSM_D_INNER))


LRU_X_COL = 2560
LRU_G_COL = 3072
SSM_Z_COL = 3584
SSM_XBC_COL = 4096


def _mixers(p, dt_pad, qkv_q, qkv_c, qkv_l, lw, lam_init, lru_h0, ssd_st0):
    ya = diff_attention(qkv_q, qkv_c, qkv_l, lw["da_lambda"], lw["da_subln"], lam_init)
    yb = gqa_attention(qkv_q, qkv_c, qkv_l)
    xc = dwconv(p, LRU_X_COL, LRU_WIDTH, lw["lru_conv_w"], lw["lru_conv_b"], act=False)
    yc, lru_hT = rglru(xc, p, LRU_G_COL, lw["lru_w_gates"], lw["lru_b_gates"], lw["lru_lambda"], lru_h0)
    xbc = dwconv(p, SSM_XBC_COL, 2 * SSM_D_INNER, lw["ssm_conv_w"], lw["ssm_conv_b"], act=True)
    dt = dt_pad[:, :2 * SSM_HEADS]
    yf, ybk, ssd_stT = ssd_scan(xbc, dt, dt.T, lw["ssm_dt_bias"], lw["ssm_a_log"], ssd_st0)
    yd = ssd_finish(yf, ybk, xbc, p, SSM_Z_COL, lw["ssm_d"], lw["ssm_norm"])
    return [ya, yb, yc, yd], lru_hT, ssd_stT


def kernel(x, c, ctx, c_ctx, w_mod, b_mod, mix_norm, ffn_norm, w_in, w_out, da_lambda, da_subln, gqa_q_norm,
           gqa_k_norm, lru_conv_w, lru_conv_b, lru_w_gates, lru_b_gates, lru_lambda, ssm_conv_w, ssm_conv_b,
           ssm_dt_bias, ssm_a_log, ssm_d, ssm_norm, ffn_w_up, ffn_conv_w, ffn_conv_b, ffn_w_down, final_norm):
    depth = w_mod.shape[0]
    d = x.shape[-1]
    xl = x[0]
    xc = ctx[0]
    n_lat, n_ctx = xl.shape[0], xc.shape[0]

    cond8 = jnp.zeros((8, d), F32).at[0].set(c[0]).at[1].set(c_ctx)
    mods = modulation_all(cond8, w_mod, b_mod)
    tabs_l = _rope_tables(n_lat, DA_HEAD_DIM) + _rope_tables(n_lat, GQA_HEAD_DIM)
    tabs_c = _rope_tables(n_ctx, DA_HEAD_DIM) + _rope_tables(n_ctx, GQA_HEAD_DIM)
    lru_zero = jnp.zeros((2, LRU_WIDTH), F32)
    ssd_zero = jnp.zeros((2, SSM_D_STATE, SSM_D_INNER), F32)

    for layer in range(depth):
        need_ctx = layer < depth - 1
        lam_init = 0.8 - 0.6 * math.exp(-0.3 * layer)
        ml = [mods[layer, 0:1, k * d:(k + 1) * d] for k in range(6)]
        mc = [mods[layer, 1:2, k * d:(k + 1) * d] for k in range(6)]
        lw = dict(da_lambda=da_lambda[layer], da_subln=da_subln[layer],
                  lru_conv_w=lru_conv_w[layer], lru_conv_b=lru_conv_b[layer], lru_w_gates=lru_w_gates[layer],
                  lru_b_gates=lru_b_gates[layer], lru_lambda=lru_lambda[layer], ssm_conv_w=ssm_conv_w[layer],
                  ssm_conv_b=ssm_conv_b[layer], ssm_dt_bias=ssm_dt_bias[layer], ssm_a_log=ssm_a_log[layer],
                  ssm_d=ssm_d[layer], ssm_norm=ssm_norm[layer])
        w_dt = jnp.pad(w_in[layer][:, IN_MAIN_COLS:], ((0, 0), (0, LANES - (IN_COLS - IN_MAIN_COLS)))).astype(BF16)
        nw_m = mix_norm[layer].reshape(1, d)
        nw_f = ffn_norm[layer].reshape(1, d)

        xn_c = adaln(xc, nw_m, mc[0], mc[1])
        xn_l = adaln(xl, nw_m, ml[0], ml[1])
        p_c, dt_c = in_proj(xn_c, w_in, layer), dt_proj(xn_c, w_dt)
        p_l, dt_l = in_proj(xn_l, w_in, layer), dt_proj(xn_l, w_dt)
        qkv_c = attn_prep(p_c, tabs_c, gqa_q_norm[layer], gqa_k_norm[layer], rope=False)
        qkv_l = attn_prep(p_l, tabs_l, gqa_q_norm[layer], gqa_k_norm[layer], rope=True)

        y_c, lru_h, ssd_st = _mixers(p_c, dt_c, qkv_c, qkv_c, None, lw, lam_init, lru_zero, ssd_zero)
        y_l, _, _ = _mixers(p_l, dt_l, qkv_l, qkv_c, qkv_l, lw, lam_init, lru_h, ssd_st)

        xl = resid_proj(y_l, w_out, layer, xl, ml[2])
        act = ffn_up(adaln(xl, nw_f, ml[3], ml[4]), ffn_w_up, layer, ffn_conv_w[layer], ffn_conv_b[layer])
        xl = resid_proj([act], ffn_w_down, layer, xl, ml[5])
        if need_ctx:
            xc = resid_proj(y_c, w_out, layer, xc, mc[2])
            act = ffn_up(adaln(xc, nw_f, mc[3], mc[4]), ffn_w_up, layer, ffn_conv_w[layer], ffn_conv_b[layer])
            xc = resid_proj([act], ffn_w_down, layer, xc, mc[5])

    return final_rmsnorm(xl, final_norm)[None]
```

```python
import functools
import math

import jax
import jax.numpy as jnp
from jax import lax
from jax.experimental import pallas as pl
from jax.experimental.pallas import tpu as pltpu

F32 = jnp.float32
BF16 = jnp.bfloat16

D_MODEL = 2048
DEPTH = 4
GRID_W = 64
GROUP_WIDTH = 512
DA_HEAD_DIM = 64
DA_HEADS = 4
GQA_HEAD_DIM = 128
GQA_HEADS = 4
GQA_KV_HEADS = 2
LRU_WIDTH = 512
LRU_BLOCKS = 4
RGLRU_C = 8.0
SSM_D_INNER = 512
SSM_HEAD_DIM = 64
SSM_HEADS = 8
SSM_GROUPS = 2
SSM_D_STATE = 128
SSD_CHUNK = 128
D_FF = 5632
ROPE_THETA = 10000.0
NORM_EPS = 1e-6
LOG2E = math.log2(math.e)

LANES = 128
SUBLANES = 8
HALO = 16
IN_COLS = 5136
ATTN_COLS = 2560
QA_BLK, QB_BLK, KD_BLK, VD_BLK = 0, 4, 8, 12
QG_BLK, KG_BLK, VG_BLK = 20, 24, 26
QKV_COLS = 30 * LANES
VMEM_LIMIT = 56 * 1024 * 1024


def _cparams(*sem):
    return pltpu.CompilerParams(dimension_semantics=sem, vmem_limit_bytes=VMEM_LIMIT)


def _silu(x):
    return x * jax.nn.sigmoid(x)


def _norm_mod(x, nw, shift, scale):
    y = x * lax.rsqrt(jnp.mean(x * x, axis=-1, keepdims=True) + NORM_EPS) * nw
    return y * (1.0 + scale) + shift


def _mod_kernel(c_ref, w_ref, b_ref, o_ref):
    s = _silu(c_ref[...]).astype(BF16)
    o_ref[...] = jnp.dot(s, w_ref[...].astype(BF16), preferred_element_type=F32) + b_ref[...]


def modulation_all(cond8, w_mod, b_mod):
    depth, d, n = w_mod.shape
    tn = 1024
    return pl.pallas_call(
        _mod_kernel,
        grid=(depth, n // tn),
        in_specs=[
            pl.BlockSpec((8, d), lambda l, j: (0, 0)),
            pl.BlockSpec((None, d, tn), lambda l, j: (l, 0, j)),
            pl.BlockSpec((None, 1, tn), lambda l, j: (l, 0, j)),
        ],
        out_specs=pl.BlockSpec((None, 8, tn), lambda l, j: (l, 0, j)),
        out_shape=jax.ShapeDtypeStruct((depth, 8, n), F32),
        compiler_params=_cparams("parallel", "parallel"),
        name="modulation",
    )(cond8, w_mod, b_mod.reshape(depth, 1, n))


IN_MAIN_COLS = 5120
IN_TN = 1024


ADALN_ROWS = 16


def _adaln_kernel(x_ref, nw_ref, sh_ref, sc_ref, o_ref):
    gain = nw_ref[...] * (1.0 + sc_ref[...])
    shift = sh_ref[...]
    d = x_ref.shape[1]

    def body(r, _):
        rows = pl.ds(pl.multiple_of(r * ADALN_ROWS, ADALN_ROWS), ADALN_ROWS)
        x = x_ref[rows, :]
        inv = lax.rsqrt(jnp.sum(x * x, axis=-1, keepdims=True) * (1.0 / d) + NORM_EPS)
        o_ref[rows, :] = (x * inv * gain + shift).astype(BF16)
        return 0

    lax.fori_loop(0, x_ref.shape[0] // ADALN_ROWS, body, 0, unroll=4)


def adaln(x, nw, shift, scale):
    m, d = x.shape
    tm = min(m, 256)
    vec = pl.BlockSpec((1, d), lambda i: (0, 0))
    return pl.pallas_call(
        _adaln_kernel,
        grid=(m // tm,),
        in_specs=[pl.BlockSpec((tm, d), lambda i: (i, 0)), vec, vec, vec],
        out_specs=pl.BlockSpec((tm, d), lambda i: (i, 0)),
        out_shape=jax.ShapeDtypeStruct((m, d), BF16),
        compiler_params=_cparams("parallel"),
        name="adaln",
    )(x, nw, shift, scale)


_NT = (((1,), (1,)), ((), ()))


def _inproj_kernel(xn_ref, w_ref, o_ref, wb_ref):
    @pl.when(pl.program_id(1) == 0)
    def _():
        wb_ref[...] = w_ref[...].astype(BF16)

    o_ref[...] = lax.dot_general(xn_ref[...], wb_ref[...], _NT, preferred_element_type=F32)


def in_proj(xn, w_in_t, layer):
    m, d = xn.shape
    tm = min(m, 1024)
    return pl.pallas_call(
        _inproj_kernel,
        grid=(IN_MAIN_COLS // IN_TN, m // tm),
        in_specs=[pl.BlockSpec((tm, d), lambda j, i: (i, 0)),
                  pl.BlockSpec((None, IN_TN, d), lambda j, i: (layer, j, 0))],
        out_specs=pl.BlockSpec((tm, IN_TN), lambda j, i: (i, j)),
        out_shape=jax.ShapeDtypeStruct((m, IN_MAIN_COLS), F32),
        scratch_shapes=[pltpu.VMEM((IN_TN, d), BF16)],
        compiler_params=_cparams("parallel", "arbitrary"),
        name="in_proj",
    )(xn, w_in_t)


def _dtproj_kernel(xn_ref, w_ref, o_ref):
    o_ref[...] = lax.dot_general(xn_ref[...], w_ref[...].astype(BF16), _NT, preferred_element_type=F32)


def dt_proj(xn, w_in_t, layer):
    m, d = xn.shape
    tm = min(m, 1024)
    n_dt = IN_COLS - IN_MAIN_COLS
    return pl.pallas_call(
        _dtproj_kernel,
        grid=(m // tm,),
        in_specs=[pl.BlockSpec((tm, d), lambda i: (i, 0)),
                  pl.BlockSpec((None, n_dt, d), lambda i: (layer, IN_MAIN_COLS // n_dt, 0))],
        out_specs=pl.BlockSpec((tm, n_dt), lambda i: (i, 0)),
        out_shape=jax.ShapeDtypeStruct((m, n_dt), F32),
        compiler_params=_cparams("parallel"),
        name="dt_proj",
    )(xn, w_in_t)


def _resid_kernel(*refs, n_y):
    y_refs = refs[:n_y]
    w_ref, x_ref, g_ref, o_ref, wb_ref = refs[n_y:]

    @pl.when(pl.program_id(1) == 0)
    def _():
        wb_ref[...] = w_ref[...].astype(BF16)

    kk = wb_ref.shape[0] // n_y
    acc = functools.reduce(jnp.add, [
        jnp.dot(y_refs[a][...], wb_ref[a * kk:(a + 1) * kk, :], preferred_element_type=F32) for a in range(n_y)])
    o_ref[...] = x_ref[...] + g_ref[...] * acc


def resid_proj(ys, w_all, layer, x, gate):
    m = x.shape[0]
    _, k, n = w_all.shape
    kk = k // len(ys)
    tm = min(m, 512)
    tn = 1024 if k <= 2048 else 512
    return pl.pallas_call(
        functools.partial(_resid_kernel, n_y=len(ys)),
        grid=(n // tn, m // tm),
        in_specs=[pl.BlockSpec((tm, kk), lambda j, i: (i, 0)) for _ in ys]
                 + [pl.BlockSpec((None, k, tn), lambda j, i: (layer, 0, j)),
                    pl.BlockSpec((tm, tn), lambda j, i: (i, j)),
                    pl.BlockSpec((1, tn), lambda j, i: (0, j))],
        out_specs=pl.BlockSpec((tm, tn), lambda j, i: (i, j)),
        out_shape=jax.ShapeDtypeStruct((m, n), F32),
        scratch_shapes=[pltpu.VMEM((k, tn), BF16)],
        compiler_params=_cparams("parallel", "arbitrary"),
        name="resid_proj",
    )(*ys, w_all, x, gate)


def _ffn_up_kernel(xp_ref, x_ref, xx_ref, wg_ref, wu_ref, cw_ref, cb_ref, o_ref, xn_ref, g_ref, wb_ref,
                   *, tm, n_row_blocks):
    i = pl.program_id(1)

    @pl.when(i == 0)
    def _():
        wb_ref[0] = wg_ref[...].astype(BF16)
        wb_ref[1] = wu_ref[...].astype(BF16)

    xn_ref[pl.ds(0, HALO), :] = jnp.where(i == 0, jnp.zeros_like(xp_ref), xp_ref[...])
    xn_ref[pl.ds(HALO, tm), :] = x_ref[...]
    xn_ref[pl.ds(HALO + tm, HALO), :] = jnp.where(i == n_row_blocks - 1, jnp.zeros_like(xx_ref), xx_ref[...])
    g_ref[...] = jnp.dot(xn_ref[...], wb_ref[0], preferred_element_type=F32)
    up = jnp.dot(x_ref[...], wb_ref[1], preferred_element_type=F32)
    cw = cw_ref[...]
    gc = (cb_ref[...] + cw[0:1] * g_ref[pl.ds(HALO - 1, tm), :] + cw[1:2] * g_ref[pl.ds(HALO, tm), :]
          + cw[2:3] * g_ref[pl.ds(HALO + 1, tm), :])
    o_ref[...] = (_silu(gc) * up).astype(BF16)


def ffn_up(xn, w_up_all, layer, conv_w, conv_b):
    m, d = xn.shape
    dff = w_up_all.shape[2] // 2
    tm = min(m, 1024)
    tn = 512
    nrb = m // tm
    hb = tm // HALO
    nhb = m // HALO
    return pl.pallas_call(
        functools.partial(_ffn_up_kernel, tm=tm, n_row_blocks=nrb),
        grid=(dff // tn, nrb),
        in_specs=[pl.BlockSpec((HALO, d), lambda j, i: (jnp.maximum(i * hb - 1, 0), 0)),
                  pl.BlockSpec((tm, d), lambda j, i: (i, 0)),
                  pl.BlockSpec((HALO, d), lambda j, i: (jnp.minimum((i + 1) * hb, nhb - 1), 0)),
                  pl.BlockSpec((None, d, tn), lambda j, i: (layer, 0, j)),
                  pl.BlockSpec((None, d, tn), lambda j, i: (layer, 0, j + dff // tn)),
                  pl.BlockSpec((3, tn), lambda j, i: (0, j)),
                  pl.BlockSpec((1, tn), lambda j, i: (0, j))],
        out_specs=pl.BlockSpec((tm, tn), lambda j, i: (i, j)),
        out_shape=jax.ShapeDtypeStruct((m, dff), BF16),
        scratch_shapes=[pltpu.VMEM((tm + 2 * HALO, d), BF16), pltpu.VMEM((tm + 2 * HALO, tn), F32),
                        pltpu.VMEM((2, d, tn), BF16)],
        compiler_params=_cparams("parallel", "arbitrary"),
        name="ffn_up",
    )(xn, xn, xn, w_up_all, w_up_all, conv_w, conv_b.reshape(1, dff))


def _rmsnorm_kernel(x_ref, w_ref, o_ref):
    x = x_ref[...]
    o_ref[...] = x * lax.rsqrt(jnp.mean(x * x, axis=-1, keepdims=True) + NORM_EPS) * w_ref[...]


def final_rmsnorm(x, w):
    m, d = x.shape
    tm = min(m, 512)
    return pl.pallas_call(
        _rmsnorm_kernel,
        grid=(m // tm,),
        in_specs=[pl.BlockSpec((tm, d), lambda i: (i, 0)), pl.BlockSpec((1, d), lambda i: (0, 0))],
        out_specs=pl.BlockSpec((tm, d), lambda i: (i, 0)),
        out_shape=jax.ShapeDtypeStruct((m, d), F32),
        compiler_params=_cparams("parallel"),
        name="final_norm",
    )(x, w.reshape(1, d))


def _rope_tables(n_tok, head_dim):
    q = head_dim // 4
    lane = jnp.arange(LANES, dtype=jnp.int32)
    u = lane % head_dim
    region = u // q
    freqs = jnp.power(ROPE_THETA, -(u % q).astype(F32) / q)[None, :]
    ang_r = jnp.arange(n_tok // GRID_W, dtype=jnp.int32).astype(F32)[:, None] * freqs
    ang_c = jnp.arange(GRID_W, dtype=jnp.int32).astype(F32)[:, None] * freqs
    by_row = (region < 2)[None, None, :]

    def per_token(f):
        return jnp.where(by_row, f(ang_r)[:, None, :], f(ang_c)[None, :, :]).reshape(n_tok, LANES)

    cos, sin = per_token(jnp.cos), per_token(jnp.sin)
    first = (region % 2 == 0)[None, :]
    return cos, jnp.where(first, -sin, 0.0), jnp.where(first, 0.0, sin)


def _prep_kernel(p_ref, cd_ref, ad_ref, bd_ref, cg_ref, ag_ref, bg_ref, qn_ref, kn_ref, o_ref, *, rope):
    lane = lax.broadcasted_iota(jnp.int32, (1, LANES), 1)
    lo = lane < DA_HEAD_DIM

    def blk(b):
        return p_ref[:, b * LANES:(b + 1) * LANES]

    def put(b, v):
        o_ref[:, b * LANES:(b + 1) * LANES] = v.astype(BF16)

    def rot(x, cos, sa, sb, quarter):
        if not rope:
            return x
        return (x * cos[...] + pltpu.roll(x, LANES - quarter, 1) * sa[...]
                + pltpu.roll(x, quarter, 1) * sb[...])

    def rms(x, w):
        return x * lax.rsqrt(jnp.mean(x * x, axis=-1, keepdims=True) + NORM_EPS) * w[...]

    qd = DA_HEAD_DIM // 4
    qg = GQA_HEAD_DIM // 4
    ones = jnp.ones((p_ref.shape[0], LANES), F32)
    for h in range(DA_HEADS):
        q = rot(blk(h), cd_ref, ad_ref, bd_ref, qd) * (DA_HEAD_DIM ** -0.5 * LOG2E)
        put(QA_BLK + h, jnp.where(lo, q, 0.0))
        put(QB_BLK + h, jnp.where(lo, 0.0, q))
        put(KD_BLK + h, rot(blk(4 + h), cd_ref, ad_ref, bd_ref, qd))
        put(VD_BLK + 2 * h, blk(8 + h))
        put(VD_BLK + 2 * h + 1, ones)
    for h in range(GQA_HEADS):
        q = rot(rms(blk(12 + h), qn_ref), cg_ref, ag_ref, bg_ref, qg) * (GQA_HEAD_DIM ** -0.5 * LOG2E)
        put(QG_BLK + h, q)
    for h in range(GQA_KV_HEADS):
        put(KG_BLK + h, rot(rms(blk(16 + h), kn_ref), cg_ref, ag_ref, bg_ref, qg))
        put(VG_BLK + 2 * h, blk(18 + h))
        put(VG_BLK + 2 * h + 1, ones)


def attn_prep(p, tables, q_norm, k_norm, rope):
    m = p.shape[0]
    tm = min(m, 512)
    tab = pl.BlockSpec((tm, LANES), lambda i: (i, 0))
    vec = pl.BlockSpec((1, LANES), lambda i: (0, 0))
    return pl.pallas_call(
        functools.partial(_prep_kernel, rope=rope),
        grid=(m // tm,),
        in_specs=[pl.BlockSpec((tm, ATTN_COLS), lambda i: (i, 0))] + [tab] * 6 + [vec, vec],
        out_specs=pl.BlockSpec((tm, QKV_COLS), lambda i: (i, 0)),
        out_shape=jax.ShapeDtypeStruct((m, QKV_COLS), BF16),
        compiler_params=_cparams("parallel"),
        name="attn_prep",
    )(p, *tables, q_norm.reshape(1, LANES), k_norm.reshape(1, LANES))


def _attn_kernel(*refs, tk, n_kv, diff, lam_init):
    refs = list(refs)
    qa_ref, qb_ref, kc_ref, vc_ref = refs[:4]
    del refs[:4]
    if n_kv:
        kl_ref, vl_ref = refs[:2]
        del refs[:2]
    if diff:
        lam_ref, sw_ref = refs[:2]
        del refs[:2]
    o_ref = refs.pop(0)
    if n_kv:
        s_ref = refs.pop(0)
    sc_ref, m_ref, acc_ref = refs
    tq = qa_ref.shape[0]
    nt = (((1,), (1,)), ((), ()))
    q2 = jnp.concatenate([qa_ref[...], qb_ref[...]], axis=0)

    def lane_tiles(x):
        return [x[:, t * LANES:(t + 1) * LANES] for t in range(x.shape[1] // LANES)]

    def key_rows(ref, c):
        return ref[pl.ds(pl.multiple_of(c * tk, tk), tk), :]

    def chunks(body):
        if n_kv:
            lax.fori_loop(0, n_kv, lambda c, _: body(c) or 0, 0)

    sc = lax.dot_general(q2, kc_ref[...], nt, preferred_element_type=F32)
    sc_ref[...] = sc
    m_ref[...] = functools.reduce(jnp.maximum, lane_tiles(sc))

    def score_chunk(c):
        s = lax.dot_general(q2, key_rows(kl_ref, c), nt, preferred_element_type=F32)
        s_ref[c] = s
        m_ref[...] = functools.reduce(jnp.maximum, lane_tiles(s), m_ref[...])

    chunks(score_chunk)
    m_ref[...] = jnp.broadcast_to(jnp.max(m_ref[...], axis=-1, keepdims=True), (2 * tq, LANES))

    def weigh(s, vs):
        m = m_ref[...]
        p = jnp.concatenate([jnp.exp2(st - m).astype(BF16) for st in lane_tiles(s)], axis=1)
        return jnp.dot(p, vs, preferred_element_type=F32)

    acc_ref[...] = weigh(sc_ref[...], vc_ref[...])

    def weigh_chunk(c):
        acc_ref[...] += weigh(s_ref[c], key_rows(vl_ref, c))

    chunks(weigh_chunk)
    o = acc_ref[:, 0:LANES] / acc_ref[:, LANES:2 * LANES]
    oa = o[0:tq]
    ob = o[tq:2 * tq]
    if diff:
        lv = lam_ref[...]
        lam = (jnp.exp(jnp.sum(lv[0:1] * lv[1:2], keepdims=True))
               - jnp.exp(jnp.sum(lv[2:3] * lv[3:4], keepdims=True)) + lam_init)
        o = oa - lam * ob
        o = o * lax.rsqrt(jnp.mean(o * o, axis=-1, keepdims=True) + NORM_EPS) * sw_ref[...]
        o_ref[...] = (o * (1.0 - lam_init)).astype(BF16)
    else:
        o_ref[:, 0:LANES] = oa.astype(BF16)
        o_ref[:, LANES:2 * LANES] = ob.astype(BF16)


ATTN_TQ = 512
ATTN_TK = (4096, 2048, 1024, 512, 256, 128)


def _attention(qkv_q, qkv_c, qkv_l, extra, extra_specs, cols, n_heads, out_width, name, **static):
    qa_col, qb_col, k_col, v_col = cols
    sq, n_ctx = qkv_q.shape[0], qkv_c.shape[0]
    tq = min(sq, ATTN_TQ)
    n_lat = 0 if qkv_l is None else qkv_l.shape[0]
    tk = next((t for t in ATTN_TK if n_lat and n_lat % t == 0), 0)
    n_kv = n_lat // tk if n_lat else 0

    def kv_specs(rows):
        return [pl.BlockSpec((rows, LANES), lambda h, i: (0, k_col(h))),
                pl.BlockSpec((rows, 2 * LANES), lambda h, i: (0, v_col(h)))]

    in_specs = [pl.BlockSpec((tq, LANES), lambda h, i: (i, qa_col(h))),
                pl.BlockSpec((tq, LANES), lambda h, i: (i, qb_col(h)))] + kv_specs(n_ctx)
    args = [qkv_q, qkv_q, qkv_c, qkv_c]
    scratch = [pltpu.VMEM((2 * tq, n_ctx), F32), pltpu.VMEM((2 * tq, LANES), F32),
               pltpu.VMEM((2 * tq, 2 * LANES), F32)]
    if n_kv:
        in_specs += kv_specs(n_lat)
        args += [qkv_l, qkv_l]
        scratch = [pltpu.VMEM((n_kv, 2 * tq, tk), F32)] + scratch
    return pl.pallas_call(
        functools.partial(_attn_kernel, tk=tk, n_kv=n_kv, **static),
        grid=(n_heads, sq // tq),
        in_specs=in_specs + extra_specs,
        out_specs=pl.BlockSpec((tq, out_width), lambda h, i: (i, h)),
        out_shape=jax.ShapeDtypeStruct((sq, GROUP_WIDTH), BF16),
        scratch_shapes=scratch,
        compiler_params=_cparams("parallel", "parallel"),
        name=name,
    )(*args, *extra)


def diff_attention(qkv_q, qkv_c, qkv_l, da_lambda, subln_w, lam_init):
    cols = (lambda h: QA_BLK + h, lambda h: QB_BLK + h, lambda h: KD_BLK + h, lambda h: VD_BLK // 2 + h)
    extra_specs = [pl.BlockSpec((4, DA_HEAD_DIM), lambda h, i: (0, 0)), pl.BlockSpec((1, LANES), lambda h, i: (0, 0))]
    return _attention(qkv_q, qkv_c, qkv_l, [da_lambda, subln_w.reshape(1, LANES)], extra_specs, cols,
                      DA_HEADS, LANES, "diff_attn", diff=True, lam_init=lam_init)


def gqa_attention(qkv_q, qkv_c, qkv_l):
    cols = (lambda h: QG_BLK + 2 * h, lambda h: QG_BLK + 2 * h + 1, lambda h: KG_BLK + h,
            lambda h: VG_BLK // 2 + h)
    return _attention(qkv_q, qkv_c, qkv_l, [], [], cols, GQA_KV_HEADS, 2 * LANES, "gqa_attn",
                      diff=False, lam_init=0.0)


CONV_CHUNK = 256


def _conv_chunk(x_ref, base, n_rows, chunk, cw, cb):
    cur = x_ref[pl.ds(base, chunk), :]
    prev = x_ref[pl.ds(pl.multiple_of(jnp.maximum(base - 8, 0), 8), 8), :]
    nxt = x_ref[pl.ds(pl.multiple_of(jnp.minimum(base + chunk, n_rows - 8), 8), 8), :]
    prev = jnp.where(base == 0, 0.0, prev)
    nxt = jnp.where(base + chunk >= n_rows, 0.0, nxt)
    cat = jnp.concatenate([prev, cur, nxt], axis=0)
    n = chunk + 16
    xm2 = pltpu.roll(cat, 2, 0)[8:8 + chunk]
    xm1 = pltpu.roll(cat, 1, 0)[8:8 + chunk]
    xp1 = pltpu.roll(cat, n - 1, 0)[8:8 + chunk]
    return cb + cw[0:1] * xm2 + cw[1:2] * xm1 + cw[2:3] * cur + cw[3:4] * xp1


def _dwconv_kernel(x_ref, w_ref, b_ref, o_ref, *, n_rows, chunk, act):
    cw = w_ref[...]
    cb = b_ref[...]

    def body(c, _):
        base = pl.multiple_of(c * chunk, chunk)
        y = _conv_chunk(x_ref, base, n_rows, chunk, cw, cb)
        o_ref[pl.ds(base, chunk), :] = _silu(y) if act else y
        return 0

    lax.fori_loop(0, n_rows // chunk, body, 0)


def dwconv(p, col0, width, conv_w, conv_b, act):
    n_rows = p.shape[0]
    chunk = min(n_rows, CONV_CHUNK)
    b0 = col0 // LANES
    return pl.pallas_call(
        functools.partial(_dwconv_kernel, n_rows=n_rows, chunk=chunk, act=act),
        grid=(width // LANES,),
        in_specs=[pl.BlockSpec((n_rows, LANES), lambda j: (0, b0 + j)),
                  pl.BlockSpec((4, LANES), lambda j: (0, j)),
                  pl.BlockSpec((1, LANES), lambda j: (0, j))],
        out_specs=pl.BlockSpec((n_rows, LANES), lambda j: (0, j)),
        out_shape=jax.ShapeDtypeStruct((n_rows, width), F32),
        compiler_params=_cparams("parallel"),
        name="dwconv",
    )(p, conv_w, conv_b.reshape(1, width))


LRU_CHUNK = 256


def _lru_scan_chunk(a, u, h_in, reverse):
    n = a.shape[0]
    row = lax.broadcasted_iota(jnp.int32, (n, 1), 0)
    for k in (1, 2, 4):
        if reverse:
            keep = row < n - k
            a_s = jnp.where(keep, pltpu.roll(a, n - k, 0), 1.0)
            u_s = jnp.where(keep, pltpu.roll(u, n - k, 0), 0.0)
        else:
            keep = row >= k
            a_s = jnp.where(keep, pltpu.roll(a, k, 0), 1.0)
            u_s = jnp.where(keep, pltpu.roll(u, k, 0), 0.0)
        u = u + a * u_s
        a = a * a_s
    n_tiles = n // SUBLANES
    order = range(n_tiles - 1, -1, -1) if reverse else range(n_tiles)
    tiles = [None] * n_tiles
    h = h_in
    for i in order:
        sl = slice(i * SUBLANES, (i + 1) * SUBLANES)
        h = u[sl] + a[sl] * h
        tiles[i] = h
    return jnp.concatenate(tiles, axis=0)


def _lru_kernel(xc_ref, g_ref, w_ref, b_ref, lam_ref, h0_ref, y_ref, hT_ref, hf_ref, *, n_rows, chunk):
    n_chunks = n_rows // chunk

    def direction(d, reverse, finish):
        w_r = w_ref[d, 0].astype(BF16)
        w_i = w_ref[d, 1].astype(BF16)
        b_r = b_ref[d, 0:1, :]
        b_i = b_ref[d, 1:2, :]
        lam = lam_ref[d:d + 1, :]
        log_sig = jnp.minimum(lam, 0.0) - jnp.log1p(jnp.exp(-jnp.abs(lam)))

        def body(c, h):
            cc = (n_chunks - 1 - c) if reverse else c
            base = pl.multiple_of(cc * chunk, chunk)
            x = xc_ref[pl.ds(base, chunk), :]
            xb = x.astype(BF16)
            r = jax.nn.sigmoid(jnp.dot(xb, w_r, preferred_element_type=F32) + b_r)
            i = jax.nn.sigmoid(jnp.dot(xb, w_i, preferred_element_type=F32) + b_i)
            log_a = RGLRU_C * r * log_sig
            a = jnp.exp(log_a)
            u = jnp.sqrt(1.0 - a * a) * (i * x)
            hs = _lru_scan_chunk(a, u, h, reverse)
            finish(base, hs)
            return hs[0:1] if reverse else hs[chunk - 1:chunk]

        return lax.fori_loop(0, n_chunks, body, h0_ref[d:d + 1, :])

    def keep_fwd(base, hs):
        hf_ref[pl.ds(base, chunk), :] = hs

    def emit(base, hs):
        g = g_ref[pl.ds(base, chunk), :]
        y_ref[pl.ds(base, chunk), :] = ((hf_ref[pl.ds(base, chunk), :] + hs) * jax.nn.gelu(g)).astype(BF16)

    hT_ref[0:1, :] = direction(0, False, keep_fwd)
    hT_ref[1:2, :] = direction(1, True, emit)


def rglru(xc, p, g_col0, w_gates, b_gates, lam, h0):
    n_rows = xc.shape[0]
    chunk = min(n_rows, LRU_CHUNK)
    gb = g_col0 // LANES
    return pl.pallas_call(
        functools.partial(_lru_kernel, n_rows=n_rows, chunk=chunk),
        grid=(LRU_BLOCKS,),
        in_specs=[pl.BlockSpec((n_rows, LANES), lambda j: (0, j)),
                  pl.BlockSpec((n_rows, LANES), lambda j: (0, gb + j)),
                  pl.BlockSpec((2, 2, None, LANES, LANES), lambda j: (0, 0, j, 0, 0)),
                  pl.BlockSpec((2, 2, LANES), lambda j: (0, 0, j)),
                  pl.BlockSpec((2, LANES), lambda j: (0, j)),
                  pl.BlockSpec((2, LANES), lambda j: (0, j))],
        out_specs=[pl.BlockSpec((n_rows, LANES), lambda j: (0, j)),
                   pl.BlockSpec((2, LANES), lambda j: (0, j))],
        out_shape=[jax.ShapeDtypeStruct((n_rows, LRU_WIDTH), BF16),
                   jax.ShapeDtypeStruct((2, LRU_WIDTH), F32)],
        scratch_shapes=[pltpu.VMEM((n_rows, LANES), F32)],
        compiler_params=_cparams("parallel"),
        name="rglru",
    )(xc, p, w_gates, b_gates, lam, h0)


def _ssd_direction(xs, bm, cm, dt_raw, dtT_raw, bias_row, bias_col, a_row, a_col, st_ref, reverse):
    q = SSD_CHUNK
    hi = lax.Precision.HIGHEST
    li = lax.broadcasted_iota(jnp.int32, (q, q), 0)
    si = lax.broadcasted_iota(jnp.int32, (q, q), 1)
    causal = (si >= li) if reverse else (si <= li)
    tri = causal.astype(F32)
    dt = jax.nn.softplus(dt_raw + bias_row)
    dtT = jax.nn.softplus(dtT_raw + bias_col)
    adt = dt * a_row
    adtT = dtT * a_col
    cum = jnp.dot(tri, adt, preferred_element_type=F32, precision=hi)
    cumT = lax.dot_general(adtT, tri, (((1,), (1,)), ((), ())), preferred_element_type=F32,
                           precision=hi)
    edge = cum[0:1] if reverse else cum[q - 1:q]

    def ex(v):
        r = v.shape[0]
        first = lax.broadcasted_iota(jnp.int32, (r, LANES), 1) < SSM_HEAD_DIM
        return jnp.concatenate(
            [jnp.where(first, jnp.broadcast_to(v[:, 2 * j:2 * j + 1], (r, LANES)),
                       jnp.broadcast_to(v[:, 2 * j + 1:2 * j + 2], (r, LANES)))
             for j in range(SSM_HEADS // 2)], axis=1)

    cum_e = ex(cum)
    edge_e = ex(edge)
    xdt = xs * ex(dt)
    xdd = (xdt * jnp.exp(edge_e - cum_e)).astype(BF16)
    xdt_b = xdt.astype(BF16)
    grow = jnp.exp(cum_e)
    ys = []
    for g in range(SSM_GROUPS):
        gsl = slice(g * 256, (g + 1) * 256)
        b_g = bm[:, g * SSM_D_STATE:(g + 1) * SSM_D_STATE]
        c_g = cm[:, g * SSM_D_STATE:(g + 1) * SSM_D_STATE].astype(BF16)
        b_gt = b_g.T.astype(BF16)
        gram = jnp.dot(c_g, b_gt, preferred_element_type=F32)
        st = st_ref[:, gsl]
        y_off = grow[:, gsl] * jnp.dot(c_g, st.astype(BF16), preferred_element_type=F32)
        st_ref[:, gsl] = jnp.exp(edge_e[:, gsl]) * st + jnp.dot(b_gt, xdd[:, gsl], preferred_element_type=F32)
        for hh in range(SSM_HEADS // SSM_GROUPS):
            h = g * (SSM_HEADS // SSM_GROUPS) + hh
            seg = cum[:, h:h + 1] - cumT[h:h + 1, :]
            decay = jnp.exp(jnp.where(causal, seg, -jnp.inf))
            m = (gram * decay).astype(BF16)
            hs = slice(h * SSM_HEAD_DIM, (h + 1) * SSM_HEAD_DIM)
            y_d = jnp.dot(m, xdt_b[:, hs], preferred_element_type=F32)
            ys.append(y_d + y_off[:, hh * SSM_HEAD_DIM:(hh + 1) * SSM_HEAD_DIM])
    return jnp.concatenate(ys, axis=1)


def _ssd_kernel(xf_ref, bf_ref, cf_ref, dtf_ref, dtTf_ref, xb_ref, bb_ref, cb_ref, dtb_ref, dtTb_ref,
                bias_ref, biasT_ref, alog_ref, alogT_ref, st0_ref, yf_ref, yb_ref, stT_ref, st_ref):
    c = pl.program_id(0)

    @pl.when(c == 0)
    def _():
        st_ref[...] = st0_ref[...]

    a_row = -jnp.exp(alog_ref[...])
    a_col = -jnp.exp(alogT_ref[...])
    yf_ref[...] = _ssd_direction(xf_ref[...], bf_ref[...], cf_ref[...], dtf_ref[:, 0:SSM_HEADS],
                                 dtTf_ref[0:SSM_HEADS, :], bias_ref[0:1, :], biasT_ref[:, 0:1],
                                 a_row[0:1, :], a_col[:, 0:1], st_ref.at[0], False)
    yb_ref[...] = _ssd_direction(xb_ref[...], bb_ref[...], cb_ref[...], dtb_ref[:, SSM_HEADS:2 * SSM_HEADS],
                                 dtTb_ref[SSM_HEADS:2 * SSM_HEADS, :], bias_ref[1:2, :], biasT_ref[:, 1:2],
                                 a_row[1:2, :], a_col[:, 1:2], st_ref.at[1], True)

    @pl.when(c == pl.num_programs(0) - 1)
    def _():
        stT_ref[...] = st_ref[...]


def ssd_scan(xbc, dt, dtT, dt_bias, a_log, st0):
    n_rows = xbc.shape[0]
    q = SSD_CHUNK
    nc = n_rows // q
    xb = SSM_D_INNER // q
    fwd = lambda c: c
    bwd = lambda c: nc - 1 - c

    def specs(ix):
        return [pl.BlockSpec((q, SSM_D_INNER), lambda c: (ix(c), 0)),
                pl.BlockSpec((q, 2 * SSM_D_STATE), lambda c: (ix(c), 2)),
                pl.BlockSpec((q, 2 * SSM_D_STATE), lambda c: (ix(c), 3)),
                pl.BlockSpec((q, 2 * SSM_HEADS), lambda c: (ix(c), 0)),
                pl.BlockSpec((2 * SSM_HEADS, q), lambda c: (0, ix(c)))]

    small = lambda shape: pl.BlockSpec(shape, lambda c: (0,) * len(shape))
    st_shape = (2, SSM_D_STATE, SSM_D_INNER)
    return pl.pallas_call(
        _ssd_kernel,
        grid=(nc,),
        in_specs=specs(fwd) + specs(bwd) + [small((2, SSM_HEADS)), small((SSM_HEADS, 2)),
                                            small((2, SSM_HEADS)), small((SSM_HEADS, 2)), small(st_shape)],
        out_specs=[pl.BlockSpec((q, SSM_D_INNER), lambda c: (c, 0)),
                   pl.BlockSpec((q, SSM_D_INNER), lambda c: (nc - 1 - c, 0)),
                   small(st_shape)],
        out_shape=[jax.ShapeDtypeStruct((n_rows, SSM_D_INNER), F32),
                   jax.ShapeDtypeStruct((n_rows, SSM_D_INNER), F32),
                   jax.ShapeDtypeStruct(st_shape, F32)],
        scratch_shapes=[pltpu.VMEM(st_shape, F32)],
        compiler_params=_cparams("arbitrary"),
        name="ssd_scan",
    )(xbc, xbc, xbc, dt, dtT, xbc, xbc, xbc, dt, dtT, dt_bias, dt_bias.T, a_log, a_log.T, st0)


def _ssd_finish_kernel(yf_ref, yb_ref, xs_ref, z_ref, d_ref, nw_ref, o_ref):
    hrow = lax.broadcasted_iota(jnp.int32, (SSM_HEADS, SSM_D_INNER), 0)
    hcol = lax.broadcasted_iota(jnp.int32, (SSM_HEADS, SSM_D_INNER), 1) // SSM_HEAD_DIM
    d_e = jnp.sum(jnp.where(hrow == hcol, d_ref[...], 0.0), axis=0, keepdims=True)
    y = yf_ref[...] + yb_ref[...] + d_e * xs_ref[...]
    y = y * _silu(z_ref[...])
    o_ref[...] = (y * lax.rsqrt(jnp.mean(y * y, axis=-1, keepdims=True) + NORM_EPS) * nw_ref[...]).astype(BF16)


def ssd_finish(yf, yb, xbc, p, z_col0, d_skip, norm_w):
    n_rows = yf.shape[0]
    tm = min(n_rows, 512)
    zb = z_col0 // SSM_D_INNER
    blk = lambda cb: pl.BlockSpec((tm, SSM_D_INNER), lambda i: (i, cb))
    return pl.pallas_call(
        _ssd_finish_kernel,
        grid=(n_rows // tm,),
        in_specs=[blk(0), blk(0), blk(0), blk(zb),
                  pl.BlockSpec((SSM_HEADS, 1), lambda i: (0, 0)),
                  pl.BlockSpec((1, SSM_D_INNER), lambda i: (0, 0))],
        out_specs=blk(0),
        out_shape=jax.ShapeDtypeStruct((n_rows, SSM_D_INNER), BF16),
        compiler_params=_cparams("parallel"),
        name="ssd_finish",
    )(yf, yb, xbc, p, d_skip.reshape(SSM_HEADS, 1), norm_w.reshape(1, SSM_D_INNER))


LRU_X_COL = 2560
LRU_G_COL = 3072
SSM_Z_COL = 3584
SSM_XBC_COL = 4096


def _mixers(p, dt, qkv_q, qkv_c, qkv_l, lw, lam_init, lru_h0, ssd_st0):
    ya = diff_attention(qkv_q, qkv_c, qkv_l, lw["da_lambda"], lw["da_subln"], lam_init)
    yb = gqa_attention(qkv_q, qkv_c, qkv_l)
    xc = dwconv(p, LRU_X_COL, LRU_WIDTH, lw["lru_conv_w"], lw["lru_conv_b"], act=False)
    yc, lru_hT = rglru(xc, p, LRU_G_COL, lw["lru_w_gates"], lw["lru_b_gates"], lw["lru_lambda"], lru_h0)
    xbc = dwconv(p, SSM_XBC_COL, 2 * SSM_D_INNER, lw["ssm_conv_w"], lw["ssm_conv_b"], act=True)
    yf, ybk, ssd_stT = ssd_scan(xbc, dt, dt.T, lw["ssm_dt_bias"], lw["ssm_a_log"], ssd_st0)
    yd = ssd_finish(yf, ybk, xbc, p, SSM_Z_COL, lw["ssm_d"], lw["ssm_norm"])
    return [ya, yb, yc, yd], lru_hT, ssd_stT


def kernel(x, c, ctx, c_ctx, w_mod, b_mod, mix_norm, ffn_norm, w_in, w_out, da_lambda, da_subln, gqa_q_norm,
           gqa_k_norm, lru_conv_w, lru_conv_b, lru_w_gates, lru_b_gates, lru_lambda, ssm_conv_w, ssm_conv_b,
           ssm_dt_bias, ssm_a_log, ssm_d, ssm_norm, ffn_w_up, ffn_conv_w, ffn_conv_b, ffn_w_down, final_norm):
    depth = w_mod.shape[0]
    d = x.shape[-1]
    xl = x[0]
    xc = ctx[0]
    n_lat, n_ctx = xl.shape[0], xc.shape[0]

    cond8 = jnp.zeros((8, d), F32).at[0].set(c[0]).at[1].set(c_ctx)
    mods = modulation_all(cond8, w_mod, b_mod)
    tabs_l = _rope_tables(n_lat, DA_HEAD_DIM) + _rope_tables(n_lat, GQA_HEAD_DIM)
    tabs_c = _rope_tables(n_ctx, DA_HEAD_DIM) + _rope_tables(n_ctx, GQA_HEAD_DIM)
    lru_zero = jnp.zeros((2, LRU_WIDTH), F32)
    ssd_zero = jnp.zeros((2, SSM_D_STATE, SSM_D_INNER), F32)
    w_in_t = jnp.swapaxes(w_in, 1, 2)

    for layer in range(depth):
        need_ctx = layer < depth - 1
        lam_init = 0.8 - 0.6 * math.exp(-0.3 * layer)
        ml = [mods[layer, 0:1, k * d:(k + 1) * d] for k in range(6)]
        mc = [mods[layer, 1:2, k * d:(k + 1) * d] for k in range(6)]
        lw = dict(da_lambda=da_lambda[layer], da_subln=da_subln[layer],
                  lru_conv_w=lru_conv_w[layer], lru_conv_b=lru_conv_b[layer], lru_w_gates=lru_w_gates[layer],
                  lru_b_gates=lru_b_gates[layer], lru_lambda=lru_lambda[layer], ssm_conv_w=ssm_conv_w[layer],
                  ssm_conv_b=ssm_conv_b[layer], ssm_dt_bias=ssm_dt_bias[layer], ssm_a_log=ssm_a_log[layer],
                  ssm_d=ssm_d[layer], ssm_norm=ssm_norm[layer])
        nw_m = mix_norm[layer].reshape(1, d)
        nw_f = ffn_norm[layer].reshape(1, d)

        xn_c = adaln(xc, nw_m, mc[0], mc[1])
        xn_l = adaln(xl, nw_m, ml[0], ml[1])
        p_c, dt_c = in_proj(xn_c, w_in_t, layer), dt_proj(xn_c, w_in_t, layer)
        p_l, dt_l = in_proj(xn_l, w_in_t, layer), dt_proj(xn_l, w_in_t, layer)
        qkv_c = attn_prep(p_c, tabs_c, gqa_q_norm[layer], gqa_k_norm[layer], rope=False)
        qkv_l = attn_prep(p_l, tabs_l, gqa_q_norm[layer], gqa_k_norm[layer], rope=True)

        y_c, lru_h, ssd_st = _mixers(p_c, dt_c, qkv_c, qkv_c, None, lw, lam_init, lru_zero, ssd_zero)
        y_l, _, _ = _mixers(p_l, dt_l, qkv_l, qkv_c, qkv_l, lw, lam_init, lru_h, ssd_st)

        xl = resid_proj(y_l, w_out, layer, xl, ml[2])
        act = ffn_up(adaln(xl, nw_f, ml[3], ml[4]), ffn_w_up, layer, ffn_conv_w[layer], ffn_conv_b[layer])
        xl = resid_proj([act], ffn_w_down, layer, xl, ml[5])
        if need_ctx:
            xc = resid_proj(y_c, w_out, layer, xc, mc[2])
            act = ffn_up(adaln(xc, nw_f, mc[3], mc[4]), ffn_w_up, layer, ffn_conv_w[layer], ffn_conv_b[layer])
            xc = resid_proj([act], ffn_w_down, layer, xc, mc[5])

    return final_rmsnorm(xl, final_norm)[None]
```

```python
import functools
import math

import jax
import jax.numpy as jnp
from jax import lax
from jax.experimental import pallas as pl
from jax.experimental.pallas import tpu as pltpu

F32 = jnp.float32
BF16 = jnp.bfloat16

D_MODEL = 2048
DEPTH = 4
GRID_W = 64
GROUP_WIDTH = 512
DA_HEAD_DIM = 64
DA_HEADS = 4
GQA_HEAD_DIM = 128
GQA_HEADS = 4
GQA_KV_HEADS = 2
LRU_WIDTH = 512
LRU_BLOCKS = 4
RGLRU_C = 8.0
SSM_D_INNER = 512
SSM_HEAD_DIM = 64
SSM_HEADS = 8
SSM_GROUPS = 2
SSM_D_STATE = 128
SSD_CHUNK = 128
D_FF = 5632
ROPE_THETA = 10000.0
NORM_EPS = 1e-6
LOG2E = math.log2(math.e)

LANES = 128
SUBLANES = 8
HALO = 16
IN_COLS = 5136
ATTN_COLS = 2560
QA_BLK, QB_BLK, KD_BLK, VD_BLK = 0, 4, 8, 12
QG_BLK, KG_BLK, VG_BLK = 20, 24, 26
QKV_COLS = 30 * LANES
VMEM_LIMIT = 56 * 1024 * 1024


def _cparams(*sem):
    return pltpu.CompilerParams(dimension_semantics=sem, vmem_limit_bytes=VMEM_LIMIT)


def _silu(x):
    return x * jax.nn.sigmoid(x)


def _norm_mod(x, nw, shift, scale):
    y = x * lax.rsqrt(jnp.mean(x * x, axis=-1, keepdims=True) + NORM_EPS) * nw
    return y * (1.0 + scale) + shift


def _mod_kernel(c_ref, w_ref, b_ref, o_ref):
    s = _silu(c_ref[...]).astype(BF16)
    o_ref[...] = jnp.dot(s, w_ref[...].astype(BF16), preferred_element_type=F32) + b_ref[...]


def modulation_all(cond8, w_mod, b_mod):
    depth, d, n = w_mod.shape
    tn = 1024
    return pl.pallas_call(
        _mod_kernel,
        grid=(depth, n // tn),
        in_specs=[
            pl.BlockSpec((8, d), lambda l, j: (0, 0)),
            pl.BlockSpec((None, d, tn), lambda l, j: (l, 0, j)),
            pl.BlockSpec((None, 1, tn), lambda l, j: (l, 0, j)),
        ],
        out_specs=pl.BlockSpec((None, 8, tn), lambda l, j: (l, 0, j)),
        out_shape=jax.ShapeDtypeStruct((depth, 8, n), F32),
        compiler_params=_cparams("parallel", "parallel"),
        name="modulation",
    )(cond8, w_mod, b_mod.reshape(depth, 1, n))


IN_MAIN_COLS = 5120
IN_TN = 1024


ADALN_ROWS = 16


def _adaln_kernel(x_ref, nw_ref, sh_ref, sc_ref, o_ref):
    gain = nw_ref[...] * (1.0 + sc_ref[...])
    shift = sh_ref[...]
    d = x_ref.shape[1]

    def body(r, _):
        rows = pl.ds(pl.multiple_of(r * ADALN_ROWS, ADALN_ROWS), ADALN_ROWS)
        x = x_ref[rows, :]
        inv = lax.rsqrt(jnp.sum(x * x, axis=-1, keepdims=True) * (1.0 / d) + NORM_EPS)
        o_ref[rows, :] = (x * inv * gain + shift).astype(BF16)
        return 0

    lax.fori_loop(0, x_ref.shape[0] // ADALN_ROWS, body, 0, unroll=4)


def adaln(x, nw, shift, scale):
    m, d = x.shape
    tm = min(m, 1024)
    vec = pl.BlockSpec((1, d), lambda i: (0, 0))
    return pl.pallas_call(
        _adaln_kernel,
        grid=(m // tm,),
        in_specs=[pl.BlockSpec((tm, d), lambda i: (i, 0)), vec, vec, vec],
        out_specs=pl.BlockSpec((tm, d), lambda i: (i, 0)),
        out_shape=jax.ShapeDtypeStruct((m, d), BF16),
        compiler_params=_cparams("parallel"),
        name="adaln",
    )(x, nw, shift, scale)


_NT = (((1,), (1,)), ((), ()))


def _inproj_kernel(xn_ref, w_ref, o_ref, wb_ref):
    @pl.when(pl.program_id(1) == 0)
    def _():
        wb_ref[...] = w_ref[...].astype(BF16)

    o_ref[...] = lax.dot_general(xn_ref[...], wb_ref[...], _NT, preferred_element_type=F32)


def in_proj(xn, w_in_t, layer):
    m, d = xn.shape
    tm = min(m, 1024)
    return pl.pallas_call(
        _inproj_kernel,
        grid=(IN_MAIN_COLS // IN_TN, m // tm),
        in_specs=[pl.BlockSpec((tm, d), lambda j, i: (i, 0)),
                  pl.BlockSpec((None, IN_TN, d), lambda j, i: (layer, j, 0))],
        out_specs=pl.BlockSpec((tm, IN_TN), lambda j, i: (i, j)),
        out_shape=jax.ShapeDtypeStruct((m, IN_MAIN_COLS), F32),
        scratch_shapes=[pltpu.VMEM((IN_TN, d), BF16)],
        compiler_params=_cparams("parallel", "arbitrary"),
        name="in_proj",
    )(xn, w_in_t)


def _dtproj_kernel(xn_ref, w_ref, o_ref):
    o_ref[...] = lax.dot_general(xn_ref[...], w_ref[...].astype(BF16), _NT, preferred_element_type=F32)


def dt_proj(xn, w_in_t, layer):
    m, d = xn.shape
    tm = min(m, 1024)
    n_dt = IN_COLS - IN_MAIN_COLS
    return pl.pallas_call(
        _dtproj_kernel,
        grid=(m // tm,),
        in_specs=[pl.BlockSpec((tm, d), lambda i: (i, 0)),
                  pl.BlockSpec((None, n_dt, d), lambda i: (layer, IN_MAIN_COLS // n_dt, 0))],
        out_specs=pl.BlockSpec((tm, n_dt), lambda i: (i, 0)),
        out_shape=jax.ShapeDtypeStruct((m, n_dt), F32),
        compiler_params=_cparams("parallel"),
        name="dt_proj",
    )(xn, w_in_t)


def _resid_kernel(*refs, n_y):
    y_refs = refs[:n_y]
    w_ref, x_ref, g_ref, o_ref, wb_ref = refs[n_y:]

    @pl.when(pl.program_id(1) == 0)
    def _():
        wb_ref[...] = w_ref[...].astype(BF16)

    kk = wb_ref.shape[0] // n_y
    acc = functools.reduce(jnp.add, [
        jnp.dot(y_refs[a][...], wb_ref[a * kk:(a + 1) * kk, :], preferred_element_type=F32) for a in range(n_y)])
    o_ref[...] = x_ref[...] + g_ref[...] * acc


def resid_proj(ys, w_all, layer, x, gate):
    m = x.shape[0]
    _, k, n = w_all.shape
    kk = k // len(ys)
    tm = min(m, 512)
    tn = 1024 if k <= 2048 else 512
    return pl.pallas_call(
        functools.partial(_resid_kernel, n_y=len(ys)),
        grid=(n // tn, m // tm),
        in_specs=[pl.BlockSpec((tm, kk), lambda j, i: (i, 0)) for _ in ys]
                 + [pl.BlockSpec((None, k, tn), lambda j, i: (layer, 0, j)),
                    pl.BlockSpec((tm, tn), lambda j, i: (i, j)),
                    pl.BlockSpec((1, tn), lambda j, i: (0, j))],
        out_specs=pl.BlockSpec((tm, tn), lambda j, i: (i, j)),
        out_shape=jax.ShapeDtypeStruct((m, n), F32),
        scratch_shapes=[pltpu.VMEM((k, tn), BF16)],
        compiler_params=_cparams("parallel", "arbitrary"),
        name="resid_proj",
    )(*ys, w_all, x, gate)


def _ffn_up_kernel(xp_ref, x_ref, xx_ref, wg_ref, wu_ref, cw_ref, cb_ref, o_ref, xn_ref, g_ref, wb_ref,
                   *, tm, n_row_blocks):
    i = pl.program_id(1)

    @pl.when(i == 0)
    def _():
        wb_ref[0] = wg_ref[...].astype(BF16)
        wb_ref[1] = wu_ref[...].astype(BF16)

    xn_ref[pl.ds(0, HALO), :] = jnp.where(i == 0, jnp.zeros_like(xp_ref), xp_ref[...])
    xn_ref[pl.ds(HALO, tm), :] = x_ref[...]
    xn_ref[pl.ds(HALO + tm, HALO), :] = jnp.where(i == n_row_blocks - 1, jnp.zeros_like(xx_ref), xx_ref[...])
    g_ref[...] = jnp.dot(xn_ref[...], wb_ref[0], preferred_element_type=F32)
    up = jnp.dot(x_ref[...], wb_ref[1], preferred_element_type=F32)
    cw = cw_ref[...]
    gc = (cb_ref[...] + cw[0:1] * g_ref[pl.ds(HALO - 1, tm), :] + cw[1:2] * g_ref[pl.ds(HALO, tm), :]
          + cw[2:3] * g_ref[pl.ds(HALO + 1, tm), :])
    o_ref[...] = (_silu(gc) * up).astype(BF16)


def ffn_up(xn, w_up_all, layer, conv_w, conv_b):
    m, d = xn.shape
    dff = w_up_all.shape[2] // 2
    tm = min(m, 1024)
    tn = 512
    nrb = m // tm
    hb = tm // HALO
    nhb = m // HALO
    return pl.pallas_call(
        functools.partial(_ffn_up_kernel, tm=tm, n_row_blocks=nrb),
        grid=(dff // tn, nrb),
        in_specs=[pl.BlockSpec((HALO, d), lambda j, i: (jnp.maximum(i * hb - 1, 0), 0)),
                  pl.BlockSpec((tm, d), lambda j, i: (i, 0)),
                  pl.BlockSpec((HALO, d), lambda j, i: (jnp.minimum((i + 1) * hb, nhb - 1), 0)),
                  pl.BlockSpec((None, d, tn), lambda j, i: (layer, 0, j)),
                  pl.BlockSpec((None, d, tn), lambda j, i: (layer, 0, j + dff // tn)),
                  pl.BlockSpec((3, tn), lambda j, i: (0, j)),
                  pl.BlockSpec((1, tn), lambda j, i: (0, j))],
        out_specs=pl.BlockSpec((tm, tn), lambda j, i: (i, j)),
        out_shape=jax.ShapeDtypeStruct((m, dff), BF16),
        scratch_shapes=[pltpu.VMEM((tm + 2 * HALO, d), BF16), pltpu.VMEM((tm + 2 * HALO, tn), F32),
                        pltpu.VMEM((2, d, tn), BF16)],
        compiler_params=_cparams("parallel", "arbitrary"),
        name="ffn_up",
    )(xn, xn, xn, w_up_all, w_up_all, conv_w, conv_b.reshape(1, dff))


def _rmsnorm_kernel(x_ref, w_ref, o_ref):
    x = x_ref[...]
    o_ref[...] = x * lax.rsqrt(jnp.mean(x * x, axis=-1, keepdims=True) + NORM_EPS) * w_ref[...]


def final_rmsnorm(x, w):
    m, d = x.shape
    tm = min(m, 512)
    return pl.pallas_call(
        _rmsnorm_kernel,
        grid=(m // tm,),
        in_specs=[pl.BlockSpec((tm, d), lambda i: (i, 0)), pl.BlockSpec((1, d), lambda i: (0, 0))],
        out_specs=pl.BlockSpec((tm, d), lambda i: (i, 0)),
        out_shape=jax.ShapeDtypeStruct((m, d), F32),
        compiler_params=_cparams("parallel"),
        name="final_norm",
    )(x, w.reshape(1, d))


def _rope_tables(n_tok, head_dim):
    q = head_dim // 4
    lane = jnp.arange(LANES, dtype=jnp.int32)
    u = lane % head_dim
    region = u // q
    freqs = jnp.power(ROPE_THETA, -(u % q).astype(F32) / q)[None, :]
    ang_r = jnp.arange(n_tok // GRID_W, dtype=jnp.int32).astype(F32)[:, None] * freqs
    ang_c = jnp.arange(GRID_W, dtype=jnp.int32).astype(F32)[:, None] * freqs
    by_row = (region < 2)[None, None, :]

    def per_token(f):
        return jnp.where(by_row, f(ang_r)[:, None, :], f(ang_c)[None, :, :]).reshape(n_tok, LANES)

    cos, sin = per_token(jnp.cos), per_token(jnp.sin)
    first = (region % 2 == 0)[None, :]
    return cos, jnp.where(first, -sin, 0.0), jnp.where(first, 0.0, sin)


def _prep_kernel(p_ref, cd_ref, ad_ref, bd_ref, cg_ref, ag_ref, bg_ref, qn_ref, kn_ref, o_ref, *, rope):
    lane = lax.broadcasted_iota(jnp.int32, (1, LANES), 1)
    lo = lane < DA_HEAD_DIM

    def blk(b):
        return p_ref[:, b * LANES:(b + 1) * LANES]

    def put(b, v):
        o_ref[:, b * LANES:(b + 1) * LANES] = v.astype(BF16)

    def rot(x, cos, sa, sb, quarter):
        if not rope:
            return x
        return (x * cos[...] + pltpu.roll(x, LANES - quarter, 1) * sa[...]
                + pltpu.roll(x, quarter, 1) * sb[...])

    def rms(x, w):
        return x * lax.rsqrt(jnp.mean(x * x, axis=-1, keepdims=True) + NORM_EPS) * w[...]

    qd = DA_HEAD_DIM // 4
    qg = GQA_HEAD_DIM // 4
    ones = jnp.ones((p_ref.shape[0], LANES), F32)
    for h in range(DA_HEADS):
        q = rot(blk(h), cd_ref, ad_ref, bd_ref, qd) * (DA_HEAD_DIM ** -0.5 * LOG2E)
        put(QA_BLK + h, jnp.where(lo, q, 0.0))
        put(QB_BLK + h, jnp.where(lo, 0.0, q))
        put(KD_BLK + h, rot(blk(4 + h), cd_ref, ad_ref, bd_ref, qd))
        put(VD_BLK + 2 * h, blk(8 + h))
        put(VD_BLK + 2 * h + 1, ones)
    for h in range(GQA_HEADS):
        q = rot(rms(blk(12 + h), qn_ref), cg_ref, ag_ref, bg_ref, qg) * (GQA_HEAD_DIM ** -0.5 * LOG2E)
        put(QG_BLK + h, q)
    for h in range(GQA_KV_HEADS):
        put(KG_BLK + h, rot(rms(blk(16 + h), kn_ref), cg_ref, ag_ref, bg_ref, qg))
        put(VG_BLK + 2 * h, blk(18 + h))
        put(VG_BLK + 2 * h + 1, ones)


def attn_prep(p, tables, q_norm, k_norm, rope):
    m = p.shape[0]
    tm = min(m, 512)
    tab = pl.BlockSpec((tm, LANES), lambda i: (i, 0))
    vec = pl.BlockSpec((1, LANES), lambda i: (0, 0))
    return pl.pallas_call(
        functools.partial(_prep_kernel, rope=rope),
        grid=(m // tm,),
        in_specs=[pl.BlockSpec((tm, ATTN_COLS), lambda i: (i, 0))] + [tab] * 6 + [vec, vec],
        out_specs=pl.BlockSpec((tm, QKV_COLS), lambda i: (i, 0)),
        out_shape=jax.ShapeDtypeStruct((m, QKV_COLS), BF16),
        compiler_params=_cparams("parallel"),
        name="attn_prep",
    )(p, *tables, q_norm.reshape(1, LANES), k_norm.reshape(1, LANES))


def _attn_kernel(*refs, tk, n_kv, diff, lam_init):
    refs = list(refs)
    qa_ref, qb_ref, kc_ref, vc_ref = refs[:4]
    del refs[:4]
    if n_kv:
        kl_ref, vl_ref = refs[:2]
        del refs[:2]
    if diff:
        lam_ref, sw_ref = refs[:2]
        del refs[:2]
    o_ref = refs.pop(0)
    if n_kv:
        s_ref = refs.pop(0)
    sc_ref, m_ref, acc_ref = refs
    tq = qa_ref.shape[0]
    nt = (((1,), (1,)), ((), ()))
    q2 = jnp.concatenate([qa_ref[...], qb_ref[...]], axis=0)

    def lane_tiles(x):
        return [x[:, t * LANES:(t + 1) * LANES] for t in range(x.shape[1] // LANES)]

    def key_rows(ref, c):
        return ref[pl.ds(pl.multiple_of(c * tk, tk), tk), :]

    def chunks(body):
        if n_kv:
            lax.fori_loop(0, n_kv, lambda c, _: body(c) or 0, 0)

    sc = lax.dot_general(q2, kc_ref[...], nt, preferred_element_type=F32)
    sc_ref[...] = sc
    m_ref[...] = functools.reduce(jnp.maximum, lane_tiles(sc))

    def score_chunk(c):
        s = lax.dot_general(q2, key_rows(kl_ref, c), nt, preferred_element_type=F32)
        s_ref[c] = s
        m_ref[...] = functools.reduce(jnp.maximum, lane_tiles(s), m_ref[...])

    chunks(score_chunk)
    m_ref[...] = jnp.broadcast_to(jnp.max(m_ref[...], axis=-1, keepdims=True), (2 * tq, LANES))

    def weigh(s, vs):
        m = m_ref[...]
        p = jnp.concatenate([jnp.exp2(st - m).astype(BF16) for st in lane_tiles(s)], axis=1)
        return jnp.dot(p, vs, preferred_element_type=F32)

    acc_ref[...] = weigh(sc_ref[...], vc_ref[...])

    def weigh_chunk(c):
        acc_ref[...] += weigh(s_ref[c], key_rows(vl_ref, c))

    chunks(weigh_chunk)
    o = acc_ref[:, 0:LANES] / acc_ref[:, LANES:2 * LANES]
    oa = o[0:tq]
    ob = o[tq:2 * tq]
    if diff:
        lv = lam_ref[...]
        lam = (jnp.exp(jnp.sum(lv[0:1] * lv[1:2], keepdims=True))
               - jnp.exp(jnp.sum(lv[2:3] * lv[3:4], keepdims=True)) + lam_init)
        o = oa - lam * ob
        o = o * lax.rsqrt(jnp.mean(o * o, axis=-1, keepdims=True) + NORM_EPS) * sw_ref[...]
        o_ref[...] = (o * (1.0 - lam_init)).astype(BF16)
    else:
        o_ref[:, 0:LANES] = oa.astype(BF16)
        o_ref[:, LANES:2 * LANES] = ob.astype(BF16)


ATTN_TQ = 512
ATTN_TK = (4096, 2048, 1024, 512, 256, 128)


def _attention(qkv_q, qkv_c, qkv_l, extra, extra_specs, cols, n_heads, out_width, name, **static):
    qa_col, qb_col, k_col, v_col = cols
    sq, n_ctx = qkv_q.shape[0], qkv_c.shape[0]
    tq = min(sq, ATTN_TQ)
    n_lat = 0 if qkv_l is None else qkv_l.shape[0]
    tk = next((t for t in ATTN_TK if n_lat and n_lat % t == 0), 0)
    n_kv = n_lat // tk if n_lat else 0

    def kv_specs(rows):
        return [pl.BlockSpec((rows, LANES), lambda h, i: (0, k_col(h))),
                pl.BlockSpec((rows, 2 * LANES), lambda h, i: (0, v_col(h)))]

    in_specs = [pl.BlockSpec((tq, LANES), lambda h, i: (i, qa_col(h))),
                pl.BlockSpec((tq, LANES), lambda h, i: (i, qb_col(h)))] + kv_specs(n_ctx)
    args = [qkv_q, qkv_q, qkv_c, qkv_c]
    scratch = [pltpu.VMEM((2 * tq, n_ctx), F32), pltpu.VMEM((2 * tq, LANES), F32),
               pltpu.VMEM((2 * tq, 2 * LANES), F32)]
    if n_kv:
        in_specs += kv_specs(n_lat)
        args += [qkv_l, qkv_l]
        scratch = [pltpu.VMEM((n_kv, 2 * tq, tk), F32)] + scratch
    return pl.pallas_call(
        functools.partial(_attn_kernel, tk=tk, n_kv=n_kv, **static),
        grid=(n_heads, sq // tq),
        in_specs=in_specs + extra_specs,
        out_specs=pl.BlockSpec((tq, out_width), lambda h, i: (i, h)),
        out_shape=jax.ShapeDtypeStruct((sq, GROUP_WIDTH), BF16),
        scratch_shapes=scratch,
        compiler_params=_cparams("parallel", "parallel"),
        name=name,
    )(*args, *extra)


def diff_attention(qkv_q, qkv_c, qkv_l, da_lambda, subln_w, lam_init):
    cols = (lambda h: QA_BLK + h, lambda h: QB_BLK + h, lambda h: KD_BLK + h, lambda h: VD_BLK // 2 + h)
    extra_specs = [pl.BlockSpec((4, DA_HEAD_DIM), lambda h, i: (0, 0)), pl.BlockSpec((1, LANES), lambda h, i: (0, 0))]
    return _attention(qkv_q, qkv_c, qkv_l, [da_lambda, subln_w.reshape(1, LANES)], extra_specs, cols,
                      DA_HEADS, LANES, "diff_attn", diff=True, lam_init=lam_init)


def gqa_attention(qkv_q, qkv_c, qkv_l):
    cols = (lambda h: QG_BLK + 2 * h, lambda h: QG_BLK + 2 * h + 1, lambda h: KG_BLK + h,
            lambda h: VG_BLK // 2 + h)
    return _attention(qkv_q, qkv_c, qkv_l, [], [], cols, GQA_KV_HEADS, 2 * LANES, "gqa_attn",
                      diff=False, lam_init=0.0)


CONV_CHUNK = 256


def _conv_chunk(x_ref, base, n_rows, chunk, cw, cb):
    cur = x_ref[pl.ds(base, chunk), :]
    prev = x_ref[pl.ds(pl.multiple_of(jnp.maximum(base - 8, 0), 8), 8), :]
    nxt = x_ref[pl.ds(pl.multiple_of(jnp.minimum(base + chunk, n_rows - 8), 8), 8), :]
    prev = jnp.where(base == 0, 0.0, prev)
    nxt = jnp.where(base + chunk >= n_rows, 0.0, nxt)
    cat = jnp.concatenate([prev, cur, nxt], axis=0)
    n = chunk + 16
    xm2 = pltpu.roll(cat, 2, 0)[8:8 + chunk]
    xm1 = pltpu.roll(cat, 1, 0)[8:8 + chunk]
    xp1 = pltpu.roll(cat, n - 1, 0)[8:8 + chunk]
    return cb + cw[0:1] * xm2 + cw[1:2] * xm1 + cw[2:3] * cur + cw[3:4] * xp1


def _dwconv_kernel(x_ref, w_ref, b_ref, o_ref, *, n_rows, chunk, act):
    cw = w_ref[...]
    cb = b_ref[...]

    def body(c, _):
        base = pl.multiple_of(c * chunk, chunk)
        y = _conv_chunk(x_ref, base, n_rows, chunk, cw, cb)
        o_ref[pl.ds(base, chunk), :] = _silu(y) if act else y
        return 0

    lax.fori_loop(0, n_rows // chunk, body, 0)


def dwconv(p, col0, width, conv_w, conv_b, act):
    n_rows = p.shape[0]
    chunk = min(n_rows, CONV_CHUNK)
    b0 = col0 // LANES
    return pl.pallas_call(
        functools.partial(_dwconv_kernel, n_rows=n_rows, chunk=chunk, act=act),
        grid=(width // LANES,),
        in_specs=[pl.BlockSpec((n_rows, LANES), lambda j: (0, b0 + j)),
                  pl.BlockSpec((4, LANES), lambda j: (0, j)),
                  pl.BlockSpec((1, LANES), lambda j: (0, j))],
        out_specs=pl.BlockSpec((n_rows, LANES), lambda j: (0, j)),
        out_shape=jax.ShapeDtypeStruct((n_rows, width), F32),
        compiler_params=_cparams("parallel"),
        name="dwconv",
    )(p, conv_w, conv_b.reshape(1, width))


LRU_CHUNK = 256


def _lru_scan_chunk(a, u, h_in, reverse):
    n = a.shape[0]
    row = lax.broadcasted_iota(jnp.int32, (n, 1), 0)
    for k in (1, 2, 4):
        if reverse:
            keep = row < n - k
            a_s = jnp.where(keep, pltpu.roll(a, n - k, 0), 1.0)
            u_s = jnp.where(keep, pltpu.roll(u, n - k, 0), 0.0)
        else:
            keep = row >= k
            a_s = jnp.where(keep, pltpu.roll(a, k, 0), 1.0)
            u_s = jnp.where(keep, pltpu.roll(u, k, 0), 0.0)
        u = u + a * u_s
        a = a * a_s
    n_tiles = n // SUBLANES
    order = range(n_tiles - 1, -1, -1) if reverse else range(n_tiles)
    tiles = [None] * n_tiles
    h = h_in
    for i in order:
        sl = slice(i * SUBLANES, (i + 1) * SUBLANES)
        h = u[sl] + a[sl] * h
        tiles[i] = h
    return jnp.concatenate(tiles, axis=0)


def _lru_kernel(x_ref, g_ref, cw_ref, cb_ref, w_ref, b_ref, lam_ref, h0_ref, y_ref, hT_ref, hf_ref, hb_ref,
                *, n_rows, chunk):
    n_chunks = n_rows // chunk
    cw = cw_ref[...]
    cb = cb_ref[...]

    def gates(d):
        lam = lam_ref[d:d + 1, :]
        log_sig = jnp.minimum(lam, 0.0) - jnp.log1p(jnp.exp(-jnp.abs(lam)))
        return (w_ref[d, 0].astype(BF16), w_ref[d, 1].astype(BF16), b_ref[d, 0:1, :], b_ref[d, 1:2, :], log_sig)

    def sweep(base, h, params, reverse):
        w_r, w_i, b_r, b_i, log_sig = params
        x = _conv_chunk(x_ref, base, n_rows, chunk, cw, cb)
        xb = x.astype(BF16)
        r = jax.nn.sigmoid(jnp.dot(xb, w_r, preferred_element_type=F32) + b_r)
        i = jax.nn.sigmoid(jnp.dot(xb, w_i, preferred_element_type=F32) + b_i)
        log_a = RGLRU_C * r * log_sig
        a = jnp.exp(log_a)
        u = jnp.sqrt(1.0 - a * a) * (i * x)
        return _lru_scan_chunk(a, u, h, reverse)

    fwd, bwd = gates(0), gates(1)

    def body(c, carry):
        hf, hb = carry
        base_f = pl.multiple_of(c * chunk, chunk)
        base_b = pl.multiple_of((n_chunks - 1 - c) * chunk, chunk)
        hs_f = sweep(base_f, hf, fwd, False)
        hs_b = sweep(base_b, hb, bwd, True)
        hf_ref[pl.ds(base_f, chunk), :] = hs_f
        hb_ref[pl.ds(base_b, chunk), :] = hs_b
        return hs_f[chunk - 1:chunk], hs_b[0:1]

    hT_ref[0:1, :], hT_ref[1:2, :] = lax.fori_loop(0, n_chunks, body, (h0_ref[0:1, :], h0_ref[1:2, :]))

    def emit(c, _):
        rows = pl.ds(pl.multiple_of(c * chunk, chunk), chunk)
        y_ref[rows, :] = ((hf_ref[rows, :] + hb_ref[rows, :]) * jax.nn.gelu(g_ref[rows, :])).astype(BF16)
        return 0

    lax.fori_loop(0, n_chunks, emit, 0)


def rglru(p, x_col0, g_col0, conv_w, conv_b, w_gates, b_gates, lam, h0):
    n_rows = p.shape[0]
    chunk = min(n_rows, LRU_CHUNK)
    xb = x_col0 // LANES
    gb = g_col0 // LANES
    seq = pltpu.VMEM((n_rows, LANES), F32)
    return pl.pallas_call(
        functools.partial(_lru_kernel, n_rows=n_rows, chunk=chunk),
        grid=(LRU_BLOCKS,),
        in_specs=[pl.BlockSpec((n_rows, LANES), lambda j: (0, xb + j)),
                  pl.BlockSpec((n_rows, LANES), lambda j: (0, gb + j)),
                  pl.BlockSpec((4, LANES), lambda j: (0, j)),
                  pl.BlockSpec((1, LANES), lambda j: (0, j)),
                  pl.BlockSpec((2, 2, None, LANES, LANES), lambda j: (0, 0, j, 0, 0)),
                  pl.BlockSpec((2, 2, LANES), lambda j: (0, 0, j)),
                  pl.BlockSpec((2, LANES), lambda j: (0, j)),
                  pl.BlockSpec((2, LANES), lambda j: (0, j))],
        out_specs=[pl.BlockSpec((n_rows, LANES), lambda j: (0, j)),
                   pl.BlockSpec((2, LANES), lambda j: (0, j))],
        out_shape=[jax.ShapeDtypeStruct((n_rows, LRU_WIDTH), BF16),
                   jax.ShapeDtypeStruct((2, LRU_WIDTH), F32)],
        scratch_shapes=[seq, seq],
        compiler_params=_cparams("parallel"),
        name="rglru",
    )(p, p, conv_w, conv_b.reshape(1, LRU_WIDTH), w_gates, b_gates, lam, h0)


SSD_CHUNKS_PER_STEP = 2


def _ssd_direction(xs, bm, cm, dt_raw, dtT_raw, bias_row, bias_col, a_row, a_col, st_ref, reverse):
    q = SSD_CHUNK
    hi = lax.Precision.HIGHEST
    li = lax.broadcasted_iota(jnp.int32, (q, q), 0)
    si = lax.broadcasted_iota(jnp.int32, (q, q), 1)
    causal = (si >= li) if reverse else (si <= li)
    tri = causal.astype(F32)
    dt = jax.nn.softplus(dt_raw + bias_row)
    dtT = jax.nn.softplus(dtT_raw + bias_col)
    adt = dt * a_row
    adtT = dtT * a_col
    cum = jnp.dot(tri, adt, preferred_element_type=F32, precision=hi)
    cumT = lax.dot_general(adtT, tri, (((1,), (1,)), ((), ())), preferred_element_type=F32,
                           precision=hi)
    edge = cum[0:1] if reverse else cum[q - 1:q]

    def ex(v):
        r = v.shape[0]
        first = lax.broadcasted_iota(jnp.int32, (r, LANES), 1) < SSM_HEAD_DIM
        return jnp.concatenate(
            [jnp.where(first, jnp.broadcast_to(v[:, 2 * j:2 * j + 1], (r, LANES)),
                       jnp.broadcast_to(v[:, 2 * j + 1:2 * j + 2], (r, LANES)))
             for j in range(SSM_HEADS // 2)], axis=1)

    cum_e = ex(cum)
    edge_e = ex(edge)
    xdt = xs * ex(dt)
    xdd = (xdt * jnp.exp(edge_e - cum_e)).astype(BF16)
    xdt_b = xdt.astype(BF16)
    grow = jnp.exp(cum_e)
    ys = []
    for g in range(SSM_GROUPS):
        gsl = slice(g * 256, (g + 1) * 256)
        b_g = bm[:, g * SSM_D_STATE:(g + 1) * SSM_D_STATE]
        c_g = cm[:, g * SSM_D_STATE:(g + 1) * SSM_D_STATE].astype(BF16)
        b_gt = b_g.T.astype(BF16)
        gram = jnp.dot(c_g, b_gt, preferred_element_type=F32)
        st = st_ref[:, gsl]
        y_off = grow[:, gsl] * jnp.dot(c_g, st.astype(BF16), preferred_element_type=F32)
        st_ref[:, gsl] = jnp.exp(edge_e[:, gsl]) * st + jnp.dot(b_gt, xdd[:, gsl], preferred_element_type=F32)
        for hh in range(SSM_HEADS // SSM_GROUPS):
            h = g * (SSM_HEADS // SSM_GROUPS) + hh
            seg = cum[:, h:h + 1] - cumT[h:h + 1, :]
            decay = jnp.exp(jnp.where(causal, seg, -jnp.inf))
            m = (gram * decay).astype(BF16)
            hs = slice(h * SSM_HEAD_DIM, (h + 1) * SSM_HEAD_DIM)
            y_d = jnp.dot(m, xdt_b[:, hs], preferred_element_type=F32)
            ys.append(y_d + y_off[:, hh * SSM_HEAD_DIM:(hh + 1) * SSM_HEAD_DIM])
    return jnp.concatenate(ys, axis=1)


def _ssd_kernel(xf_ref, bf_ref, cf_ref, dtf_ref, dtTf_ref, xb_ref, bb_ref, cb_ref, dtb_ref, dtTb_ref,
                bias_ref, biasT_ref, alog_ref, alogT_ref, st0_ref, yf_ref, yb_ref, stT_ref, st_ref):
    c = pl.program_id(0)

    @pl.when(c == 0)
    def _():
        st_ref[...] = st0_ref[...]

    a_row = -jnp.exp(alog_ref[...])
    a_col = -jnp.exp(alogT_ref[...])
    q = SSD_CHUNK
    n_sub = xf_ref.shape[0] // q
    for k in range(n_sub):
        rows = slice(k * q, (k + 1) * q)
        yf_ref[rows, :] = _ssd_direction(xf_ref[rows, :], bf_ref[rows, :], cf_ref[rows, :],
                                         dtf_ref[rows, 0:SSM_HEADS], dtTf_ref[0:SSM_HEADS, rows],
                                         bias_ref[0:1, :], biasT_ref[:, 0:1], a_row[0:1, :], a_col[:, 0:1],
                                         st_ref.at[0], False)
    for k in reversed(range(n_sub)):
        rows = slice(k * q, (k + 1) * q)
        yb_ref[rows, :] = _ssd_direction(xb_ref[rows, :], bb_ref[rows, :], cb_ref[rows, :],
                                         dtb_ref[rows, SSM_HEADS:2 * SSM_HEADS],
                                         dtTb_ref[SSM_HEADS:2 * SSM_HEADS, rows],
                                         bias_ref[1:2, :], biasT_ref[:, 1:2], a_row[1:2, :], a_col[:, 1:2],
                                         st_ref.at[1], True)

    @pl.when(c == pl.num_programs(0) - 1)
    def _():
        stT_ref[...] = st_ref[...]


def ssd_scan(xbc, dt, dtT, dt_bias, a_log, st0):
    n_rows = xbc.shape[0]
    q = SSD_CHUNK * SSD_CHUNKS_PER_STEP
    nc = n_rows // q
    fwd = lambda c: c
    bwd = lambda c: nc - 1 - c

    def specs(ix):
        return [pl.BlockSpec((q, SSM_D_INNER), lambda c: (ix(c), 0)),
                pl.BlockSpec((q, 2 * SSM_D_STATE), lambda c: (ix(c), 2)),
                pl.BlockSpec((q, 2 * SSM_D_STATE), lambda c: (ix(c), 3)),
                pl.BlockSpec((q, 2 * SSM_HEADS), lambda c: (ix(c), 0)),
                pl.BlockSpec((2 * SSM_HEADS, q), lambda c: (0, ix(c)))]

    small = lambda shape: pl.BlockSpec(shape, lambda c: (0,) * len(shape))
    st_shape = (2, SSM_D_STATE, SSM_D_INNER)
    return pl.pallas_call(
        _ssd_kernel,
        grid=(nc,),
        in_specs=specs(fwd) + specs(bwd) + [small((2, SSM_HEADS)), small((SSM_HEADS, 2)),
                                            small((2, SSM_HEADS)), small((SSM_HEADS, 2)), small(st_shape)],
        out_specs=[pl.BlockSpec((q, SSM_D_INNER), lambda c: (c, 0)),
                   pl.BlockSpec((q, SSM_D_INNER), lambda c: (nc - 1 - c, 0)),
                   small(st_shape)],
        out_shape=[jax.ShapeDtypeStruct((n_rows, SSM_D_INNER), F32),
                   jax.ShapeDtypeStruct((n_rows, SSM_D_INNER), F32),
                   jax.ShapeDtypeStruct(st_shape, F32)],
        scratch_shapes=[pltpu.VMEM(st_shape, F32)],
        compiler_params=_cparams("arbitrary"),
        name="ssd_scan",
    )(xbc, xbc, xbc, dt, dtT, xbc, xbc, xbc, dt, dtT, dt_bias, dt_bias.T, a_log, a_log.T, st0)


def _ssd_finish_kernel(yf_ref, yb_ref, xs_ref, z_ref, d_ref, nw_ref, o_ref):
    hrow = lax.broadcasted_iota(jnp.int32, (SSM_HEADS, SSM_D_INNER), 0)
    hcol = lax.broadcasted_iota(jnp.int32, (SSM_HEADS, SSM_D_INNER), 1) // SSM_HEAD_DIM
    d_e = jnp.sum(jnp.where(hrow == hcol, d_ref[...], 0.0), axis=0, keepdims=True)
    y = yf_ref[...] + yb_ref[...] + d_e * xs_ref[...]
    y = y * _silu(z_ref[...])
    o_ref[...] = (y * lax.rsqrt(jnp.mean(y * y, axis=-1, keepdims=True) + NORM_EPS) * nw_ref[...]).astype(BF16)


def ssd_finish(yf, yb, xbc, p, z_col0, d_skip, norm_w):
    n_rows = yf.shape[0]
    tm = min(n_rows, 512)
    zb = z_col0 // SSM_D_INNER
    blk = lambda cb: pl.BlockSpec((tm, SSM_D_INNER), lambda i: (i, cb))
    return pl.pallas_call(
        _ssd_finish_kernel,
        grid=(n_rows // tm,),
        in_specs=[blk(0), blk(0), blk(0), blk(zb),
                  pl.BlockSpec((SSM_HEADS, 1), lambda i: (0, 0)),
                  pl.BlockSpec((1, SSM_D_INNER), lambda i: (0, 0))],
        out_specs=blk(0),
        out_shape=jax.ShapeDtypeStruct((n_rows, SSM_D_INNER), BF16),
        compiler_params=_cparams("parallel"),
        name="ssd_finish",
    )(yf, yb, xbc, p, d_skip.reshape(SSM_HEADS, 1), norm_w.reshape(1, SSM_D_INNER))


LRU_X_COL = 2560
LRU_G_COL = 3072
SSM_Z_COL = 3584
SSM_XBC_COL = 4096


def _mixers(p, dt, qkv_q, qkv_c, qkv_l, lw, lam_init, lru_h0, ssd_st0):
    ya = diff_attention(qkv_q, qkv_c, qkv_l, lw["da_lambda"], lw["da_subln"], lam_init)
    yb = gqa_attention(qkv_q, qkv_c, qkv_l)
    yc, lru_hT = rglru(p, LRU_X_COL, LRU_G_COL, lw["lru_conv_w"], lw["lru_conv_b"], lw["lru_w_gates"],
                       lw["lru_b_gates"], lw["lru_lambda"], lru_h0)
    xbc = dwconv(p, SSM_XBC_COL, 2 * SSM_D_INNER, lw["ssm_conv_w"], lw["ssm_conv_b"], act=True)
    yf, ybk, ssd_stT = ssd_scan(xbc, dt, dt.T, lw["ssm_dt_bias"], lw["ssm_a_log"], ssd_st0)
    yd = ssd_finish(yf, ybk, xbc, p, SSM_Z_COL, lw["ssm_d"], lw["ssm_norm"])
    return [ya, yb, yc, yd], lru_hT, ssd_stT


def kernel(x, c, ctx, c_ctx, w_mod, b_mod, mix_norm, ffn_norm, w_in, w_out, da_lambda, da_subln, gqa_q_norm,
           gqa_k_norm, lru_conv_w, lru_conv_b, lru_w_gates, lru_b_gates, lru_lambda, ssm_conv_w, ssm_conv_b,
           ssm_dt_bias, ssm_a_log, ssm_d, ssm_norm, ffn_w_up, ffn_conv_w, ffn_conv_b, ffn_w_down, final_norm):
    depth = w_mod.shape[0]
    d = x.shape[-1]
    xl = x[0]
    xc = ctx[0]
    n_lat, n_ctx = xl.shape[0], xc.shape[0]

    cond8 = jnp.zeros((8, d), F32).at[0].set(c[0]).at[1].set(c_ctx)
    mods = modulation_all(cond8, w_mod, b_mod)
    tabs_l = _rope_tables(n_lat, DA_HEAD_DIM) + _rope_tables(n_lat, GQA_HEAD_DIM)
    tabs_c = _rope_tables(n_ctx, DA_HEAD_DIM) + _rope_tables(n_ctx, GQA_HEAD_DIM)
    lru_zero = jnp.zeros((2, LRU_WIDTH), F32)
    ssd_zero = jnp.zeros((2, SSM_D_STATE, SSM_D_INNER), F32)
    w_in_t = jnp.swapaxes(w_in, 1, 2)

    for layer in range(depth):
        need_ctx = layer < depth - 1
        lam_init = 0.8 - 0.6 * math.exp(-0.3 * layer)
        ml = [mods[layer, 0:1, k * d:(k + 1) * d] for k in range(6)]
        mc = [mods[layer, 1:2, k * d:(k + 1) * d] for k in range(6)]
        lw = dict(da_lambda=da_lambda[layer], da_subln=da_subln[layer],
                  lru_conv_w=lru_conv_w[layer], lru_conv_b=lru_conv_b[layer], lru_w_gates=lru_w_gates[layer],
                  lru_b_gates=lru_b_gates[layer], lru_lambda=lru_lambda[layer], ssm_conv_w=ssm_conv_w[layer],
                  ssm_conv_b=ssm_conv_b[layer], ssm_dt_bias=ssm_dt_bias[layer], ssm_a_log=ssm_a_log[layer],
                  ssm_d=ssm_d[layer], ssm_norm=ssm_norm[layer])
        nw_m = mix_norm[layer].reshape(1, d)
        nw_f = ffn_norm[layer].reshape(1, d)

        xn_c = adaln(xc, nw_m, mc[0], mc[1])
        xn_l = adaln(xl, nw_m, ml[0], ml[1])
        p_c, dt_c = in_proj(xn_c, w_in_t, layer), dt_proj(xn_c, w_in_t, layer)
        p_l, dt_l = in_proj(xn_l, w_in_t, layer), dt_proj(xn_l, w_in_t, layer)
        qkv_c = attn_prep(p_c, tabs_c, gqa_q_norm[layer], gqa_k_norm[layer], rope=False)
        qkv_l = attn_prep(p_l, tabs_l, gqa_q_norm[layer], gqa_k_norm[layer], rope=True)

        y_c, lru_h, ssd_st = _mixers(p_c, dt_c, qkv_c, qkv_c, None, lw, lam_init, lru_zero, ssd_zero)
        y_l, _, _ = _mixers(p_l, dt_l, qkv_l, qkv_c, qkv_l, lw, lam_init, lru_h, ssd_st)

        xl = resid_proj(y_l, w_out, layer, xl, ml[2])
        act = ffn_up(adaln(xl, nw_f, ml[3], ml[4]), ffn_w_up, layer, ffn_conv_w[layer], ffn_conv_b[layer])
        xl = resid_proj([act], ffn_w_down, layer, xl, ml[5])
        if need_ctx:
            xc = resid_proj(y_c, w_out, layer, xc, mc[2])
            act = ffn_up(adaln(xc, nw_f, mc[3], mc[4]), ffn_w_up, layer, ffn_conv_w[layer], ffn_conv_b[layer])
            xc = resid_proj([act], ffn_w_down, layer, xc, mc[5])

    return final_rmsnorm(xl, final_norm)[None]
```

```python
import functools
import math

import jax
import jax.numpy as jnp
from jax import lax
from jax.experimental import pallas as pl
from jax.experimental.pallas import tpu as pltpu

F32 = jnp.float32
BF16 = jnp.bfloat16

D_MODEL = 2048
DEPTH = 4
GRID_W = 64
GROUP_WIDTH = 512
DA_HEAD_DIM = 64
DA_HEADS = 4
GQA_HEAD_DIM = 128
GQA_HEADS = 4
GQA_KV_HEADS = 2
LRU_WIDTH = 512
LRU_BLOCKS = 4
RGLRU_C = 8.0
SSM_D_INNER = 512
SSM_HEAD_DIM = 64
SSM_HEADS = 8
SSM_GROUPS = 2
SSM_D_STATE = 128
SSD_CHUNK = 128
D_FF = 5632
ROPE_THETA = 10000.0
NORM_EPS = 1e-6
LOG2E = math.log2(math.e)

LANES = 128
SUBLANES = 8
HALO = 16
IN_COLS = 5136
ATTN_COLS = 2560
QA_BLK, QB_BLK, KD_BLK, VD_BLK = 0, 4, 8, 12
QG_BLK, KG_BLK, VG_BLK = 20, 24, 26
QKV_COLS = 30 * LANES
VMEM_LIMIT = 56 * 1024 * 1024


def _cparams(*sem):
    return pltpu.CompilerParams(dimension_semantics=sem, vmem_limit_bytes=VMEM_LIMIT)


def _silu(x):
    return x * jax.nn.sigmoid(x)


def _norm_mod(x, nw, shift, scale):
    y = x * lax.rsqrt(jnp.mean(x * x, axis=-1, keepdims=True) + NORM_EPS) * nw
    return y * (1.0 + scale) + shift


def _mod_kernel(c_ref, w_ref, b_ref, o_ref):
    s = _silu(c_ref[...]).astype(BF16)
    o_ref[...] = jnp.dot(s, w_ref[...].astype(BF16), preferred_element_type=F32) + b_ref[...]


def modulation_all(cond8, w_mod, b_mod):
    depth, d, n = w_mod.shape
    tn = 1024
    return pl.pallas_call(
        _mod_kernel,
        grid=(depth, n // tn),
        in_specs=[
            pl.BlockSpec((8, d), lambda l, j: (0, 0)),
            pl.BlockSpec((None, d, tn), lambda l, j: (l, 0, j)),
            pl.BlockSpec((None, 1, tn), lambda l, j: (l, 0, j)),
        ],
        out_specs=pl.BlockSpec((None, 8, tn), lambda l, j: (l, 0, j)),
        out_shape=jax.ShapeDtypeStruct((depth, 8, n), F32),
        compiler_params=_cparams("parallel", "parallel"),
        name="modulation",
    )(cond8, w_mod, b_mod.reshape(depth, 1, n))


IN_MAIN_COLS = 5120
IN_TN = 1024


ADALN_ROWS = 16


def _adaln_kernel(x_ref, nw_ref, sh_ref, sc_ref, o_ref):
    gain = nw_ref[...] * (1.0 + sc_ref[...])
    shift = sh_ref[...]
    d = x_ref.shape[1]

    def body(r, _):
        rows = pl.ds(pl.multiple_of(r * ADALN_ROWS, ADALN_ROWS), ADALN_ROWS)
        x = x_ref[rows, :]
        inv = lax.rsqrt(jnp.sum(x * x, axis=-1, keepdims=True) * (1.0 / d) + NORM_EPS)
        o_ref[rows, :] = (x * inv * gain + shift).astype(BF16)
        return 0

    lax.fori_loop(0, x_ref.shape[0] // ADALN_ROWS, body, 0, unroll=4)


def adaln(x, nw, shift, scale):
    m, d = x.shape
    tm = min(m, 1024)
    vec = pl.BlockSpec((1, d), lambda i: (0, 0))
    return pl.pallas_call(
        _adaln_kernel,
        grid=(m // tm,),
        in_specs=[pl.BlockSpec((tm, d), lambda i: (i, 0)), vec, vec, vec],
        out_specs=pl.BlockSpec((tm, d), lambda i: (i, 0)),
        out_shape=jax.ShapeDtypeStruct((m, d), BF16),
        compiler_params=_cparams("parallel"),
        name="adaln",
    )(x, nw, shift, scale)


_NT = (((1,), (1,)), ((), ()))


def _inproj_kernel(xn_ref, w_ref, o_ref, wb_ref):
    @pl.when(pl.program_id(1) == 0)
    def _():
        wb_ref[...] = w_ref[...].astype(BF16)

    o_ref[...] = lax.dot_general(xn_ref[...], wb_ref[...], _NT, preferred_element_type=F32)


def in_proj(xn, w_in_t, layer):
    m, d = xn.shape
    tm = min(m, 1024)
    return pl.pallas_call(
        _inproj_kernel,
        grid=(IN_MAIN_COLS // IN_TN, m // tm),
        in_specs=[pl.BlockSpec((tm, d), lambda j, i: (i, 0)),
                  pl.BlockSpec((None, IN_TN, d), lambda j, i: (layer, j, 0))],
        out_specs=pl.BlockSpec((tm, IN_TN), lambda j, i: (i, j)),
        out_shape=jax.ShapeDtypeStruct((m, IN_MAIN_COLS), F32),
        scratch_shapes=[pltpu.VMEM((IN_TN, d), BF16)],
        compiler_params=_cparams("parallel", "arbitrary"),
        name="in_proj",
    )(xn, w_in_t)


def _dtproj_kernel(xn_ref, w_ref, o_ref):
    o_ref[...] = lax.dot_general(xn_ref[...], w_ref[...].astype(BF16), _NT, preferred_element_type=F32)


def dt_proj(xn, w_in_t, layer):
    m, d = xn.shape
    tm = min(m, 1024)
    n_dt = IN_COLS - IN_MAIN_COLS
    return pl.pallas_call(
        _dtproj_kernel,
        grid=(m // tm,),
        in_specs=[pl.BlockSpec((tm, d), lambda i: (i, 0)),
                  pl.BlockSpec((None, n_dt, d), lambda i: (layer, IN_MAIN_COLS // n_dt, 0))],
        out_specs=pl.BlockSpec((tm, n_dt), lambda i: (i, 0)),
        out_shape=jax.ShapeDtypeStruct((m, n_dt), F32),
        compiler_params=_cparams("parallel"),
        name="dt_proj",
    )(xn, w_in_t)


def _resid_kernel(*refs, n_y):
    y_refs = refs[:n_y]
    w_ref, x_ref, g_ref, o_ref, wb_ref = refs[n_y:]

    @pl.when(pl.program_id(1) == 0)
    def _():
        wb_ref[...] = w_ref[...].astype(BF16)

    kk = wb_ref.shape[0] // n_y
    acc = functools.reduce(jnp.add, [
        jnp.dot(y_refs[a][...], wb_ref[a * kk:(a + 1) * kk, :], preferred_element_type=F32) for a in range(n_y)])
    o_ref[...] = x_ref[...] + g_ref[...] * acc


def resid_proj(ys, w_all, layer, x, gate):
    m = x.shape[0]
    _, k, n = w_all.shape
    kk = k // len(ys)
    tm = min(m, 512)
    tn = 1024 if k <= 2048 else 512
    return pl.pallas_call(
        functools.partial(_resid_kernel, n_y=len(ys)),
        grid=(n // tn, m // tm),
        in_specs=[pl.BlockSpec((tm, kk), lambda j, i: (i, 0)) for _ in ys]
                 + [pl.BlockSpec((None, k, tn), lambda j, i: (layer, 0, j)),
                    pl.BlockSpec((tm, tn), lambda j, i: (i, j)),
                    pl.BlockSpec((1, tn), lambda j, i: (0, j))],
        out_specs=pl.BlockSpec((tm, tn), lambda j, i: (i, j)),
        out_shape=jax.ShapeDtypeStruct((m, n), F32),
        scratch_shapes=[pltpu.VMEM((k, tn), BF16)],
        compiler_params=_cparams("parallel", "arbitrary"),
        name="resid_proj",
    )(*ys, w_all, x, gate)


def _ffn_up_kernel(xp_ref, x_ref, xx_ref, wg_ref, wu_ref, cw_ref, cb_ref, o_ref, xn_ref, g_ref, wb_ref,
                   *, tm, n_row_blocks):
    i = pl.program_id(1)

    @pl.when(i == 0)
    def _():
        wb_ref[0] = wg_ref[...].astype(BF16)
        wb_ref[1] = wu_ref[...].astype(BF16)

    xn_ref[pl.ds(0, HALO), :] = jnp.where(i == 0, jnp.zeros_like(xp_ref), xp_ref[...])
    xn_ref[pl.ds(HALO, tm), :] = x_ref[...]
    xn_ref[pl.ds(HALO + tm, HALO), :] = jnp.where(i == n_row_blocks - 1, jnp.zeros_like(xx_ref), xx_ref[...])
    g_ref[...] = jnp.dot(xn_ref[...], wb_ref[0], preferred_element_type=F32)
    up = jnp.dot(x_ref[...], wb_ref[1], preferred_element_type=F32)
    cw = cw_ref[...]
    gc = (cb_ref[...] + cw[0:1] * g_ref[pl.ds(HALO - 1, tm), :] + cw[1:2] * g_ref[pl.ds(HALO, tm), :]
          + cw[2:3] * g_ref[pl.ds(HALO + 1, tm), :])
    o_ref[...] = (_silu(gc) * up).astype(BF16)


def ffn_up(xn, w_up_all, layer, conv_w, conv_b):
    m, d = xn.shape
    dff = w_up_all.shape[2] // 2
    tm = min(m, 1024)
    tn = 512
    nrb = m // tm
    hb = tm // HALO
    nhb = m // HALO
    return pl.pallas_call(
        functools.partial(_ffn_up_kernel, tm=tm, n_row_blocks=nrb),
        grid=(dff // tn, nrb),
        in_specs=[pl.BlockSpec((HALO, d), lambda j, i: (jnp.maximum(i * hb - 1, 0), 0)),
                  pl.BlockSpec((tm, d), lambda j, i: (i, 0)),
                  pl.BlockSpec((HALO, d), lambda j, i: (jnp.minimum((i + 1) * hb, nhb - 1), 0)),
                  pl.BlockSpec((None, d, tn), lambda j, i: (layer, 0, j)),
                  pl.BlockSpec((None, d, tn), lambda j, i: (layer, 0, j + dff // tn)),
                  pl.BlockSpec((3, tn), lambda j, i: (0, j)),
                  pl.BlockSpec((1, tn), lambda j, i: (0, j))],
        out_specs=pl.BlockSpec((tm, tn), lambda j, i: (i, j)),
        out_shape=jax.ShapeDtypeStruct((m, dff), BF16),
        scratch_shapes=[pltpu.VMEM((tm + 2 * HALO, d), BF16), pltpu.VMEM((tm + 2 * HALO, tn), F32),
                        pltpu.VMEM((2, d, tn), BF16)],
        compiler_params=_cparams("parallel", "arbitrary"),
        name="ffn_up",
    )(xn, xn, xn, w_up_all, w_up_all, conv_w, conv_b.reshape(1, dff))


def _rmsnorm_kernel(x_ref, w_ref, o_ref):
    x = x_ref[...]
    o_ref[...] = x * lax.rsqrt(jnp.mean(x * x, axis=-1, keepdims=True) + NORM_EPS) * w_ref[...]


def final_rmsnorm(x, w):
    m, d = x.shape
    tm = min(m, 512)
    return pl.pallas_call(
        _rmsnorm_kernel,
        grid=(m // tm,),
        in_specs=[pl.BlockSpec((tm, d), lambda i: (i, 0)), pl.BlockSpec((1, d), lambda i: (0, 0))],
        out_specs=pl.BlockSpec((tm, d), lambda i: (i, 0)),
        out_shape=jax.ShapeDtypeStruct((m, d), F32),
        compiler_params=_cparams("parallel"),
        name="final_norm",
    )(x, w.reshape(1, d))


def _rope_tables(n_tok, head_dim):
    q = head_dim // 4
    lane = jnp.arange(LANES, dtype=jnp.int32)
    u = lane % head_dim
    region = u // q
    freqs = jnp.power(ROPE_THETA, -(u % q).astype(F32) / q)[None, :]
    ang_r = jnp.arange(n_tok // GRID_W, dtype=jnp.int32).astype(F32)[:, None] * freqs
    ang_c = jnp.arange(GRID_W, dtype=jnp.int32).astype(F32)[:, None] * freqs
    by_row = (region < 2)[None, None, :]

    def per_token(f):
        return jnp.where(by_row, f(ang_r)[:, None, :], f(ang_c)[None, :, :]).reshape(n_tok, LANES)

    cos, sin = per_token(jnp.cos), per_token(jnp.sin)
    first = (region % 2 == 0)[None, :]
    return cos, jnp.where(first, -sin, 0.0), jnp.where(first, 0.0, sin)


def _prep_kernel(p_ref, cd_ref, ad_ref, bd_ref, cg_ref, ag_ref, bg_ref, qn_ref, kn_ref, o_ref, *, rope):
    lane = lax.broadcasted_iota(jnp.int32, (1, LANES), 1)
    lo = lane < DA_HEAD_DIM

    def blk(b):
        return p_ref[:, b * LANES:(b + 1) * LANES]

    def put(b, v):
        o_ref[:, b * LANES:(b + 1) * LANES] = v.astype(BF16)

    def rot(x, cos, sa, sb, quarter):
        if not rope:
            return x
        return (x * cos[...] + pltpu.roll(x, LANES - quarter, 1) * sa[...]
                + pltpu.roll(x, quarter, 1) * sb[...])

    def rms(x, w):
        return x * lax.rsqrt(jnp.mean(x * x, axis=-1, keepdims=True) + NORM_EPS) * w[...]

    qd = DA_HEAD_DIM // 4
    qg = GQA_HEAD_DIM // 4
    ones = jnp.ones((p_ref.shape[0], LANES), F32)
    for h in range(DA_HEADS):
        q = rot(blk(h), cd_ref, ad_ref, bd_ref, qd) * (DA_HEAD_DIM ** -0.5 * LOG2E)
        put(QA_BLK + h, jnp.where(lo, q, 0.0))
        put(QB_BLK + h, jnp.where(lo, 0.0, q))
        put(KD_BLK + h, rot(blk(4 + h), cd_ref, ad_ref, bd_ref, qd))
        put(VD_BLK + 2 * h, blk(8 + h))
        put(VD_BLK + 2 * h + 1, ones)
    for h in range(GQA_HEADS):
        q = rot(rms(blk(12 + h), qn_ref), cg_ref, ag_ref, bg_ref, qg) * (GQA_HEAD_DIM ** -0.5 * LOG2E)
        put(QG_BLK + h, q)
    for h in range(GQA_KV_HEADS):
        put(KG_BLK + h, rot(rms(blk(16 + h), kn_ref), cg_ref, ag_ref, bg_ref, qg))
        put(VG_BLK + 2 * h, blk(18 + h))
        put(VG_BLK + 2 * h + 1, ones)


def attn_prep(p, tables, q_norm, k_norm, rope):
    m = p.shape[0]
    tm = min(m, 512)
    tab = pl.BlockSpec((tm, LANES), lambda i: (i, 0))
    vec = pl.BlockSpec((1, LANES), lambda i: (0, 0))
    return pl.pallas_call(
        functools.partial(_prep_kernel, rope=rope),
        grid=(m // tm,),
        in_specs=[pl.BlockSpec((tm, ATTN_COLS), lambda i: (i, 0))] + [tab] * 6 + [vec, vec],
        out_specs=pl.BlockSpec((tm, QKV_COLS), lambda i: (i, 0)),
        out_shape=jax.ShapeDtypeStruct((m, QKV_COLS), BF16),
        compiler_params=_cparams("parallel"),
        name="attn_prep",
    )(p, *tables, q_norm.reshape(1, LANES), k_norm.reshape(1, LANES))


def _attn_kernel(*refs, tk, n_kv, diff, lam_init):
    refs = list(refs)
    qa_ref, qb_ref, kc_ref, vc_ref = refs[:4]
    del refs[:4]
    if n_kv:
        kl_ref, vl_ref = refs[:2]
        del refs[:2]
    if diff:
        lam_ref, sw_ref = refs[:2]
        del refs[:2]
    o_ref = refs.pop(0)
    if n_kv:
        s_ref = refs.pop(0)
    sc_ref, m_ref, acc_ref = refs
    tq = qa_ref.shape[0]
    nt = (((1,), (1,)), ((), ()))
    q2 = jnp.concatenate([qa_ref[...], qb_ref[...]], axis=0)

    def lane_tiles(x):
        return [x[:, t * LANES:(t + 1) * LANES] for t in range(x.shape[1] // LANES)]

    def key_rows(ref, c):
        return ref[pl.ds(pl.multiple_of(c * tk, tk), tk), :]

    def chunks(body):
        if n_kv:
            lax.fori_loop(0, n_kv, lambda c, _: body(c) or 0, 0)

    sc = lax.dot_general(q2, kc_ref[...], nt, preferred_element_type=F32)
    sc_ref[...] = sc
    m_ref[...] = functools.reduce(jnp.maximum, lane_tiles(sc))

    def score_chunk(c):
        s = lax.dot_general(q2, key_rows(kl_ref, c), nt, preferred_element_type=F32)
        s_ref[c] = s
        m_ref[...] = functools.reduce(jnp.maximum, lane_tiles(s), m_ref[...])

    chunks(score_chunk)
    m_ref[...] = jnp.broadcast_to(jnp.max(m_ref[...], axis=-1, keepdims=True), (2 * tq, LANES))

    def weigh(s, vs):
        m = m_ref[...]
        p = jnp.concatenate([jnp.exp2(st - m).astype(BF16) for st in lane_tiles(s)], axis=1)
        return jnp.dot(p, vs, preferred_element_type=F32)

    acc_ref[...] = weigh(sc_ref[...], vc_ref[...])

    def weigh_chunk(c):
        acc_ref[...] += weigh(s_ref[c], key_rows(vl_ref, c))

    chunks(weigh_chunk)
    o = acc_ref[:, 0:LANES] / acc_ref[:, LANES:2 * LANES]
    oa = o[0:tq]
    ob = o[tq:2 * tq]
    if diff:
        lv = lam_ref[...]
        lam = (jnp.exp(jnp.sum(lv[0:1] * lv[1:2], keepdims=True))
               - jnp.exp(jnp.sum(lv[2:3] * lv[3:4], keepdims=True)) + lam_init)
        o = oa - lam * ob
        o = o * lax.rsqrt(jnp.mean(o * o, axis=-1, keepdims=True) + NORM_EPS) * sw_ref[...]
        o_ref[...] = (o * (1.0 - lam_init)).astype(BF16)
    else:
        o_ref[:, 0:LANES] = oa.astype(BF16)
        o_ref[:, LANES:2 * LANES] = ob.astype(BF16)


ATTN_TQ = 512
ATTN_TK = (4096, 2048, 1024, 512, 256, 128)


def _attention(qkv_q, qkv_c, qkv_l, extra, extra_specs, cols, n_heads, out_width, name, **static):
    qa_col, qb_col, k_col, v_col = cols
    sq, n_ctx = qkv_q.shape[0], qkv_c.shape[0]
    tq = min(sq, ATTN_TQ)
    n_lat = 0 if qkv_l is None else qkv_l.shape[0]
    tk = next((t for t in ATTN_TK if n_lat and n_lat % t == 0), 0)
    n_kv = n_lat // tk if n_lat else 0

    def kv_specs(rows):
        return [pl.BlockSpec((rows, LANES), lambda h, i: (0, k_col(h))),
                pl.BlockSpec((rows, 2 * LANES), lambda h, i: (0, v_col(h)))]

    in_specs = [pl.BlockSpec((tq, LANES), lambda h, i: (i, qa_col(h))),
                pl.BlockSpec((tq, LANES), lambda h, i: (i, qb_col(h)))] + kv_specs(n_ctx)
    args = [qkv_q, qkv_q, qkv_c, qkv_c]
    scratch = [pltpu.VMEM((2 * tq, n_ctx), F32), pltpu.VMEM((2 * tq, LANES), F32),
               pltpu.VMEM((2 * tq, 2 * LANES), F32)]
    if n_kv:
        in_specs += kv_specs(n_lat)
        args += [qkv_l, qkv_l]
        scratch = [pltpu.VMEM((n_kv, 2 * tq, tk), F32)] + scratch
    return pl.pallas_call(
        functools.partial(_attn_kernel, tk=tk, n_kv=n_kv, **static),
        grid=(n_heads, sq // tq),
        in_specs=in_specs + extra_specs,
        out_specs=pl.BlockSpec((tq, out_width), lambda h, i: (i, h)),
        out_shape=jax.ShapeDtypeStruct((sq, GROUP_WIDTH), BF16),
        scratch_shapes=scratch,
        compiler_params=_cparams("parallel", "parallel"),
        name=name,
    )(*args, *extra)


def diff_attention(qkv_q, qkv_c, qkv_l, da_lambda, subln_w, lam_init):
    cols = (lambda h: QA_BLK + h, lambda h: QB_BLK + h, lambda h: KD_BLK + h, lambda h: VD_BLK // 2 + h)
    extra_specs = [pl.BlockSpec((4, DA_HEAD_DIM), lambda h, i: (0, 0)), pl.BlockSpec((1, LANES), lambda h, i: (0, 0))]
    return _attention(qkv_q, qkv_c, qkv_l, [da_lambda, subln_w.reshape(1, LANES)], extra_specs, cols,
                      DA_HEADS, LANES, "diff_attn", diff=True, lam_init=lam_init)


def gqa_attention(qkv_q, qkv_c, qkv_l):
    cols = (lambda h: QG_BLK + 2 * h, lambda h: QG_BLK + 2 * h + 1, lambda h: KG_BLK + h,
            lambda h: VG_BLK // 2 + h)
    return _attention(qkv_q, qkv_c, qkv_l, [], [], cols, GQA_KV_HEADS, 2 * LANES, "gqa_attn",
                      diff=False, lam_init=0.0)


CONV_CHUNK = 256


def _conv_chunk(x_ref, base, n_rows, chunk, cw, cb):
    cur = x_ref[pl.ds(base, chunk), :]
    prev = x_ref[pl.ds(pl.multiple_of(jnp.maximum(base - 8, 0), 8), 8), :]
    nxt = x_ref[pl.ds(pl.multiple_of(jnp.minimum(base + chunk, n_rows - 8), 8), 8), :]
    prev = jnp.where(base == 0, 0.0, prev)
    nxt = jnp.where(base + chunk >= n_rows, 0.0, nxt)
    cat = jnp.concatenate([prev, cur, nxt], axis=0)
    n = chunk + 16
    xm2 = pltpu.roll(cat, 2, 0)[8:8 + chunk]
    xm1 = pltpu.roll(cat, 1, 0)[8:8 + chunk]
    xp1 = pltpu.roll(cat, n - 1, 0)[8:8 + chunk]
    return cb + cw[0:1] * xm2 + cw[1:2] * xm1 + cw[2:3] * cur + cw[3:4] * xp1


def _dwconv_kernel(x_ref, w_ref, b_ref, o_ref, *, n_rows, chunk, act):
    cw = w_ref[...]
    cb = b_ref[...]

    def body(c, _):
        base = pl.multiple_of(c * chunk, chunk)
        y = _conv_chunk(x_ref, base, n_rows, chunk, cw, cb)
        o_ref[pl.ds(base, chunk), :] = _silu(y) if act else y
        return 0

    lax.fori_loop(0, n_rows // chunk, body, 0)


def dwconv(p, col0, width, conv_w, conv_b, act):
    n_rows = p.shape[0]
    chunk = min(n_rows, CONV_CHUNK)
    b0 = col0 // LANES
    return pl.pallas_call(
        functools.partial(_dwconv_kernel, n_rows=n_rows, chunk=chunk, act=act),
        grid=(width // LANES,),
        in_specs=[pl.BlockSpec((n_rows, LANES), lambda j: (0, b0 + j)),
                  pl.BlockSpec((4, LANES), lambda j: (0, j)),
                  pl.BlockSpec((1, LANES), lambda j: (0, j))],
        out_specs=pl.BlockSpec((n_rows, LANES), lambda j: (0, j)),
        out_shape=jax.ShapeDtypeStruct((n_rows, width), F32),
        compiler_params=_cparams("parallel"),
        name="dwconv",
    )(p, conv_w, conv_b.reshape(1, width))


LRU_CHUNK = 256


def _lru_scan_chunk(a, u, h_in, reverse):
    n = a.shape[0]
    row = lax.broadcasted_iota(jnp.int32, (n, 1), 0)
    for k in (1, 2, 4):
        if reverse:
            keep = row < n - k
            a_s = jnp.where(keep, pltpu.roll(a, n - k, 0), 1.0)
            u_s = jnp.where(keep, pltpu.roll(u, n - k, 0), 0.0)
        else:
            keep = row >= k
            a_s = jnp.where(keep, pltpu.roll(a, k, 0), 1.0)
            u_s = jnp.where(keep, pltpu.roll(u, k, 0), 0.0)
        u = u + a * u_s
        a = a * a_s
    n_tiles = n // SUBLANES
    order = range(n_tiles - 1, -1, -1) if reverse else range(n_tiles)
    tiles = [None] * n_tiles
    h = h_in
    for i in order:
        sl = slice(i * SUBLANES, (i + 1) * SUBLANES)
        h = u[sl] + a[sl] * h
        tiles[i] = h
    return jnp.concatenate(tiles, axis=0)


def _lru_kernel(x_ref, g_ref, cw_ref, cb_ref, w_ref, b_ref, lam_ref, h0_ref, y_ref, hT_ref, hf_ref, hb_ref,
                *, n_rows, chunk):
    n_chunks = n_rows // chunk
    cw = cw_ref[...]
    cb = cb_ref[...]

    def gates(d):
        lam = lam_ref[d:d + 1, :]
        log_sig = jnp.minimum(lam, 0.0) - jnp.log1p(jnp.exp(-jnp.abs(lam)))
        return (w_ref[d, 0].astype(BF16), w_ref[d, 1].astype(BF16), b_ref[d, 0:1, :], b_ref[d, 1:2, :], log_sig)

    def sweep(base, h, params, reverse):
        w_r, w_i, b_r, b_i, log_sig = params
        x = _conv_chunk(x_ref, base, n_rows, chunk, cw, cb)
        xb = x.astype(BF16)
        r = jax.nn.sigmoid(jnp.dot(xb, w_r, preferred_element_type=F32) + b_r)
        i = jax.nn.sigmoid(jnp.dot(xb, w_i, preferred_element_type=F32) + b_i)
        log_a = RGLRU_C * r * log_sig
        a = jnp.exp(log_a)
        u = jnp.sqrt(1.0 - a * a) * (i * x)
        return _lru_scan_chunk(a, u, h, reverse)

    fwd, bwd = gates(0), gates(1)

    def body(c, carry):
        hf, hb = carry
        base_f = pl.multiple_of(c * chunk, chunk)
        base_b = pl.multiple_of((n_chunks - 1 - c) * chunk, chunk)
        hs_f = sweep(base_f, hf, fwd, False)
        hs_b = sweep(base_b, hb, bwd, True)
        hf_ref[pl.ds(base_f, chunk), :] = hs_f
        hb_ref[pl.ds(base_b, chunk), :] = hs_b
        return hs_f[chunk - 1:chunk], hs_b[0:1]

    hT_ref[0:1, :], hT_ref[1:2, :] = lax.fori_loop(0, n_chunks, body, (h0_ref[0:1, :], h0_ref[1:2, :]))

    def emit(c, _):
        rows = pl.ds(pl.multiple_of(c * chunk, chunk), chunk)
        y_ref[rows, :] = ((hf_ref[rows, :] + hb_ref[rows, :]) * jax.nn.gelu(g_ref[rows, :])).astype(BF16)
        return 0

    lax.fori_loop(0, n_chunks, emit, 0)


def rglru(p, x_col0, g_col0, conv_w, conv_b, w_gates, b_gates, lam, h0):
    n_rows = p.shape[0]
    chunk = min(n_rows, LRU_CHUNK)
    xb = x_col0 // LANES
    gb = g_col0 // LANES
    seq = pltpu.VMEM((n_rows, LANES), F32)
    return pl.pallas_call(
        functools.partial(_lru_kernel, n_rows=n_rows, chunk=chunk),
        grid=(LRU_BLOCKS,),
        in_specs=[pl.BlockSpec((n_rows, LANES), lambda j: (0, xb + j)),
                  pl.BlockSpec((n_rows, LANES), lambda j: (0, gb + j)),
                  pl.BlockSpec((4, LANES), lambda j: (0, j)),
                  pl.BlockSpec((1, LANES), lambda j: (0, j)),
                  pl.BlockSpec((2, 2, None, LANES, LANES), lambda j: (0, 0, j, 0, 0)),
                  pl.BlockSpec((2, 2, LANES), lambda j: (0, 0, j)),
                  pl.BlockSpec((2, LANES), lambda j: (0, j)),
                  pl.BlockSpec((2, LANES), lambda j: (0, j))],
        out_specs=[pl.BlockSpec((n_rows, LANES), lambda j: (0, j)),
                   pl.BlockSpec((2, LANES), lambda j: (0, j))],
        out_shape=[jax.ShapeDtypeStruct((n_rows, LRU_WIDTH), BF16),
                   jax.ShapeDtypeStruct((2, LRU_WIDTH), F32)],
        scratch_shapes=[seq, seq],
        compiler_params=_cparams("parallel"),
        name="rglru",
    )(p, p, conv_w, conv_b.reshape(1, LRU_WIDTH), w_gates, b_gates, lam, h0)


SSD_CHUNKS_PER_STEP = 4


def _ssd_direction(xs, bm, cm, dt_raw, dtT_raw, bias_row, bias_col, a_row, a_col, st_ref, reverse):
    q = SSD_CHUNK
    hi = lax.Precision.HIGHEST
    li = lax.broadcasted_iota(jnp.int32, (q, q), 0)
    si = lax.broadcasted_iota(jnp.int32, (q, q), 1)
    causal = (si >= li) if reverse else (si <= li)
    tri = causal.astype(F32)
    dt = jax.nn.softplus(dt_raw + bias_row)
    dtT = jax.nn.softplus(dtT_raw + bias_col)
    adt = dt * a_row
    adtT = dtT * a_col
    cum = jnp.dot(tri, adt, preferred_element_type=F32, precision=hi)
    cumT = lax.dot_general(adtT, tri, (((1,), (1,)), ((), ())), preferred_element_type=F32,
                           precision=hi)
    edge = cum[0:1] if reverse else cum[q - 1:q]

    def lanes(v):
        return [jnp.broadcast_to(v[:, h:h + 1], (v.shape[0], LANES)) for h in range(SSM_HEADS)]

    def per_channel(cols):
        first = lax.broadcasted_iota(jnp.int32, cols[0].shape, 1) < SSM_HEAD_DIM
        return jnp.concatenate([jnp.where(first, cols[2 * j], cols[2 * j + 1]) for j in range(SSM_HEADS // 2)],
                               axis=1)

    edgeT = cumT[:, 0:1] if reverse else cumT[:, q - 1:q]
    wT = dtT * jnp.exp(edgeT - cumT)
    cum_l = lanes(cum)
    grow = jnp.exp(per_channel(cum_l))
    keep = jnp.exp(per_channel(lanes(edge)))
    xs_b = xs.astype(BF16)
    per_group = SSM_HEADS // SSM_GROUPS
    first_head = lax.broadcasted_iota(jnp.int32, (1, LANES), 1) < SSM_HEAD_DIM
    ys = []
    for g in range(SSM_GROUPS):
        gsl = slice(g * per_group * SSM_HEAD_DIM, (g + 1) * per_group * SSM_HEAD_DIM)
        b_gt = bm[:, g * SSM_D_STATE:(g + 1) * SSM_D_STATE].T
        c_g = cm[:, g * SSM_D_STATE:(g + 1) * SSM_D_STATE].astype(BF16)
        gram = jnp.dot(c_g, b_gt.astype(BF16), preferred_element_type=F32)
        st = st_ref[:, gsl]
        y_off = grow[:, gsl] * jnp.dot(c_g, st.astype(BF16), preferred_element_type=F32)
        upd = []
        for pp in range(per_group // 2):
            psl = slice(pp * LANES, (pp + 1) * LANES)
            x_p = xs_b[:, gsl][:, psl]
            y_pair, u_pair = [], []
            for h in (g * per_group + 2 * pp, g * per_group + 2 * pp + 1):
                seg = cum_l[h] - cumT[h:h + 1, :]
                decay = jnp.exp(jnp.where(causal, seg, -jnp.inf))
                m = (gram * (decay * dtT[h:h + 1, :])).astype(BF16)
                y_pair.append(jnp.dot(m, x_p, preferred_element_type=F32))
                u_pair.append(jnp.dot((b_gt * wT[h:h + 1, :]).astype(BF16), x_p, preferred_element_type=F32))
            ys.append(jnp.where(first_head, y_pair[0], y_pair[1]) + y_off[:, psl])
            upd.append(jnp.where(first_head, u_pair[0], u_pair[1]))
        st_ref[:, gsl] = keep[:, gsl] * st + jnp.concatenate(upd, axis=1)
    return jnp.concatenate(ys, axis=1)


def _ssd_kernel(xf_ref, bf_ref, cf_ref, dtf_ref, dtTf_ref, xb_ref, bb_ref, cb_ref, dtb_ref, dtTb_ref,
                bias_ref, biasT_ref, alog_ref, alogT_ref, st0_ref, yf_ref, yb_ref, stT_ref, st_ref):
    c = pl.program_id(0)

    @pl.when(c == 0)
    def _():
        st_ref[...] = st0_ref[...]

    a_row = -jnp.exp(alog_ref[...])
    a_col = -jnp.exp(alogT_ref[...])
    q = SSD_CHUNK
    n_sub = xf_ref.shape[0] // q
    for k in range(n_sub):
        rows = slice(k * q, (k + 1) * q)
        yf_ref[rows, :] = _ssd_direction(xf_ref[rows, :], bf_ref[rows, :], cf_ref[rows, :],
                                         dtf_ref[rows, 0:SSM_HEADS], dtTf_ref[0:SSM_HEADS, rows],
                                         bias_ref[0:1, :], biasT_ref[:, 0:1], a_row[0:1, :], a_col[:, 0:1],
                                         st_ref.at[0], False)
    for k in reversed(range(n_sub)):
        rows = slice(k * q, (k + 1) * q)
        yb_ref[rows, :] = _ssd_direction(xb_ref[rows, :], bb_ref[rows, :], cb_ref[rows, :],
                                         dtb_ref[rows, SSM_HEADS:2 * SSM_HEADS],
                                         dtTb_ref[SSM_HEADS:2 * SSM_HEADS, rows],
                                         bias_ref[1:2, :], biasT_ref[:, 1:2], a_row[1:2, :], a_col[:, 1:2],
                                         st_ref.at[1], True)

    @pl.when(c == pl.num_programs(0) - 1)
    def _():
        stT_ref[...] = st_ref[...]


def ssd_scan(xbc, dt, dtT, dt_bias, a_log, st0):
    n_rows = xbc.shape[0]
    q = SSD_CHUNK * min(SSD_CHUNKS_PER_STEP, n_rows // SSD_CHUNK)
    nc = n_rows // q
    fwd = lambda c: c
    bwd = lambda c: nc - 1 - c

    def specs(ix):
        return [pl.BlockSpec((q, SSM_D_INNER), lambda c: (ix(c), 0)),
                pl.BlockSpec((q, 2 * SSM_D_STATE), lambda c: (ix(c), 2)),
                pl.BlockSpec((q, 2 * SSM_D_STATE), lambda c: (ix(c), 3)),
                pl.BlockSpec((q, 2 * SSM_HEADS), lambda c: (ix(c), 0)),
                pl.BlockSpec((2 * SSM_HEADS, q), lambda c: (0, ix(c)))]

    small = lambda shape: pl.BlockSpec(shape, lambda c: (0,) * len(shape))
    st_shape = (2, SSM_D_STATE, SSM_D_INNER)
    return pl.pallas_call(
        _ssd_kernel,
        grid=(nc,),
        in_specs=specs(fwd) + specs(bwd) + [small((2, SSM_HEADS)), small((SSM_HEADS, 2)),
                                            small((2, SSM_HEADS)), small((SSM_HEADS, 2)), small(st_shape)],
        out_specs=[pl.BlockSpec((q, SSM_D_INNER), lambda c: (c, 0)),
                   pl.BlockSpec((q, SSM_D_INNER), lambda c: (nc - 1 - c, 0)),
                   small(st_shape)],
        out_shape=[jax.ShapeDtypeStruct((n_rows, SSM_D_INNER), F32),
                   jax.ShapeDtypeStruct((n_rows, SSM_D_INNER), F32),
                   jax.ShapeDtypeStruct(st_shape, F32)],
        scratch_shapes=[pltpu.VMEM(st_shape, F32)],
        compiler_params=_cparams("arbitrary"),
        name="ssd_scan",
    )(xbc, xbc, xbc, dt, dtT, xbc, xbc, xbc, dt, dtT, dt_bias, dt_bias.T, a_log, a_log.T, st0)


def _ssd_finish_kernel(yf_ref, yb_ref, xs_ref, z_ref, d_ref, nw_ref, o_ref):
    hrow = lax.broadcasted_iota(jnp.int32, (SSM_HEADS, SSM_D_INNER), 0)
    hcol = lax.broadcasted_iota(jnp.int32, (SSM_HEADS, SSM_D_INNER), 1) // SSM_HEAD_DIM
    d_e = jnp.sum(jnp.where(hrow == hcol, d_ref[...], 0.0), axis=0, keepdims=True)
    y = yf_ref[...] + yb_ref[...] + d_e * xs_ref[...]
    y = y * _silu(z_ref[...])
    o_ref[...] = (y * lax.rsqrt(jnp.mean(y * y, axis=-1, keepdims=True) + NORM_EPS) * nw_ref[...]).astype(BF16)


def ssd_finish(yf, yb, xbc, p, z_col0, d_skip, norm_w):
    n_rows = yf.shape[0]
    tm = min(n_rows, 512)
    zb = z_col0 // SSM_D_INNER
    blk = lambda cb: pl.BlockSpec((tm, SSM_D_INNER), lambda i: (i, cb))
    return pl.pallas_call(
        _ssd_finish_kernel,
        grid=(n_rows // tm,),
        in_specs=[blk(0), blk(0), blk(0), blk(zb),
                  pl.BlockSpec((SSM_HEADS, 1), lambda i: (0, 0)),
                  pl.BlockSpec((1, SSM_D_INNER), lambda i: (0, 0))],
        out_specs=blk(0),
        out_shape=jax.ShapeDtypeStruct((n_rows, SSM_D_INNER), BF16),
        compiler_params=_cparams("parallel"),
        name="ssd_finish",
    )(yf, yb, xbc, p, d_skip.reshape(SSM_HEADS, 1), norm_w.reshape(1, SSM_D_INNER))


LRU_X_COL = 2560
LRU_G_COL = 3072
SSM_Z_COL = 3584
SSM_XBC_COL = 4096


def _mixers(p, dt, qkv_q, qkv_c, qkv_l, lw, lam_init, lru_h0, ssd_st0):
    ya = diff_attention(qkv_q, qkv_c, qkv_l, lw["da_lambda"], lw["da_subln"], lam_init)
    yb = gqa_attention(qkv_q, qkv_c, qkv_l)
    yc, lru_hT = rglru(p, LRU_X_COL, LRU_G_COL, lw["lru_conv_w"], lw["lru_conv_b"], lw["lru_w_gates"],
                       lw["lru_b_gates"], lw["lru_lambda"], lru_h0)
    xbc = dwconv(p, SSM_XBC_COL, 2 * SSM_D_INNER, lw["ssm_conv_w"], lw["ssm_conv_b"], act=True)
    yf, ybk, ssd_stT = ssd_scan(xbc, dt, dt.T, lw["ssm_dt_bias"], lw["ssm_a_log"], ssd_st0)
    yd = ssd_finish(yf, ybk, xbc, p, SSM_Z_COL, lw["ssm_d"], lw["ssm_norm"])
    return [ya, yb, yc, yd], lru_hT, ssd_stT


def kernel(x, c, ctx, c_ctx, w_mod, b_mod, mix_norm, ffn_norm, w_in, w_out, da_lambda, da_subln, gqa_q_norm,
           gqa_k_norm, lru_conv_w, lru_conv_b, lru_w_gates, lru_b_gates, lru_lambda, ssm_conv_w, ssm_conv_b,
           ssm_dt_bias, ssm_a_log, ssm_d, ssm_norm, ffn_w_up, ffn_conv_w, ffn_conv_b, ffn_w_down, final_norm):
    depth = w_mod.shape[0]
    d = x.shape[-1]
    xl = x[0]
    xc = ctx[0]
    n_lat, n_ctx = xl.shape[0], xc.shape[0]

    cond8 = jnp.zeros((8, d), F32).at[0].set(c[0]).at[1].set(c_ctx)
    mods = modulation_all(cond8, w_mod, b_mod)
    tabs_l = _rope_tables(n_lat, DA_HEAD_DIM) + _rope_tables(n_lat, GQA_HEAD_DIM)
    tabs_c = _rope_tables(n_ctx, DA_HEAD_DIM) + _rope_tables(n_ctx, GQA_HEAD_DIM)
    lru_zero = jnp.zeros((2, LRU_WIDTH), F32)
    ssd_zero = jnp.zeros((2, SSM_D_STATE, SSM_D_INNER), F32)
    w_in_t = jnp.swapaxes(w_in, 1, 2)

    for layer in range(depth):
        need_ctx = layer < depth - 1
        lam_init = 0.8 - 0.6 * math.exp(-0.3 * layer)
        ml = [mods[layer, 0:1, k * d:(k + 1) * d] for k in range(6)]
        mc = [mods[layer, 1:2, k * d:(k + 1) * d] for k in range(6)]
        lw = dict(da_lambda=da_lambda[layer], da_subln=da_subln[layer],
                  lru_conv_w=lru_conv_w[layer], lru_conv_b=lru_conv_b[layer], lru_w_gates=lru_w_gates[layer],
                  lru_b_gates=lru_b_gates[layer], lru_lambda=lru_lambda[layer], ssm_conv_w=ssm_conv_w[layer],
                  ssm_conv_b=ssm_conv_b[layer], ssm_dt_bias=ssm_dt_bias[layer], ssm_a_log=ssm_a_log[layer],
                  ssm_d=ssm_d[layer], ssm_norm=ssm_norm[layer])
        nw_m = mix_norm[layer].reshape(1, d)
        nw_f = ffn_norm[layer].reshape(1, d)

        xn_c = adaln(xc, nw_m, mc[0], mc[1])
        xn_l = adaln(xl, nw_m, ml[0], ml[1])
        p_c, dt_c = in_proj(xn_c, w_in_t, layer), dt_proj(xn_c, w_in_t, layer)
        p_l, dt_l = in_proj(xn_l, w_in_t, layer), dt_proj(xn_l, w_in_t, layer)
        qkv_c = attn_prep(p_c, tabs_c, gqa_q_norm[layer], gqa_k_norm[layer], rope=False)
        qkv_l = attn_prep(p_l, tabs_l, gqa_q_norm[layer], gqa_k_norm[layer], rope=True)

        y_c, lru_h, ssd_st = _mixers(p_c, dt_c, qkv_c, qkv_c, None, lw, lam_init, lru_zero, ssd_zero)
        y_l, _, _ = _mixers(p_l, dt_l, qkv_l, qkv_c, qkv_l, lw, lam_init, lru_h, ssd_st)

        xl = resid_proj(y_l, w_out, layer, xl, ml[2])
        act = ffn_up(adaln(xl, nw_f, ml[3], ml[4]), ffn_w_up, layer, ffn_conv_w[layer], ffn_conv_b[layer])
        xl = resid_proj([act], ffn_w_down, layer, xl, ml[5])
        if need_ctx:
            xc = resid_proj(y_c, w_out, layer, xc, mc[2])
            act = ffn_up(adaln(xc, nw_f, mc[3], mc[4]), ffn_w_up, layer, ffn_conv_w[layer], ffn_conv_b[layer])
            xc = resid_proj([act], ffn_w_down, layer, xc, mc[5])

    return final_rmsnorm(xl, final_norm)[None]
```

```python
import functools
import math

import jax
import jax.numpy as jnp
from jax import lax
from jax.experimental import pallas as pl
from jax.experimental.pallas import tpu as pltpu

F32 = jnp.float32
BF16 = jnp.bfloat16

GRID_W = 64
GROUP_WIDTH = 512
DA_HEAD_DIM = 64
DA_HEADS = 4
GQA_HEAD_DIM = 128
GQA_HEADS = 4
GQA_KV_HEADS = 2
LRU_WIDTH = 512
LRU_BLOCKS = 4
RGLRU_C = 8.0
SSM_D_INNER = 512
SSM_HEAD_DIM = 64
SSM_HEADS = 8
SSM_GROUPS = 2
SSM_D_STATE = 128
SSD_CHUNK = 128
D_FF = 5632
ROPE_THETA = 10000.0
NORM_EPS = 1e-6
LOG2E = math.log2(math.e)

LANES = 128
SUBLANES = 8
HALO = 16
IN_COLS = 5136
IN_MAIN_COLS = 5120
N_DT = IN_COLS - IN_MAIN_COLS
ATTN_COLS = 2560
QA_BLK, QB_BLK, KD_BLK, VD_BLK = 0, 4, 8, 12
QG_BLK, KG_BLK, VG_BLK = 20, 24, 26
QKV_COLS = 30 * LANES
P_QD, P_KD, P_VD, P_QG, P_KG, P_VG = 0, 4, 8, 12, 16, 18

VMEM_LIMIT = 56 * 1024 * 1024
PROJ_TM = 1024
IN_TN = 1024
RESID_TM = 512
RESID_TN_MAX = 1024
RESID_W_ELEMS = 2048 * 1024
ROWWISE_TM = 512
FFN_TN = 512


def _cparams(*sem):
    return pltpu.CompilerParams(dimension_semantics=sem, vmem_limit_bytes=VMEM_LIMIT)


def _silu(x):
    return x * jax.nn.sigmoid(x)


def _norm_mod(x, nw, shift, scale):
    y = x * lax.rsqrt(jnp.mean(x * x, axis=-1, keepdims=True) + NORM_EPS) * nw
    return y * (1.0 + scale) + shift


def _mod_kernel(c_ref, w_ref, b_ref, o_ref):
    s = _silu(c_ref[...]).astype(BF16)
    o_ref[...] = jnp.dot(s, w_ref[...].astype(BF16), preferred_element_type=F32) + b_ref[...]


def modulation_all(cond8, w_mod, b_mod):
    depth, d, n = w_mod.shape
    tn = IN_TN
    return pl.pallas_call(
        _mod_kernel,
        grid=(depth, n // tn),
        in_specs=[
            pl.BlockSpec((8, d), lambda l, j: (0, 0)),
            pl.BlockSpec((None, d, tn), lambda l, j: (l, 0, j)),
            pl.BlockSpec((None, 1, tn), lambda l, j: (l, 0, j)),
        ],
        out_specs=pl.BlockSpec((None, 8, tn), lambda l, j: (l, 0, j)),
        out_shape=jax.ShapeDtypeStruct((depth, 8, n), F32),
        compiler_params=_cparams("parallel", "parallel"),
        name="modulation",
    )(cond8, w_mod, b_mod.reshape(depth, 1, n))


ADALN_ROWS = 16
_NT = (((1,), (1,)), ((), ()))


def _adaln_kernel(*refs, with_dt):
    if with_dt:
        x_ref, nw_ref, sh_ref, sc_ref, wdt_ref, o_ref, dt_ref = refs
    else:
        x_ref, nw_ref, sh_ref, sc_ref, o_ref = refs
    gain = nw_ref[...] * (1.0 + sc_ref[...])
    shift = sh_ref[...]
    d = x_ref.shape[1]

    def body(r, _):
        rows = pl.ds(pl.multiple_of(r * ADALN_ROWS, ADALN_ROWS), ADALN_ROWS)
        x = x_ref[rows, :]
        inv = lax.rsqrt(jnp.sum(x * x, axis=-1, keepdims=True) * (1.0 / d) + NORM_EPS)
        o_ref[rows, :] = (x * inv * gain + shift).astype(BF16)
        return 0

    lax.fori_loop(0, x_ref.shape[0] // ADALN_ROWS, body, 0, unroll=4)
    if with_dt:
        dt_ref[...] = lax.dot_general(o_ref[...], wdt_ref[...].astype(BF16), _NT, preferred_element_type=F32)


def adaln(x, nw, shift, scale, w_in_t=None, layer=None):
    m, d = x.shape
    tm = min(m, PROJ_TM)
    with_dt = w_in_t is not None
    vec = pl.BlockSpec((1, d), lambda i: (0, 0))
    rows = pl.BlockSpec((tm, d), lambda i: (i, 0))
    in_specs, args = [rows, vec, vec, vec], [x, nw, shift, scale]
    out_specs, out_shape = rows, jax.ShapeDtypeStruct((m, d), BF16)
    if with_dt:
        in_specs.append(pl.BlockSpec((None, N_DT, d), lambda i: (layer, IN_MAIN_COLS // N_DT, 0)))
        args.append(w_in_t)
        out_specs = [rows, pl.BlockSpec((tm, N_DT), lambda i: (i, 0))]
        out_shape = [out_shape, jax.ShapeDtypeStruct((m, N_DT), F32)]
    return pl.pallas_call(
        functools.partial(_adaln_kernel, with_dt=with_dt),
        grid=(m // tm,),
        in_specs=in_specs,
        out_specs=out_specs,
        out_shape=out_shape,
        compiler_params=_cparams("parallel"),
        name="adaln",
    )(*args)


def _inproj_kernel(xn_ref, w_ref, o_ref, wb_ref):
    @pl.when(pl.program_id(1) == 0)
    def _():
        wb_ref[...] = w_ref[...].astype(BF16)

    o_ref[...] = lax.dot_general(xn_ref[...], wb_ref[...], _NT, preferred_element_type=F32)


def in_proj(xn, w_in_t, layer):
    m, d = xn.shape
    tm = min(m, PROJ_TM)
    return pl.pallas_call(
        _inproj_kernel,
        grid=(IN_MAIN_COLS // IN_TN, m // tm),
        in_specs=[pl.BlockSpec((tm, d), lambda j, i: (i, 0)),
                  pl.BlockSpec((None, IN_TN, d), lambda j, i: (layer, j, 0))],
        out_specs=pl.BlockSpec((tm, IN_TN), lambda j, i: (i, j)),
        out_shape=jax.ShapeDtypeStruct((m, IN_MAIN_COLS), F32),
        scratch_shapes=[pltpu.VMEM((IN_TN, d), BF16)],
        compiler_params=_cparams("parallel", "arbitrary"),
        name="in_proj",
    )(xn, w_in_t)


def _resid_kernel(*refs, n_y):
    y_refs = refs[:n_y]
    w_ref, x_ref, g_ref, o_ref, wb_ref = refs[n_y:]

    @pl.when(pl.program_id(1) == 0)
    def _():
        wb_ref[...] = w_ref[...].astype(BF16)

    kk = wb_ref.shape[0] // n_y
    acc = functools.reduce(jnp.add, [
        jnp.dot(y_refs[a][...], wb_ref[a * kk:(a + 1) * kk, :], preferred_element_type=F32) for a in range(n_y)])
    o_ref[...] = x_ref[...] + g_ref[...] * acc


def resid_proj(ys, w_all, layer, x, gate):
    m = x.shape[0]
    _, k, n = w_all.shape
    kk = k // len(ys)
    tm = min(m, RESID_TM)
    tn = RESID_TN_MAX if k * RESID_TN_MAX <= RESID_W_ELEMS else RESID_TN_MAX // 2
    return pl.pallas_call(
        functools.partial(_resid_kernel, n_y=len(ys)),
        grid=(n // tn, m // tm),
        in_specs=[pl.BlockSpec((tm, kk), lambda j, i: (i, 0)) for _ in ys]
                 + [pl.BlockSpec((None, k, tn), lambda j, i: (layer, 0, j)),
                    pl.BlockSpec((tm, tn), lambda j, i: (i, j)),
                    pl.BlockSpec((1, tn), lambda j, i: (0, j))],
        out_specs=pl.BlockSpec((tm, tn), lambda j, i: (i, j)),
        out_shape=jax.ShapeDtypeStruct((m, n), F32),
        scratch_shapes=[pltpu.VMEM((k, tn), BF16)],
        compiler_params=_cparams("parallel", "arbitrary"),
        name="resid_proj",
    )(*ys, w_all, x, gate)


def _ffn_up_kernel(xp_ref, x_ref, xx_ref, wg_ref, wu_ref, cw_ref, cb_ref, o_ref, xn_ref, g_ref, wb_ref,
                   *, tm, n_row_blocks):
    i = pl.program_id(1)

    @pl.when(i == 0)
    def _():
        wb_ref[0] = wg_ref[...].astype(BF16)
        wb_ref[1] = wu_ref[...].astype(BF16)

    xn_ref[pl.ds(0, HALO), :] = jnp.where(i == 0, jnp.zeros_like(xp_ref), xp_ref[...])
    xn_ref[pl.ds(HALO, tm), :] = x_ref[...]
    xn_ref[pl.ds(HALO + tm, HALO), :] = jnp.where(i == n_row_blocks - 1, jnp.zeros_like(xx_ref), xx_ref[...])
    g_ref[...] = jnp.dot(xn_ref[...], wb_ref[0], preferred_element_type=F32)
    up = jnp.dot(x_ref[...], wb_ref[1], preferred_element_type=F32)
    cw = cw_ref[...]
    gc = (cb_ref[...] + cw[0:1] * g_ref[pl.ds(HALO - 1, tm), :] + cw[1:2] * g_ref[pl.ds(HALO, tm), :]
          + cw[2:3] * g_ref[pl.ds(HALO + 1, tm), :])
    o_ref[...] = (_silu(gc) * up).astype(BF16)


def ffn_up(xn, w_up_all, layer, conv_w, conv_b):
    m, d = xn.shape
    dff = w_up_all.shape[2] // 2
    tm = min(m, PROJ_TM)
    tn = FFN_TN
    nrb = m // tm
    hb = tm // HALO
    nhb = m // HALO
    return pl.pallas_call(
        functools.partial(_ffn_up_kernel, tm=tm, n_row_blocks=nrb),
        grid=(dff // tn, nrb),
        in_specs=[pl.BlockSpec((HALO, d), lambda j, i: (jnp.maximum(i * hb - 1, 0), 0)),
                  pl.BlockSpec((tm, d), lambda j, i: (i, 0)),
                  pl.BlockSpec((HALO, d), lambda j, i: (jnp.minimum((i + 1) * hb, nhb - 1), 0)),
                  pl.BlockSpec((None, d, tn), lambda j, i: (layer, 0, j)),
                  pl.BlockSpec((None, d, tn), lambda j, i: (layer, 0, j + dff // tn)),
                  pl.BlockSpec((3, tn), lambda j, i: (0, j)),
                  pl.BlockSpec((1, tn), lambda j, i: (0, j))],
        out_specs=pl.BlockSpec((tm, tn), lambda j, i: (i, j)),
        out_shape=jax.ShapeDtypeStruct((m, dff), BF16),
        scratch_shapes=[pltpu.VMEM((tm + 2 * HALO, d), BF16), pltpu.VMEM((tm + 2 * HALO, tn), F32),
                        pltpu.VMEM((2, d, tn), BF16)],
        compiler_params=_cparams("parallel", "arbitrary"),
        name="ffn_up",
    )(xn, xn, xn, w_up_all, w_up_all, conv_w, conv_b.reshape(1, dff))


def _rmsnorm_kernel(x_ref, w_ref, o_ref):
    x = x_ref[...]
    o_ref[...] = x * lax.rsqrt(jnp.mean(x * x, axis=-1, keepdims=True) + NORM_EPS) * w_ref[...]


def final_rmsnorm(x, w):
    m, d = x.shape
    tm = min(m, ROWWISE_TM)
    return pl.pallas_call(
        _rmsnorm_kernel,
        grid=(m // tm,),
        in_specs=[pl.BlockSpec((tm, d), lambda i: (i, 0)), pl.BlockSpec((1, d), lambda i: (0, 0))],
        out_specs=pl.BlockSpec((tm, d), lambda i: (i, 0)),
        out_shape=jax.ShapeDtypeStruct((m, d), F32),
        compiler_params=_cparams("parallel"),
        name="final_norm",
    )(x, w.reshape(1, d))


def _rope_tables(n_tok, head_dim):
    q = head_dim // 4
    lane = jnp.arange(LANES, dtype=jnp.int32)
    u = lane % head_dim
    region = u // q
    freqs = jnp.power(ROPE_THETA, -(u % q).astype(F32) / q)[None, :]
    ang_r = jnp.arange(n_tok // GRID_W, dtype=jnp.int32).astype(F32)[:, None] * freqs
    ang_c = jnp.arange(GRID_W, dtype=jnp.int32).astype(F32)[:, None] * freqs
    by_row = (region < 2)[None, None, :]

    def per_token(f):
        return jnp.where(by_row, f(ang_r)[:, None, :], f(ang_c)[None, :, :]).reshape(n_tok, LANES)

    cos, sin = per_token(jnp.cos), per_token(jnp.sin)
    first = (region % 2 == 0)[None, :]
    return cos, jnp.where(first, -sin, 0.0), jnp.where(first, 0.0, sin)


def _prep_kernel(p_ref, cd_ref, ad_ref, bd_ref, cg_ref, ag_ref, bg_ref, qn_ref, kn_ref, o_ref, *, rope):
    lane = lax.broadcasted_iota(jnp.int32, (1, LANES), 1)
    lo = lane < DA_HEAD_DIM

    def blk(b):
        return p_ref[:, b * LANES:(b + 1) * LANES]

    def put(b, v):
        o_ref[:, b * LANES:(b + 1) * LANES] = v.astype(BF16)

    def rot(x, cos, sa, sb, quarter):
        if not rope:
            return x
        return (x * cos[...] + pltpu.roll(x, LANES - quarter, 1) * sa[...]
                + pltpu.roll(x, quarter, 1) * sb[...])

    def rms(x, w):
        return x * lax.rsqrt(jnp.mean(x * x, axis=-1, keepdims=True) + NORM_EPS) * w[...]

    qd = DA_HEAD_DIM // 4
    qg = GQA_HEAD_DIM // 4
    ones = jnp.ones((p_ref.shape[0], LANES), F32)
    for h in range(DA_HEADS):
        q = rot(blk(P_QD + h), cd_ref, ad_ref, bd_ref, qd) * (DA_HEAD_DIM ** -0.5 * LOG2E)
        put(QA_BLK + h, jnp.where(lo, q, 0.0))
        put(QB_BLK + h, jnp.where(lo, 0.0, q))
        put(KD_BLK + h, rot(blk(P_KD + h), cd_ref, ad_ref, bd_ref, qd))
        put(VD_BLK + 2 * h, blk(P_VD + h))
        put(VD_BLK + 2 * h + 1, ones)
    for h in range(GQA_HEADS):
        q = rot(rms(blk(P_QG + h), qn_ref), cg_ref, ag_ref, bg_ref, qg) * (GQA_HEAD_DIM ** -0.5 * LOG2E)
        put(QG_BLK + h, q)
    for h in range(GQA_KV_HEADS):
        put(KG_BLK + h, rot(rms(blk(P_KG + h), kn_ref), cg_ref, ag_ref, bg_ref, qg))
        put(VG_BLK + 2 * h, blk(P_VG + h))
        put(VG_BLK + 2 * h + 1, ones)


def attn_prep(p, tables, q_norm, k_norm, rope):
    m = p.shape[0]
    tm = min(m, ROWWISE_TM)
    tab = pl.BlockSpec((tm, LANES), lambda i: (i, 0))
    vec = pl.BlockSpec((1, LANES), lambda i: (0, 0))
    return pl.pallas_call(
        functools.partial(_prep_kernel, rope=rope),
        grid=(m // tm,),
        in_specs=[pl.BlockSpec((tm, ATTN_COLS), lambda i: (i, 0))] + [tab] * 6 + [vec, vec],
        out_specs=pl.BlockSpec((tm, QKV_COLS), lambda i: (i, 0)),
        out_shape=jax.ShapeDtypeStruct((m, QKV_COLS), BF16),
        compiler_params=_cparams("parallel"),
        name="attn_prep",
    )(p, *tables, q_norm.reshape(1, LANES), k_norm.reshape(1, LANES))


def _attn_kernel(*refs, tk, n_kv, diff, lam_init):
    refs = list(refs)
    qa_ref, qb_ref, kc_ref, vc_ref = refs[:4]
    del refs[:4]
    if n_kv:
        kl_ref, vl_ref = refs[:2]
        del refs[:2]
    if diff:
        lam_ref, sw_ref = refs[:2]
        del refs[:2]
    o_ref = refs.pop(0)
    if n_kv:
        s_ref = refs.pop(0)
    sc_ref, m_ref, acc_ref = refs
    tq = qa_ref.shape[0]
    nt = (((1,), (1,)), ((), ()))
    q2 = jnp.concatenate([qa_ref[...], qb_ref[...]], axis=0)

    def lane_tiles(x):
        return [x[:, t * LANES:(t + 1) * LANES] for t in range(x.shape[1] // LANES)]

    def key_rows(ref, c):
        return ref[pl.ds(pl.multiple_of(c * tk, tk), tk), :]

    def chunks(body):
        if n_kv:
            lax.fori_loop(0, n_kv, lambda c, _: body(c) or 0, 0)

    sc = lax.dot_general(q2, kc_ref[...], nt, preferred_element_type=F32)
    sc_ref[...] = sc
    m_ref[...] = functools.reduce(jnp.maximum, lane_tiles(sc))

    def score_chunk(c):
        s = lax.dot_general(q2, key_rows(kl_ref, c), nt, preferred_element_type=F32)
        s_ref[c] = s
        m_ref[...] = functools.reduce(jnp.maximum, lane_tiles(s), m_ref[...])

    chunks(score_chunk)
    m_ref[...] = jnp.broadcast_to(jnp.max(m_ref[...], axis=-1, keepdims=True), (2 * tq, LANES))

    def weigh(s, vs):
        m = m_ref[...]
        p = jnp.concatenate([jnp.exp2(st - m).astype(BF16) for st in lane_tiles(s)], axis=1)
        return jnp.dot(p, vs, preferred_element_type=F32)

    acc_ref[...] = weigh(sc_ref[...], vc_ref[...])

    def weigh_chunk(c):
        acc_ref[...] += weigh(s_ref[c], key_rows(vl_ref, c))

    chunks(weigh_chunk)
    o = acc_ref[:, 0:LANES] / acc_ref[:, LANES:2 * LANES]
    oa = o[0:tq]
    ob = o[tq:2 * tq]
    if diff:
        lv = lam_ref[...]
        lam = (jnp.exp(jnp.sum(lv[0:1] * lv[1:2], keepdims=True))
               - jnp.exp(jnp.sum(lv[2:3] * lv[3:4], keepdims=True)) + lam_init)
        o = oa - lam * ob
        o = o * lax.rsqrt(jnp.mean(o * o, axis=-1, keepdims=True) + NORM_EPS) * sw_ref[...]
        o_ref[...] = (o * (1.0 - lam_init)).astype(BF16)
    else:
        o_ref[:, 0:LANES] = oa.astype(BF16)
        o_ref[:, LANES:2 * LANES] = ob.astype(BF16)


ATTN_TQ = 512
ATTN_TK = (4096, 2048, 1024, 512, 256, 128)


def _attention(qkv_q, qkv_c, qkv_l, extra, extra_specs, cols, n_heads, out_width, name, **static):
    qa_col, qb_col, k_col, v_col = cols
    sq, n_ctx = qkv_q.shape[0], qkv_c.shape[0]
    tq = min(sq, ATTN_TQ)
    n_lat = 0 if qkv_l is None else qkv_l.shape[0]
    tk = next((t for t in ATTN_TK if n_lat and n_lat % t == 0), 0)
    n_kv = n_lat // tk if n_lat else 0

    def kv_specs(rows):
        return [pl.BlockSpec((rows, LANES), lambda h, i: (0, k_col(h))),
                pl.BlockSpec((rows, 2 * LANES), lambda h, i: (0, v_col(h)))]

    in_specs = [pl.BlockSpec((tq, LANES), lambda h, i: (i, qa_col(h))),
                pl.BlockSpec((tq, LANES), lambda h, i: (i, qb_col(h)))] + kv_specs(n_ctx)
    args = [qkv_q, qkv_q, qkv_c, qkv_c]
    scratch = [pltpu.VMEM((2 * tq, n_ctx), F32), pltpu.VMEM((2 * tq, LANES), F32),
               pltpu.VMEM((2 * tq, 2 * LANES), F32)]
    if n_kv:
        in_specs += kv_specs(n_lat)
        args += [qkv_l, qkv_l]
        scratch = [pltpu.VMEM((n_kv, 2 * tq, tk), F32)] + scratch
    return pl.pallas_call(
        functools.partial(_attn_kernel, tk=tk, n_kv=n_kv, **static),
        grid=(n_heads, sq // tq),
        in_specs=in_specs + extra_specs,
        out_specs=pl.BlockSpec((tq, out_width), lambda h, i: (i, h)),
        out_shape=jax.ShapeDtypeStruct((sq, GROUP_WIDTH), BF16),
        scratch_shapes=scratch,
        compiler_params=_cparams("parallel", "parallel"),
        name=name,
    )(*args, *extra)


def diff_attention(qkv_q, qkv_c, qkv_l, da_lambda, subln_w, lam_init):
    cols = (lambda h: QA_BLK + h, lambda h: QB_BLK + h, lambda h: KD_BLK + h, lambda h: VD_BLK // 2 + h)
    extra_specs = [pl.BlockSpec((4, DA_HEAD_DIM), lambda h, i: (0, 0)), pl.BlockSpec((1, LANES), lambda h, i: (0, 0))]
    return _attention(qkv_q, qkv_c, qkv_l, [da_lambda, subln_w.reshape(1, LANES)], extra_specs, cols,
                      DA_HEADS, LANES, "diff_attn", diff=True, lam_init=lam_init)


def gqa_attention(qkv_q, qkv_c, qkv_l):
    cols = (lambda h: QG_BLK + 2 * h, lambda h: QG_BLK + 2 * h + 1, lambda h: KG_BLK + h,
            lambda h: VG_BLK // 2 + h)
    return _attention(qkv_q, qkv_c, qkv_l, [], [], cols, GQA_KV_HEADS, 2 * LANES, "gqa_attn",
                      diff=False, lam_init=0.0)


CONV_CHUNK = 256


def _conv_chunk(x_ref, base, n_rows, chunk, cw, cb):
    cur = x_ref[pl.ds(base, chunk), :]
    prev = x_ref[pl.ds(pl.multiple_of(jnp.maximum(base - 8, 0), 8), 8), :]
    nxt = x_ref[pl.ds(pl.multiple_of(jnp.minimum(base + chunk, n_rows - 8), 8), 8), :]
    prev = jnp.where(base == 0, 0.0, prev)
    nxt = jnp.where(base + chunk >= n_rows, 0.0, nxt)
    cat = jnp.concatenate([prev, cur, nxt], axis=0)
    n = chunk + 16
    xm2 = pltpu.roll(cat, 2, 0)[8:8 + chunk]
    xm1 = pltpu.roll(cat, 1, 0)[8:8 + chunk]
    xp1 = pltpu.roll(cat, n - 1, 0)[8:8 + chunk]
    return cb + cw[0:1] * xm2 + cw[1:2] * xm1 + cw[2:3] * cur + cw[3:4] * xp1


def _dwconv_kernel(x_ref, w_ref, b_ref, o_ref, *, n_rows, chunk, act):
    cw = w_ref[...]
    cb = b_ref[...]

    def body(c, _):
        base = pl.multiple_of(c * chunk, chunk)
        y = _conv_chunk(x_ref, base, n_rows, chunk, cw, cb)
        o_ref[pl.ds(base, chunk), :] = _silu(y) if act else y
        return 0

    lax.fori_loop(0, n_rows // chunk, body, 0)


def dwconv(p, col0, width, conv_w, conv_b, act):
    n_rows = p.shape[0]
    chunk = min(n_rows, CONV_CHUNK)
    b0 = col0 // LANES
    return pl.pallas_call(
        functools.partial(_dwconv_kernel, n_rows=n_rows, chunk=chunk, act=act),
        grid=(width // LANES,),
        in_specs=[pl.BlockSpec((n_rows, LANES), lambda j: (0, b0 + j)),
                  pl.BlockSpec((4, LANES), lambda j: (0, j)),
                  pl.BlockSpec((1, LANES), lambda j: (0, j))],
        out_specs=pl.BlockSpec((n_rows, LANES), lambda j: (0, j)),
        out_shape=jax.ShapeDtypeStruct((n_rows, width), F32),
        compiler_params=_cparams("parallel"),
        name="dwconv",
    )(p, conv_w, conv_b.reshape(1, width))


LRU_CHUNK = 256


def _lru_scan_chunk(a, u, h_in, reverse):
    n = a.shape[0]
    row = lax.broadcasted_iota(jnp.int32, (n, 1), 0)
    for k in (1, 2, 4):
        if reverse:
            keep = row < n - k
            a_s = jnp.where(keep, pltpu.roll(a, n - k, 0), 1.0)
            u_s = jnp.where(keep, pltpu.roll(u, n - k, 0), 0.0)
        else:
            keep = row >= k
            a_s = jnp.where(keep, pltpu.roll(a, k, 0), 1.0)
            u_s = jnp.where(keep, pltpu.roll(u, k, 0), 0.0)
        u = u + a * u_s
        a = a * a_s
    n_tiles = n // SUBLANES
    order = range(n_tiles - 1, -1, -1) if reverse else range(n_tiles)
    tiles = [None] * n_tiles
    h = h_in
    for i in order:
        sl = slice(i * SUBLANES, (i + 1) * SUBLANES)
        h = u[sl] + a[sl] * h
        tiles[i] = h
    return jnp.concatenate(tiles, axis=0)


def _lru_kernel(x_ref, g_ref, cw_ref, cb_ref, w_ref, b_ref, lam_ref, h0_ref, y_ref, hT_ref, hf_ref, hb_ref,
                *, n_rows, chunk):
    n_chunks = n_rows // chunk
    cw = cw_ref[...]
    cb = cb_ref[...]

    def gates(d):
        lam = lam_ref[d:d + 1, :]
        log_sig = jnp.minimum(lam, 0.0) - jnp.log1p(jnp.exp(-jnp.abs(lam)))
        return (w_ref[d, 0].astype(BF16), w_ref[d, 1].astype(BF16), b_ref[d, 0:1, :], b_ref[d, 1:2, :], log_sig)

    def sweep(base, h, params, reverse):
        w_r, w_i, b_r, b_i, log_sig = params
        x = _conv_chunk(x_ref, base, n_rows, chunk, cw, cb)
        xb = x.astype(BF16)
        r = jax.nn.sigmoid(jnp.dot(xb, w_r, preferred_element_type=F32) + b_r)
        i = jax.nn.sigmoid(jnp.dot(xb, w_i, preferred_element_type=F32) + b_i)
        log_a = RGLRU_C * r * log_sig
        a = jnp.exp(log_a)
        u = jnp.sqrt(1.0 - a * a) * (i * x)
        return _lru_scan_chunk(a, u, h, reverse)

    fwd, bwd = gates(0), gates(1)

    def body(c, carry):
        hf, hb = carry
        base_f = pl.multiple_of(c * chunk, chunk)
        base_b = pl.multiple_of((n_chunks - 1 - c) * chunk, chunk)
        hs_f = sweep(base_f, hf, fwd, False)
        hs_b = sweep(base_b, hb, bwd, True)
        hf_ref[pl.ds(base_f, chunk), :] = hs_f
        hb_ref[pl.ds(base_b, chunk), :] = hs_b
        return hs_f[chunk - 1:chunk], hs_b[0:1]

    hT_ref[0:1, :], hT_ref[1:2, :] = lax.fori_loop(0, n_chunks, body, (h0_ref[0:1, :], h0_ref[1:2, :]))

    def emit(c, _):
        rows = pl.ds(pl.multiple_of(c * chunk, chunk), chunk)
        y_ref[rows, :] = ((hf_ref[rows, :] + hb_ref[rows, :]) * jax.nn.gelu(g_ref[rows, :])).astype(BF16)
        return 0

    lax.fori_loop(0, n_chunks, emit, 0)


def rglru(p, x_col0, g_col0, conv_w, conv_b, w_gates, b_gates, lam, h0):
    n_rows = p.shape[0]
    chunk = min(n_rows, LRU_CHUNK)
    xb = x_col0 // LANES
    gb = g_col0 // LANES
    seq = pltpu.VMEM((n_rows, LANES), F32)
    return pl.pallas_call(
        functools.partial(_lru_kernel, n_rows=n_rows, chunk=chunk),
        grid=(LRU_BLOCKS,),
        in_specs=[pl.BlockSpec((n_rows, LANES), lambda j: (0, xb + j)),
                  pl.BlockSpec((n_rows, LANES), lambda j: (0, gb + j)),
                  pl.BlockSpec((4, LANES), lambda j: (0, j)),
                  pl.BlockSpec((1, LANES), lambda j: (0, j)),
                  pl.BlockSpec((2, 2, None, LANES, LANES), lambda j: (0, 0, j, 0, 0)),
                  pl.BlockSpec((2, 2, LANES), lambda j: (0, 0, j)),
                  pl.BlockSpec((2, LANES), lambda j: (0, j)),
                  pl.BlockSpec((2, LANES), lambda j: (0, j))],
        out_specs=[pl.BlockSpec((n_rows, LANES), lambda j: (0, j)),
                   pl.BlockSpec((2, LANES), lambda j: (0, j))],
        out_shape=[jax.ShapeDtypeStruct((n_rows, LRU_WIDTH), BF16),
                   jax.ShapeDtypeStruct((2, LRU_WIDTH), F32)],
        scratch_shapes=[seq, seq],
        compiler_params=_cparams("parallel"),
        name="rglru",
    )(p, p, conv_w, conv_b.reshape(1, LRU_WIDTH), w_gates, b_gates, lam, h0)


SSD_CHUNKS_PER_STEP = 4


def _ssd_direction(xs, bm, cm, dt_raw, dtT_raw, bias_row, bias_col, a_row, a_col, st_ref, reverse):
    q = SSD_CHUNK
    hi = lax.Precision.HIGHEST
    li = lax.broadcasted_iota(jnp.int32, (q, q), 0)
    si = lax.broadcasted_iota(jnp.int32, (q, q), 1)
    causal = (si >= li) if reverse else (si <= li)
    tri = causal.astype(F32)
    dt = jax.nn.softplus(dt_raw + bias_row)
    dtT = jax.nn.softplus(dtT_raw + bias_col)
    adt = dt * a_row
    adtT = dtT * a_col
    cum = jnp.dot(tri, adt, preferred_element_type=F32, precision=hi)
    cumT = lax.dot_general(adtT, tri, (((1,), (1,)), ((), ())), preferred_element_type=F32,
                           precision=hi)
    edge = cum[0:1] if reverse else cum[q - 1:q]

    def lanes(v):
        return [jnp.broadcast_to(v[:, h:h + 1], (v.shape[0], LANES)) for h in range(SSM_HEADS)]

    def per_channel(cols):
        first = lax.broadcasted_iota(jnp.int32, cols[0].shape, 1) < SSM_HEAD_DIM
        return jnp.concatenate([jnp.where(first, cols[2 * j], cols[2 * j + 1]) for j in range(SSM_HEADS // 2)],
                               axis=1)

    edgeT = cumT[:, 0:1] if reverse else cumT[:, q - 1:q]
    wT = dtT * jnp.exp(edgeT - cumT)
    cum_l = lanes(cum)
    grow = jnp.exp(per_channel(cum_l))
    keep = jnp.exp(per_channel(lanes(edge)))
    xs_b = xs.astype(BF16)
    per_group = SSM_HEADS // SSM_GROUPS
    first_head = lax.broadcasted_iota(jnp.int32, (1, LANES), 1) < SSM_HEAD_DIM
    ys = []
    for g in range(SSM_GROUPS):
        gsl = slice(g * per_group * SSM_HEAD_DIM, (g + 1) * per_group * SSM_HEAD_DIM)
        b_gt = bm[:, g * SSM_D_STATE:(g + 1) * SSM_D_STATE].T
        c_g = cm[:, g * SSM_D_STATE:(g + 1) * SSM_D_STATE].astype(BF16)
        gram = jnp.dot(c_g, b_gt.astype(BF16), preferred_element_type=F32)
        st = st_ref[:, gsl]
        y_off = grow[:, gsl] * jnp.dot(c_g, st.astype(BF16), preferred_element_type=F32)
        upd = []
        for pp in range(per_group // 2):
            psl = slice(pp * LANES, (pp + 1) * LANES)
            x_p = xs_b[:, gsl][:, psl]
            y_pair, u_pair = [], []
            for h in (g * per_group + 2 * pp, g * per_group + 2 * pp + 1):
                seg = cum_l[h] - cumT[h:h + 1, :]
                decay = jnp.exp(jnp.where(causal, seg, -jnp.inf))
                m = (gram * (decay * dtT[h:h + 1, :])).astype(BF16)
                y_pair.append(jnp.dot(m, x_p, preferred_element_type=F32))
                u_pair.append(jnp.dot((b_gt * wT[h:h + 1, :]).astype(BF16), x_p, preferred_element_type=F32))
            ys.append(jnp.where(first_head, y_pair[0], y_pair[1]) + y_off[:, psl])
            upd.append(jnp.where(first_head, u_pair[0], u_pair[1]))
        st_ref[:, gsl] = keep[:, gsl] * st + jnp.concatenate(upd, axis=1)
    return jnp.concatenate(ys, axis=1)


def _ssd_kernel(xf_ref, bf_ref, cf_ref, dtf_ref, dtTf_ref, xb_ref, bb_ref, cb_ref, dtb_ref, dtTb_ref,
                bias_ref, biasT_ref, alog_ref, alogT_ref, st0_ref, yf_ref, yb_ref, stT_ref, st_ref):
    c = pl.program_id(0)

    @pl.when(c == 0)
    def _():
        st_ref[...] = st0_ref[...]

    a_row = -jnp.exp(alog_ref[...])
    a_col = -jnp.exp(alogT_ref[...])
    q = SSD_CHUNK
    n_sub = xf_ref.shape[0] // q
    for k in range(n_sub):
        rows = slice(k * q, (k + 1) * q)
        yf_ref[rows, :] = _ssd_direction(xf_ref[rows, :], bf_ref[rows, :], cf_ref[rows, :],
                                         dtf_ref[rows, 0:SSM_HEADS], dtTf_ref[0:SSM_HEADS, rows],
                                         bias_ref[0:1, :], biasT_ref[:, 0:1], a_row[0:1, :], a_col[:, 0:1],
                                         st_ref.at[0], False)
    for k in reversed(range(n_sub)):
        rows = slice(k * q, (k + 1) * q)
        yb_ref[rows, :] = _ssd_direction(xb_ref[rows, :], bb_ref[rows, :], cb_ref[rows, :],
                                         dtb_ref[rows, SSM_HEADS:2 * SSM_HEADS],
                                         dtTb_ref[SSM_HEADS:2 * SSM_HEADS, rows],
                                         bias_ref[1:2, :], biasT_ref[:, 1:2], a_row[1:2, :], a_col[:, 1:2],
                                         st_ref.at[1], True)

    @pl.when(c == pl.num_programs(0) - 1)
    def _():
        stT_ref[...] = st_ref[...]


def ssd_scan(xbc, dt, dtT, dt_bias, a_log, st0):
    n_rows = xbc.shape[0]
    q = SSD_CHUNK * min(SSD_CHUNKS_PER_STEP, n_rows // SSD_CHUNK)
    nc = n_rows // q
    fwd = lambda c: c
    bwd = lambda c: nc - 1 - c

    def specs(ix):
        return [pl.BlockSpec((q, SSM_D_INNER), lambda c: (ix(c), 0)),
                pl.BlockSpec((q, 2 * SSM_D_STATE), lambda c: (ix(c), 2)),
                pl.BlockSpec((q, 2 * SSM_D_STATE), lambda c: (ix(c), 3)),
                pl.BlockSpec((q, 2 * SSM_HEADS), lambda c: (ix(c), 0)),
                pl.BlockSpec((2 * SSM_HEADS, q), lambda c: (0, ix(c)))]

    small = lambda shape: pl.BlockSpec(shape, lambda c: (0,) * len(shape))
    st_shape = (2, SSM_D_STATE, SSM_D_INNER)
    return pl.pallas_call(
        _ssd_kernel,
        grid=(nc,),
        in_specs=specs(fwd) + specs(bwd) + [small((2, SSM_HEADS)), small((SSM_HEADS, 2)),
                                            small((2, SSM_HEADS)), small((SSM_HEADS, 2)), small(st_shape)],
        out_specs=[pl.BlockSpec((q, SSM_D_INNER), lambda c: (c, 0)),
                   pl.BlockSpec((q, SSM_D_INNER), lambda c: (nc - 1 - c, 0)),
                   small(st_shape)],
        out_shape=[jax.ShapeDtypeStruct((n_rows, SSM_D_INNER), F32),
                   jax.ShapeDtypeStruct((n_rows, SSM_D_INNER), F32),
                   jax.ShapeDtypeStruct(st_shape, F32)],
        scratch_shapes=[pltpu.VMEM(st_shape, F32)],
        compiler_params=_cparams("arbitrary"),
        name="ssd_scan",
    )(xbc, xbc, xbc, dt, dtT, xbc, xbc, xbc, dt, dtT, dt_bias, dt_bias.T, a_log, a_log.T, st0)


def _ssd_finish_kernel(yf_ref, yb_ref, xs_ref, z_ref, d_ref, nw_ref, o_ref):
    hrow = lax.broadcasted_iota(jnp.int32, (SSM_HEADS, SSM_D_INNER), 0)
    hcol = lax.broadcasted_iota(jnp.int32, (SSM_HEADS, SSM_D_INNER), 1) // SSM_HEAD_DIM
    d_e = jnp.sum(jnp.where(hrow == hcol, d_ref[...], 0.0), axis=0, keepdims=True)
    y = yf_ref[...] + yb_ref[...] + d_e * xs_ref[...]
    y = y * _silu(z_ref[...])
    o_ref[...] = (y * lax.rsqrt(jnp.mean(y * y, axis=-1, keepdims=True) + NORM_EPS) * nw_ref[...]).astype(BF16)


def ssd_finish(yf, yb, xbc, p, z_col0, d_skip, norm_w):
    n_rows = yf.shape[0]
    tm = min(n_rows, ROWWISE_TM)
    zb = z_col0 // SSM_D_INNER
    blk = lambda cb: pl.BlockSpec((tm, SSM_D_INNER), lambda i: (i, cb))
    return pl.pallas_call(
        _ssd_finish_kernel,
        grid=(n_rows // tm,),
        in_specs=[blk(0), blk(0), blk(0), blk(zb),
                  pl.BlockSpec((SSM_HEADS, 1), lambda i: (0, 0)),
                  pl.BlockSpec((1, SSM_D_INNER), lambda i: (0, 0))],
        out_specs=blk(0),
        out_shape=jax.ShapeDtypeStruct((n_rows, SSM_D_INNER), BF16),
        compiler_params=_cparams("parallel"),
        name="ssd_finish",
    )(yf, yb, xbc, p, d_skip.reshape(SSM_HEADS, 1), norm_w.reshape(1, SSM_D_INNER))


LRU_X_COL = 2560
LRU_G_COL = 3072
SSM_Z_COL = 3584
SSM_XBC_COL = 4096


def _mixers(p, dt, qkv_q, qkv_c, qkv_l, lw, lam_init, lru_h0, ssd_st0):
    ya = diff_attention(qkv_q, qkv_c, qkv_l, lw["da_lambda"], lw["da_subln"], lam_init)
    yb = gqa_attention(qkv_q, qkv_c, qkv_l)
    yc, lru_hT = rglru(p, LRU_X_COL, LRU_G_COL, lw["lru_conv_w"], lw["lru_conv_b"], lw["lru_w_gates"],
                       lw["lru_b_gates"], lw["lru_lambda"], lru_h0)
    xbc = dwconv(p, SSM_XBC_COL, 2 * SSM_D_INNER, lw["ssm_conv_w"], lw["ssm_conv_b"], act=True)
    yf, ybk, ssd_stT = ssd_scan(xbc, dt, dt.T, lw["ssm_dt_bias"], lw["ssm_a_log"], ssd_st0)
    yd = ssd_finish(yf, ybk, xbc, p, SSM_Z_COL, lw["ssm_d"], lw["ssm_norm"])
    return [ya, yb, yc, yd], lru_hT, ssd_stT


def kernel(x, c, ctx, c_ctx, w_mod, b_mod, mix_norm, ffn_norm, w_in, w_out, da_lambda, da_subln, gqa_q_norm,
           gqa_k_norm, lru_conv_w, lru_conv_b, lru_w_gates, lru_b_gates, lru_lambda, ssm_conv_w, ssm_conv_b,
           ssm_dt_bias, ssm_a_log, ssm_d, ssm_norm, ffn_w_up, ffn_conv_w, ffn_conv_b, ffn_w_down, final_norm):
    depth = w_mod.shape[0]
    d = x.shape[-1]
    xl = x[0]
    xc = ctx[0]
    n_lat, n_ctx = xl.shape[0], xc.shape[0]

    cond8 = jnp.zeros((8, d), F32).at[0].set(c[0]).at[1].set(c_ctx)
    mods = modulation_all(cond8, w_mod, b_mod)
    tabs_l = _rope_tables(n_lat, DA_HEAD_DIM) + _rope_tables(n_lat, GQA_HEAD_DIM)
    tabs_c = _rope_tables(n_ctx, DA_HEAD_DIM) + _rope_tables(n_ctx, GQA_HEAD_DIM)
    lru_zero = jnp.zeros((2, LRU_WIDTH), F32)
    ssd_zero = jnp.zeros((2, SSM_D_STATE, SSM_D_INNER), F32)
    w_in_t = jnp.swapaxes(w_in, 1, 2)

    for layer in range(depth):
        need_ctx = layer < depth - 1
        lam_init = 0.8 - 0.6 * math.exp(-0.3 * layer)
        ml = [mods[layer, 0:1, k * d:(k + 1) * d] for k in range(6)]
        mc = [mods[layer, 1:2, k * d:(k + 1) * d] for k in range(6)]
        lw = dict(da_lambda=da_lambda[layer], da_subln=da_subln[layer],
                  lru_conv_w=lru_conv_w[layer], lru_conv_b=lru_conv_b[layer], lru_w_gates=lru_w_gates[layer],
                  lru_b_gates=lru_b_gates[layer], lru_lambda=lru_lambda[layer], ssm_conv_w=ssm_conv_w[layer],
                  ssm_conv_b=ssm_conv_b[layer], ssm_dt_bias=ssm_dt_bias[layer], ssm_a_log=ssm_a_log[layer],
                  ssm_d=ssm_d[layer], ssm_norm=ssm_norm[layer])
        nw_m = mix_norm[layer].reshape(1, d)
        nw_f = ffn_norm[layer].reshape(1, d)

        xn_c, dt_c = adaln(xc, nw_m, mc[0], mc[1], w_in_t, layer)
        xn_l, dt_l = adaln(xl, nw_m, ml[0], ml[1], w_in_t, layer)
        p_c = in_proj(xn_c, w_in_t, layer)
        p_l = in_proj(xn_l, w_in_t, layer)
        qkv_c = attn_prep(p_c, tabs_c, gqa_q_norm[layer], gqa_k_norm[layer], rope=False)
        qkv_l = attn_prep(p_l, tabs_l, gqa_q_norm[layer], gqa_k_norm[layer], rope=True)

        y_c, lru_h, ssd_st = _mixers(p_c, dt_c, qkv_c, qkv_c, None, lw, lam_init, lru_zero, ssd_zero)
        y_l, _, _ = _mixers(p_l, dt_l, qkv_l, qkv_c, qkv_l, lw, lam_init, lru_h, ssd_st)

        xl = resid_proj(y_l, w_out, layer, xl, ml[2])
        act = ffn_up(adaln(xl, nw_f, ml[3], ml[4]), ffn_w_up, layer, ffn_conv_w[layer], ffn_conv_b[layer])
        xl = resid_proj([act], ffn_w_down, layer, xl, ml[5])
        if need_ctx:
            xc = resid_proj(y_c, w_out, layer, xc, mc[2])
            act = ffn_up(adaln(xc, nw_f, mc[3], mc[4]), ffn_w_up, layer, ffn_conv_w[layer], ffn_conv_b[layer])
            xc = resid_proj([act], ffn_w_down, layer, xc, mc[5])

    return final_rmsnorm(xl, final_norm)[None]
```

```python
import functools
import math

import jax
import jax.numpy as jnp
from jax import lax
from jax.experimental import pallas as pl
from jax.experimental.pallas import tpu as pltpu

F32 = jnp.float32
BF16 = jnp.bfloat16

GRID_W = 64
GROUP_WIDTH = 512
DA_HEAD_DIM = 64
DA_HEADS = 4
GQA_HEAD_DIM = 128
GQA_HEADS = 4
GQA_KV_HEADS = 2
LRU_WIDTH = 512
LRU_BLOCKS = 4
RGLRU_C = 8.0
SSM_D_INNER = 512
SSM_HEAD_DIM = 64
SSM_HEADS = 8
SSM_GROUPS = 2
SSM_D_STATE = 128
SSD_CHUNK = 128
D_FF = 5632
ROPE_THETA = 10000.0
NORM_EPS = 1e-6
LOG2E = math.log2(math.e)

LANES = 128
SUBLANES = 8
HALO = 16
IN_COLS = 5136
IN_MAIN_COLS = 5120
N_DT = IN_COLS - IN_MAIN_COLS
ATTN_COLS = 2560
QA_BLK, QB_BLK, KD_BLK, VD_BLK = 0, 4, 8, 12
QG_BLK, KG_BLK, VG_BLK = 20, 24, 26
QKV_COLS = 30 * LANES
P_QD, P_KD, P_VD, P_QG, P_KG, P_VG = 0, 4, 8, 12, 16, 18

VMEM_LIMIT = 56 * 1024 * 1024
PROJ_TM = 1024
IN_TN = 1024
RESID_TM = 512
RESID_TN_MAX = 1024
RESID_W_ELEMS = 2048 * 1024
ROWWISE_TM = 1024
FFN_TN = 512


def _cparams(*sem):
    return pltpu.CompilerParams(dimension_semantics=sem, vmem_limit_bytes=VMEM_LIMIT)


def _silu(x):
    return x * jax.nn.sigmoid(x)


def _norm_mod(x, nw, shift, scale):
    y = x * lax.rsqrt(jnp.mean(x * x, axis=-1, keepdims=True) + NORM_EPS) * nw
    return y * (1.0 + scale) + shift


def _mod_kernel(c_ref, w_ref, b_ref, o_ref):
    s = _silu(c_ref[...]).astype(BF16)
    o_ref[...] = jnp.dot(s, w_ref[...].astype(BF16), preferred_element_type=F32) + b_ref[...]


def modulation_all(cond8, w_mod, b_mod):
    depth, d, n = w_mod.shape
    tn = IN_TN
    return pl.pallas_call(
        _mod_kernel,
        grid=(depth, n // tn),
        in_specs=[
            pl.BlockSpec((8, d), lambda l, j: (0, 0)),
            pl.BlockSpec((None, d, tn), lambda l, j: (l, 0, j)),
            pl.BlockSpec((None, 1, tn), lambda l, j: (l, 0, j)),
        ],
        out_specs=pl.BlockSpec((None, 8, tn), lambda l, j: (l, 0, j)),
        out_shape=jax.ShapeDtypeStruct((depth, 8, n), F32),
        compiler_params=_cparams("parallel", "parallel"),
        name="modulation",
    )(cond8, w_mod, b_mod.reshape(depth, 1, n))


ADALN_ROWS = 16
_NT = (((1,), (1,)), ((), ()))


def _adaln_kernel(*refs, with_dt):
    if with_dt:
        x_ref, nw_ref, sh_ref, sc_ref, wdt_ref, o_ref, dt_ref = refs
    else:
        x_ref, nw_ref, sh_ref, sc_ref, o_ref = refs
    gain = nw_ref[...] * (1.0 + sc_ref[...])
    shift = sh_ref[...]
    d = x_ref.shape[1]

    def body(r, _):
        rows = pl.ds(pl.multiple_of(r * ADALN_ROWS, ADALN_ROWS), ADALN_ROWS)
        x = x_ref[rows, :]
        inv = lax.rsqrt(jnp.sum(x * x, axis=-1, keepdims=True) * (1.0 / d) + NORM_EPS)
        o_ref[rows, :] = (x * inv * gain + shift).astype(BF16)
        return 0

    lax.fori_loop(0, x_ref.shape[0] // ADALN_ROWS, body, 0, unroll=4)
    if with_dt:
        dt_ref[...] = lax.dot_general(o_ref[...], wdt_ref[...].astype(BF16), _NT, preferred_element_type=F32)


def adaln(x, nw, shift, scale, w_in_t=None, layer=None):
    m, d = x.shape
    tm = min(m, PROJ_TM)
    with_dt = w_in_t is not None
    vec = pl.BlockSpec((1, d), lambda i: (0, 0))
    rows = pl.BlockSpec((tm, d), lambda i: (i, 0))
    in_specs, args = [rows, vec, vec, vec], [x, nw, shift, scale]
    out_specs, out_shape = rows, jax.ShapeDtypeStruct((m, d), BF16)
    if with_dt:
        in_specs.append(pl.BlockSpec((None, N_DT, d), lambda i: (layer, IN_MAIN_COLS // N_DT, 0)))
        args.append(w_in_t)
        out_specs = [rows, pl.BlockSpec((tm, N_DT), lambda i: (i, 0))]
        out_shape = [out_shape, jax.ShapeDtypeStruct((m, N_DT), F32)]
    return pl.pallas_call(
        functools.partial(_adaln_kernel, with_dt=with_dt),
        grid=(m // tm,),
        in_specs=in_specs,
        out_specs=out_specs,
        out_shape=out_shape,
        compiler_params=_cparams("parallel"),
        name="adaln",
    )(*args)


def _lat_block(i):
    return jnp.maximum(i - 1, 0)


def _inproj_kernel(xc_ref, xl_ref, w_ref, oc_ref, ol_ref, wb_ref):
    i = pl.program_id(1)

    @pl.when(i == 0)
    def _():
        wb_ref[...] = w_ref[...].astype(BF16)
        oc_ref[...] = lax.dot_general(xc_ref[...], wb_ref[...], _NT, preferred_element_type=F32)
        ol_ref[...] = jnp.zeros_like(ol_ref)

    @pl.when(i > 0)
    def _():
        ol_ref[...] = lax.dot_general(xl_ref[...], wb_ref[...], _NT, preferred_element_type=F32)


def in_proj(xn_c, xn_l, w_in_t, layer):
    (mc, d), ml = xn_c.shape, xn_l.shape[0]
    tm = min(ml, PROJ_TM)
    return pl.pallas_call(
        _inproj_kernel,
        grid=(IN_MAIN_COLS // IN_TN, 1 + ml // tm),
        in_specs=[pl.BlockSpec((mc, d), lambda j, i: (0, 0)),
                  pl.BlockSpec((tm, d), lambda j, i: (_lat_block(i), 0)),
                  pl.BlockSpec((None, IN_TN, d), lambda j, i: (layer, j, 0))],
        out_specs=[pl.BlockSpec((mc, IN_TN), lambda j, i: (0, j)),
                   pl.BlockSpec((tm, IN_TN), lambda j, i: (_lat_block(i), j))],
        out_shape=[jax.ShapeDtypeStruct((mc, IN_MAIN_COLS), F32), jax.ShapeDtypeStruct((ml, IN_MAIN_COLS), F32)],
        scratch_shapes=[pltpu.VMEM((IN_TN, d), BF16)],
        compiler_params=_cparams("parallel", "arbitrary"),
        name="in_proj",
    )(xn_c, xn_l, w_in_t)


def _resid_kernel(*refs, n_y, with_ctx):
    refs = list(refs)
    ctx_refs = None
    if with_ctx:
        ctx_refs, refs = refs[:n_y + 2], refs[n_y + 2:]
    lat_refs, refs = refs[:n_y + 2], refs[n_y + 2:]
    w_ref = refs.pop(0)
    oc_ref = refs.pop(0) if with_ctx else None
    ol_ref, wb_ref = refs
    i = pl.program_id(1)
    kk = wb_ref.shape[0] // n_y

    def project(stream, o_ref):
        ys, x_ref, g_ref = stream[:n_y], stream[n_y], stream[n_y + 1]
        acc = functools.reduce(jnp.add, [
            jnp.dot(ys[a][...], wb_ref[a * kk:(a + 1) * kk, :], preferred_element_type=F32) for a in range(n_y)])
        o_ref[...] = x_ref[...] + g_ref[...] * acc

    @pl.when(i == 0)
    def _():
        wb_ref[...] = w_ref[...].astype(BF16)
        if with_ctx:
            project(ctx_refs, oc_ref)
            ol_ref[...] = jnp.zeros_like(ol_ref)
        else:
            project(lat_refs, ol_ref)

    @pl.when(i > 0)
    def _():
        project(lat_refs, ol_ref)


def resid_proj(w_all, layer, lat, ctx=None):
    ys_l, x_l, gate_l = lat
    ml = x_l.shape[0]
    _, k, n = w_all.shape
    n_y = len(ys_l)
    kk = k // n_y
    tm = min(ml, RESID_TM)
    tn = RESID_TN_MAX if k * RESID_TN_MAX <= RESID_W_ELEMS else RESID_TN_MAX // 2
    with_ctx = ctx is not None
    lat_blk = _lat_block if with_ctx else (lambda i: i)

    def stream_specs(rows, blk):
        return ([pl.BlockSpec((rows, kk), lambda j, i: (blk(i), 0)) for _ in range(n_y)]
                + [pl.BlockSpec((rows, tn), lambda j, i: (blk(i), j)), pl.BlockSpec((1, tn), lambda j, i: (0, j))])

    in_specs = stream_specs(tm, lat_blk) + [pl.BlockSpec((None, k, tn), lambda j, i: (layer, 0, j))]
    args = [*ys_l, x_l, gate_l, w_all]
    out_specs = [pl.BlockSpec((tm, tn), lambda j, i: (lat_blk(i), j))]
    out_shape = [jax.ShapeDtypeStruct((ml, n), F32)]
    if with_ctx:
        ys_c, x_c, gate_c = ctx
        mc = x_c.shape[0]
        in_specs = stream_specs(mc, lambda i: 0) + in_specs
        args = [*ys_c, x_c, gate_c] + args
        out_specs = [pl.BlockSpec((mc, tn), lambda j, i: (0, j))] + out_specs
        out_shape = [jax.ShapeDtypeStruct((mc, n), F32)] + out_shape
    out = pl.pallas_call(
        functools.partial(_resid_kernel, n_y=n_y, with_ctx=with_ctx),
        grid=(n // tn, ml // tm + int(with_ctx)),
        in_specs=in_specs,
        out_specs=out_specs,
        out_shape=out_shape,
        scratch_shapes=[pltpu.VMEM((k, tn), BF16)],
        compiler_params=_cparams("parallel", "arbitrary"),
        name="resid_proj",
    )(*args)
    return tuple(out) if with_ctx else out[0]


def _ffn_up_kernel(*refs, n_lat_blocks, with_ctx):
    refs = list(refs)
    xc_ref = refs.pop(0) if with_ctx else None
    xp_ref, x_ref, xx_ref, wg_ref, wu_ref, cw_ref, cb_ref = refs[:7]
    del refs[:7]
    oc_ref = refs.pop(0) if with_ctx else None
    o_ref, xn_ref, g_ref, wb_ref = refs
    i = pl.program_id(1)
    blk = (i - 1) if with_ctx else i

    def up_block(prev, cur_ref, nxt, out_ref):
        n = cur_ref.shape[0]
        xn_ref[pl.ds(0, HALO), :] = prev
        xn_ref[pl.ds(HALO, n), :] = cur_ref[...]
        xn_ref[pl.ds(HALO + n, HALO), :] = nxt
        g_ref[pl.ds(0, n + 2 * HALO), :] = jnp.dot(xn_ref[pl.ds(0, n + 2 * HALO), :], wb_ref[0],
                                                   preferred_element_type=F32)
        up = jnp.dot(cur_ref[...], wb_ref[1], preferred_element_type=F32)
        cw = cw_ref[...]
        gc = (cb_ref[...] + cw[0:1] * g_ref[pl.ds(HALO - 1, n), :] + cw[1:2] * g_ref[pl.ds(HALO, n), :]
              + cw[2:3] * g_ref[pl.ds(HALO + 1, n), :])
        out_ref[...] = (_silu(gc) * up).astype(BF16)

    def latent_block():
        zeros = jnp.zeros_like(xp_ref)
        up_block(jnp.where(blk == 0, zeros, xp_ref[...]), x_ref,
                 jnp.where(blk == n_lat_blocks - 1, zeros, xx_ref[...]), o_ref)

    @pl.when(i == 0)
    def _():
        wb_ref[0] = wg_ref[...].astype(BF16)
        wb_ref[1] = wu_ref[...].astype(BF16)
        if with_ctx:
            zeros = jnp.zeros_like(xp_ref)
            up_block(zeros, xc_ref, zeros, oc_ref)
            o_ref[...] = jnp.zeros_like(o_ref)
        else:
            latent_block()

    @pl.when(i > 0)
    def _():
        latent_block()


def ffn_up(w_up_all, layer, conv_w, conv_b, xn_l, xn_c=None):
    ml, d = xn_l.shape
    dff = w_up_all.shape[2] // 2
    tm = min(ml, PROJ_TM)
    tn = FFN_TN
    nrb = ml // tm
    hb = tm // HALO
    nhb = ml // HALO
    with_ctx = xn_c is not None
    blk = _lat_block if with_ctx else (lambda i: i)
    in_specs = [pl.BlockSpec((HALO, d), lambda j, i: (jnp.maximum(blk(i) * hb - 1, 0), 0)),
                pl.BlockSpec((tm, d), lambda j, i: (blk(i), 0)),
                pl.BlockSpec((HALO, d), lambda j, i: (jnp.minimum((blk(i) + 1) * hb, nhb - 1), 0)),
                pl.BlockSpec((None, d, tn), lambda j, i: (layer, 0, j)),
                pl.BlockSpec((None, d, tn), lambda j, i: (layer, 0, j + dff // tn)),
                pl.BlockSpec((3, tn), lambda j, i: (0, j)),
                pl.BlockSpec((1, tn), lambda j, i: (0, j))]
    args = [xn_l, xn_l, xn_l, w_up_all, w_up_all, conv_w, conv_b.reshape(1, dff)]
    out_specs = [pl.BlockSpec((tm, tn), lambda j, i: (blk(i), j))]
    out_shape = [jax.ShapeDtypeStruct((ml, dff), BF16)]
    if with_ctx:
        mc = xn_c.shape[0]
        in_specs = [pl.BlockSpec((mc, d), lambda j, i: (0, 0))] + in_specs
        args = [xn_c] + args
        out_specs = [pl.BlockSpec((mc, tn), lambda j, i: (0, j))] + out_specs
        out_shape = [jax.ShapeDtypeStruct((mc, dff), BF16)] + out_shape
    out = pl.pallas_call(
        functools.partial(_ffn_up_kernel, n_lat_blocks=nrb, with_ctx=with_ctx),
        grid=(dff // tn, nrb + int(with_ctx)),
        in_specs=in_specs,
        out_specs=out_specs,
        out_shape=out_shape,
        scratch_shapes=[pltpu.VMEM((tm + 2 * HALO, d), BF16), pltpu.VMEM((tm + 2 * HALO, tn), F32),
                        pltpu.VMEM((2, d, tn), BF16)],
        compiler_params=_cparams("parallel", "arbitrary"),
        name="ffn_up",
    )(*args)
    return tuple(out) if with_ctx else out[0]


def _rmsnorm_kernel(x_ref, w_ref, o_ref):
    x = x_ref[...]
    o_ref[...] = x * lax.rsqrt(jnp.mean(x * x, axis=-1, keepdims=True) + NORM_EPS) * w_ref[...]


def final_rmsnorm(x, w):
    m, d = x.shape
    tm = min(m, ROWWISE_TM)
    return pl.pallas_call(
        _rmsnorm_kernel,
        grid=(m // tm,),
        in_specs=[pl.BlockSpec((tm, d), lambda i: (i, 0)), pl.BlockSpec((1, d), lambda i: (0, 0))],
        out_specs=pl.BlockSpec((tm, d), lambda i: (i, 0)),
        out_shape=jax.ShapeDtypeStruct((m, d), F32),
        compiler_params=_cparams("parallel"),
        name="final_norm",
    )(x, w.reshape(1, d))


def _rope_tables(n_tok, head_dim):
    q = head_dim // 4
    lane = jnp.arange(LANES, dtype=jnp.int32)
    u = lane % head_dim
    region = u // q
    freqs = jnp.power(ROPE_THETA, -(u % q).astype(F32) / q)[None, :]
    ang_r = jnp.arange(n_tok // GRID_W, dtype=jnp.int32).astype(F32)[:, None] * freqs
    ang_c = jnp.arange(GRID_W, dtype=jnp.int32).astype(F32)[:, None] * freqs
    by_row = (region < 2)[None, None, :]

    def per_token(f):
        return jnp.where(by_row, f(ang_r)[:, None, :], f(ang_c)[None, :, :]).reshape(n_tok, LANES)

    cos, sin = per_token(jnp.cos), per_token(jnp.sin)
    first = (region % 2 == 0)[None, :]
    return cos, jnp.where(first, -sin, 0.0), jnp.where(first, 0.0, sin)


def _prep_kernel(p_ref, cd_ref, ad_ref, bd_ref, cg_ref, ag_ref, bg_ref, qn_ref, kn_ref, o_ref, *, rope):
    lane = lax.broadcasted_iota(jnp.int32, (1, LANES), 1)
    lo = lane < DA_HEAD_DIM

    def blk(b):
        return p_ref[:, b * LANES:(b + 1) * LANES]

    def put(b, v):
        o_ref[:, b * LANES:(b + 1) * LANES] = v.astype(BF16)

    def rot(x, cos, sa, sb, quarter):
        if not rope:
            return x
        return (x * cos[...] + pltpu.roll(x, LANES - quarter, 1) * sa[...]
                + pltpu.roll(x, quarter, 1) * sb[...])

    def rms(x, w):
        return x * lax.rsqrt(jnp.mean(x * x, axis=-1, keepdims=True) + NORM_EPS) * w[...]

    qd = DA_HEAD_DIM // 4
    qg = GQA_HEAD_DIM // 4
    ones = jnp.ones((p_ref.shape[0], LANES), F32)
    for h in range(DA_HEADS):
        q = rot(blk(P_QD + h), cd_ref, ad_ref, bd_ref, qd) * (DA_HEAD_DIM ** -0.5 * LOG2E)
        put(QA_BLK + h, jnp.where(lo, q, 0.0))
        put(QB_BLK + h, jnp.where(lo, 0.0, q))
        put(KD_BLK + h, rot(blk(P_KD + h), cd_ref, ad_ref, bd_ref, qd))
        put(VD_BLK + 2 * h, blk(P_VD + h))
        put(VD_BLK + 2 * h + 1, ones)
    for h in range(GQA_HEADS):
        q = rot(rms(blk(P_QG + h), qn_ref), cg_ref, ag_ref, bg_ref, qg) * (GQA_HEAD_DIM ** -0.5 * LOG2E)
        put(QG_BLK + h, q)
    for h in range(GQA_KV_HEADS):
        put(KG_BLK + h, rot(rms(blk(P_KG + h), kn_ref), cg_ref, ag_ref, bg_ref, qg))
        put(VG_BLK + 2 * h, blk(P_VG + h))
        put(VG_BLK + 2 * h + 1, ones)


def attn_prep(p, tables, q_norm, k_norm, rope):
    m = p.shape[0]
    tm = min(m, ROWWISE_TM)
    tab = pl.BlockSpec((tm, LANES), lambda i: (i, 0))
    vec = pl.BlockSpec((1, LANES), lambda i: (0, 0))
    return pl.pallas_call(
        functools.partial(_prep_kernel, rope=rope),
        grid=(m // tm,),
        in_specs=[pl.BlockSpec((tm, ATTN_COLS), lambda i: (i, 0))] + [tab] * 6 + [vec, vec],
        out_specs=pl.BlockSpec((tm, QKV_COLS), lambda i: (i, 0)),
        out_shape=jax.ShapeDtypeStruct((m, QKV_COLS), BF16),
        compiler_params=_cparams("parallel"),
        name="attn_prep",
    )(p, *tables, q_norm.reshape(1, LANES), k_norm.reshape(1, LANES))


def _attn_kernel(*refs, tk, n_kv, diff, lam_init):
    refs = list(refs)
    qa_ref, qb_ref, kc_ref, vc_ref = refs[:4]
    del refs[:4]
    if n_kv:
        kl_ref, vl_ref = refs[:2]
        del refs[:2]
    if diff:
        lam_ref, sw_ref = refs[:2]
        del refs[:2]
    o_ref = refs.pop(0)
    if n_kv:
        s_ref = refs.pop(0)
    sc_ref, m_ref, acc_ref = refs
    tq = qa_ref.shape[0]
    nt = (((1,), (1,)), ((), ()))
    q2 = jnp.concatenate([qa_ref[...], qb_ref[...]], axis=0)

    def lane_tiles(x):
        return [x[:, t * LANES:(t + 1) * LANES] for t in range(x.shape[1] // LANES)]

    def key_rows(ref, c):
        return ref[pl.ds(pl.multiple_of(c * tk, tk), tk), :]

    def chunks(body):
        if n_kv:
            lax.fori_loop(0, n_kv, lambda c, _: body(c) or 0, 0)

    sc = lax.dot_general(q2, kc_ref[...], nt, preferred_element_type=F32)
    sc_ref[...] = sc
    m_ref[...] = functools.reduce(jnp.maximum, lane_tiles(sc))

    def score_chunk(c):
        s = lax.dot_general(q2, key_rows(kl_ref, c), nt, preferred_element_type=F32)
        s_ref[c] = s
        m_ref[...] = functools.reduce(jnp.maximum, lane_tiles(s), m_ref[...])

    chunks(score_chunk)
    m_ref[...] = jnp.broadcast_to(jnp.max(m_ref[...], axis=-1, keepdims=True), (2 * tq, LANES))

    def weigh(s, vs):
        m = m_ref[...]
        p = jnp.concatenate([jnp.exp2(st - m).astype(BF16) for st in lane_tiles(s)], axis=1)
        return jnp.dot(p, vs, preferred_element_type=F32)

    acc_ref[...] = weigh(sc_ref[...], vc_ref[...])

    def weigh_chunk(c):
        acc_ref[...] += weigh(s_ref[c], key_rows(vl_ref, c))

    chunks(weigh_chunk)
    o = acc_ref[:, 0:LANES] / acc_ref[:, LANES:2 * LANES]
    oa = o[0:tq]
    ob = o[tq:2 * tq]
    if diff:
        lv = lam_ref[...]
        lam = (jnp.exp(jnp.sum(lv[0:1] * lv[1:2], keepdims=True))
               - jnp.exp(jnp.sum(lv[2:3] * lv[3:4], keepdims=True)) + lam_init)
        o = oa - lam * ob
        o = o * lax.rsqrt(jnp.mean(o * o, axis=-1, keepdims=True) + NORM_EPS) * sw_ref[...]
        o_ref[...] = (o * (1.0 - lam_init)).astype(BF16)
    else:
        o_ref[:, 0:LANES] = oa.astype(BF16)
        o_ref[:, LANES:2 * LANES] = ob.astype(BF16)


ATTN_TQ = 512
ATTN_TK = (4096, 2048, 1024, 512, 256, 128)


def _attention(qkv_q, qkv_c, qkv_l, extra, extra_specs, cols, n_heads, out_width, name, **static):
    qa_col, qb_col, k_col, v_col = cols
    sq, n_ctx = qkv_q.shape[0], qkv_c.shape[0]
    tq = min(sq, ATTN_TQ)
    n_lat = 0 if qkv_l is None else qkv_l.shape[0]
    tk = next((t for t in ATTN_TK if n_lat and n_lat % t == 0), 0)
    n_kv = n_lat // tk if n_lat else 0

    def kv_specs(rows):
        return [pl.BlockSpec((rows, LANES), lambda h, i: (0, k_col(h))),
                pl.BlockSpec((rows, 2 * LANES), lambda h, i: (0, v_col(h)))]

    in_specs = [pl.BlockSpec((tq, LANES), lambda h, i: (i, qa_col(h))),
                pl.BlockSpec((tq, LANES), lambda h, i: (i, qb_col(h)))] + kv_specs(n_ctx)
    args = [qkv_q, qkv_q, qkv_c, qkv_c]
    scratch = [pltpu.VMEM((2 * tq, n_ctx), F32), pltpu.VMEM((2 * tq, LANES), F32),
               pltpu.VMEM((2 * tq, 2 * LANES), F32)]
    if n_kv:
        in_specs += kv_specs(n_lat)
        args += [qkv_l, qkv_l]
        scratch = [pltpu.VMEM((n_kv, 2 * tq, tk), F32)] + scratch
    return pl.pallas_call(
        functools.partial(_attn_kernel, tk=tk, n_kv=n_kv, **static),
        grid=(n_heads, sq // tq),
        in_specs=in_specs + extra_specs,
        out_specs=pl.BlockSpec((tq, out_width), lambda h, i: (i, h)),
        out_shape=jax.ShapeDtypeStruct((sq, GROUP_WIDTH), BF16),
        scratch_shapes=scratch,
        compiler_params=_cparams("parallel", "parallel"),
        name=name,
    )(*args, *extra)


def diff_attention(qkv_q, qkv_c, qkv_l, da_lambda, subln_w, lam_init):
    cols = (lambda h: QA_BLK + h, lambda h: QB_BLK + h, lambda h: KD_BLK + h, lambda h: VD_BLK // 2 + h)
    extra_specs = [pl.BlockSpec((4, DA_HEAD_DIM), lambda h, i: (0, 0)), pl.BlockSpec((1, LANES), lambda h, i: (0, 0))]
    return _attention(qkv_q, qkv_c, qkv_l, [da_lambda, subln_w.reshape(1, LANES)], extra_specs, cols,
                      DA_HEADS, LANES, "diff_attn", diff=True, lam_init=lam_init)


def gqa_attention(qkv_q, qkv_c, qkv_l):
    cols = (lambda h: QG_BLK + 2 * h, lambda h: QG_BLK + 2 * h + 1, lambda h: KG_BLK + h,
            lambda h: VG_BLK // 2 + h)
    return _attention(qkv_q, qkv_c, qkv_l, [], [], cols, GQA_KV_HEADS, 2 * LANES, "gqa_attn",
                      diff=False, lam_init=0.0)


CONV_CHUNK = 256


def _conv_chunk(x_ref, base, n_rows, chunk, cw, cb):
    cur = x_ref[pl.ds(base, chunk), :]
    prev = x_ref[pl.ds(pl.multiple_of(jnp.maximum(base - 8, 0), 8), 8), :]
    nxt = x_ref[pl.ds(pl.multiple_of(jnp.minimum(base + chunk, n_rows - 8), 8), 8), :]
    prev = jnp.where(base == 0, 0.0, prev)
    nxt = jnp.where(base + chunk >= n_rows, 0.0, nxt)
    cat = jnp.concatenate([prev, cur, nxt], axis=0)
    n = chunk + 16
    xm2 = pltpu.roll(cat, 2, 0)[8:8 + chunk]
    xm1 = pltpu.roll(cat, 1, 0)[8:8 + chunk]
    xp1 = pltpu.roll(cat, n - 1, 0)[8:8 + chunk]
    return cb + cw[0:1] * xm2 + cw[1:2] * xm1 + cw[2:3] * cur + cw[3:4] * xp1


def _dwconv_kernel(x_ref, w_ref, b_ref, o_ref, *, n_rows, chunk, act):
    cw = w_ref[...]
    cb = b_ref[...]

    def body(c, _):
        base = pl.multiple_of(c * chunk, chunk)
        y = _conv_chunk(x_ref, base, n_rows, chunk, cw, cb)
        o_ref[pl.ds(base, chunk), :] = _silu(y) if act else y
        return 0

    lax.fori_loop(0, n_rows // chunk, body, 0)


def dwconv(p, col0, width, conv_w, conv_b, act):
    n_rows = p.shape[0]
    chunk = min(n_rows, CONV_CHUNK)
    b0 = col0 // LANES
    return pl.pallas_call(
        functools.partial(_dwconv_kernel, n_rows=n_rows, chunk=chunk, act=act),
        grid=(width // LANES,),
        in_specs=[pl.BlockSpec((n_rows, LANES), lambda j: (0, b0 + j)),
                  pl.BlockSpec((4, LANES), lambda j: (0, j)),
                  pl.BlockSpec((1, LANES), lambda j: (0, j))],
        out_specs=pl.BlockSpec((n_rows, LANES), lambda j: (0, j)),
        out_shape=jax.ShapeDtypeStruct((n_rows, width), F32),
        compiler_params=_cparams("parallel"),
        name="dwconv",
    )(p, conv_w, conv_b.reshape(1, width))


LRU_CHUNK = 256


def _lru_scan_chunk(a, u, h_in, reverse):
    n = a.shape[0]
    row = lax.broadcasted_iota(jnp.int32, (n, 1), 0)
    for k in (1, 2, 4):
        if reverse:
            keep = row < n - k
            a_s = jnp.where(keep, pltpu.roll(a, n - k, 0), 1.0)
            u_s = jnp.where(keep, pltpu.roll(u, n - k, 0), 0.0)
        else:
            keep = row >= k
            a_s = jnp.where(keep, pltpu.roll(a, k, 0), 1.0)
            u_s = jnp.where(keep, pltpu.roll(u, k, 0), 0.0)
        u = u + a * u_s
        a = a * a_s
    n_tiles = n // SUBLANES
    order = range(n_tiles - 1, -1, -1) if reverse else range(n_tiles)
    tiles = [None] * n_tiles
    h = h_in
    for i in order:
        sl = slice(i * SUBLANES, (i + 1) * SUBLANES)
        h = u[sl] + a[sl] * h
        tiles[i] = h
    return jnp.concatenate(tiles, axis=0)


def _lru_kernel(x_ref, g_ref, cw_ref, cb_ref, w_ref, b_ref, lam_ref, h0_ref, y_ref, hT_ref, hf_ref, hb_ref,
                *, n_rows, chunk):
    n_chunks = n_rows // chunk
    cw = cw_ref[...]
    cb = cb_ref[...]

    def gates(d):
        lam = lam_ref[d:d + 1, :]
        log_sig = jnp.minimum(lam, 0.0) - jnp.log1p(jnp.exp(-jnp.abs(lam)))
        return (w_ref[d, 0].astype(BF16), w_ref[d, 1].astype(BF16), b_ref[d, 0:1, :], b_ref[d, 1:2, :], log_sig)

    def sweep(base, h, params, reverse):
        w_r, w_i, b_r, b_i, log_sig = params
        x = _conv_chunk(x_ref, base, n_rows, chunk, cw, cb)
        xb = x.astype(BF16)
        r = jax.nn.sigmoid(jnp.dot(xb, w_r, preferred_element_type=F32) + b_r)
        i = jax.nn.sigmoid(jnp.dot(xb, w_i, preferred_element_type=F32) + b_i)
        log_a = RGLRU_C * r * log_sig
        a = jnp.exp(log_a)
        u = jnp.sqrt(1.0 - a * a) * (i * x)
        return _lru_scan_chunk(a, u, h, reverse)

    fwd, bwd = gates(0), gates(1)

    def body(c, carry):
        hf, hb = carry
        base_f = pl.multiple_of(c * chunk, chunk)
        base_b = pl.multiple_of((n_chunks - 1 - c) * chunk, chunk)
        hs_f = sweep(base_f, hf, fwd, False)
        hs_b = sweep(base_b, hb, bwd, True)
        hf_ref[pl.ds(base_f, chunk), :] = hs_f
        hb_ref[pl.ds(base_b, chunk), :] = hs_b
        return hs_f[chunk - 1:chunk], hs_b[0:1]

    hT_ref[0:1, :], hT_ref[1:2, :] = lax.fori_loop(0, n_chunks, body, (h0_ref[0:1, :], h0_ref[1:2, :]))

    def emit(c, _):
        rows = pl.ds(pl.multiple_of(c * chunk, chunk), chunk)
        y_ref[rows, :] = ((hf_ref[rows, :] + hb_ref[rows, :]) * jax.nn.gelu(g_ref[rows, :])).astype(BF16)
        return 0

    lax.fori_loop(0, n_chunks, emit, 0)


def rglru(p, x_col0, g_col0, conv_w, conv_b, w_gates, b_gates, lam, h0):
    n_rows = p.shape[0]
    chunk = min(n_rows, LRU_CHUNK)
    xb = x_col0 // LANES
    gb = g_col0 // LANES
    seq = pltpu.VMEM((n_rows, LANES), F32)
    return pl.pallas_call(
        functools.partial(_lru_kernel, n_rows=n_rows, chunk=chunk),
        grid=(LRU_BLOCKS,),
        in_specs=[pl.BlockSpec((n_rows, LANES), lambda j: (0, xb + j)),
                  pl.BlockSpec((n_rows, LANES), lambda j: (0, gb + j)),
                  pl.BlockSpec((4, LANES), lambda j: (0, j)),
                  pl.BlockSpec((1, LANES), lambda j: (0, j)),
                  pl.BlockSpec((2, 2, None, LANES, LANES), lambda j: (0, 0, j, 0, 0)),
                  pl.BlockSpec((2, 2, LANES), lambda j: (0, 0, j)),
                  pl.BlockSpec((2, LANES), lambda j: (0, j)),
                  pl.BlockSpec((2, LANES), lambda j: (0, j))],
        out_specs=[pl.BlockSpec((n_rows, LANES), lambda j: (0, j)),
                   pl.BlockSpec((2, LANES), lambda j: (0, j))],
        out_shape=[jax.ShapeDtypeStruct((n_rows, LRU_WIDTH), BF16),
                   jax.ShapeDtypeStruct((2, LRU_WIDTH), F32)],
        scratch_shapes=[seq, seq],
        compiler_params=_cparams("parallel"),
        name="rglru",
    )(p, p, conv_w, conv_b.reshape(1, LRU_WIDTH), w_gates, b_gates, lam, h0)


SSD_CHUNKS_PER_STEP = 4


def _ssd_direction(xs, bm, cm, dt_raw, dtT_raw, bias_row, bias_col, a_row, a_col, st_ref, reverse):
    q = SSD_CHUNK
    hi = lax.Precision.HIGHEST
    li = lax.broadcasted_iota(jnp.int32, (q, q), 0)
    si = lax.broadcasted_iota(jnp.int32, (q, q), 1)
    causal = (si >= li) if reverse else (si <= li)
    tri = causal.astype(F32)
    dt = jax.nn.softplus(dt_raw + bias_row)
    dtT = jax.nn.softplus(dtT_raw + bias_col)
    adt = dt * a_row
    adtT = dtT * a_col
    cum = jnp.dot(tri, adt, preferred_element_type=F32, precision=hi)
    cumT = lax.dot_general(adtT, tri, (((1,), (1,)), ((), ())), preferred_element_type=F32,
                           precision=hi)
    edge = cum[0:1] if reverse else cum[q - 1:q]

    def lanes(v):
        return [jnp.broadcast_to(v[:, h:h + 1], (v.shape[0], LANES)) for h in range(SSM_HEADS)]

    def per_channel(cols):
        first = lax.broadcasted_iota(jnp.int32, cols[0].shape, 1) < SSM_HEAD_DIM
        return jnp.concatenate([jnp.where(first, cols[2 * j], cols[2 * j + 1]) for j in range(SSM_HEADS // 2)],
                               axis=1)

    edgeT = cumT[:, 0:1] if reverse else cumT[:, q - 1:q]
    wT = dtT * jnp.exp(edgeT - cumT)
    cum_l = lanes(cum)
    grow = jnp.exp(per_channel(cum_l))
    keep = jnp.exp(per_channel(lanes(edge)))
    xs_b = xs.astype(BF16)
    per_group = SSM_HEADS // SSM_GROUPS
    first_head = lax.broadcasted_iota(jnp.int32, (1, LANES), 1) < SSM_HEAD_DIM
    ys = []
    for g in range(SSM_GROUPS):
        gsl = slice(g * per_group * SSM_HEAD_DIM, (g + 1) * per_group * SSM_HEAD_DIM)
        b_gt = bm[:, g * SSM_D_STATE:(g + 1) * SSM_D_STATE].T
        c_g = cm[:, g * SSM_D_STATE:(g + 1) * SSM_D_STATE].astype(BF16)
        gram = jnp.dot(c_g, b_gt.astype(BF16), preferred_element_type=F32)
        st = st_ref[:, gsl]
        y_off = grow[:, gsl] * jnp.dot(c_g, st.astype(BF16), preferred_element_type=F32)
        upd = []
        for pp in range(per_group // 2):
            psl = slice(pp * LANES, (pp + 1) * LANES)
            x_p = xs_b[:, gsl][:, psl]
            y_pair, u_pair = [], []
            for h in (g * per_group + 2 * pp, g * per_group + 2 * pp + 1):
                seg = cum_l[h] - cumT[h:h + 1, :]
                decay = jnp.exp(jnp.where(causal, seg, -jnp.inf))
                m = (gram * (decay * dtT[h:h + 1, :])).astype(BF16)
                y_pair.append(jnp.dot(m, x_p, preferred_element_type=F32))
                u_pair.append(jnp.dot((b_gt * wT[h:h + 1, :]).astype(BF16), x_p, preferred_element_type=F32))
            ys.append(jnp.where(first_head, y_pair[0], y_pair[1]) + y_off[:, psl])
            upd.append(jnp.where(first_head, u_pair[0], u_pair[1]))
        st_ref[:, gsl] = keep[:, gsl] * st + jnp.concatenate(upd, axis=1)
    return jnp.concatenate(ys, axis=1)


def _ssd_kernel(xf_ref, bf_ref, cf_ref, dtf_ref, dtTf_ref, xb_ref, bb_ref, cb_ref, dtb_ref, dtTb_ref,
                bias_ref, biasT_ref, alog_ref, alogT_ref, st0_ref, yf_ref, yb_ref, stT_ref, st_ref):
    c = pl.program_id(0)

    @pl.when(c == 0)
    def _():
        st_ref[...] = st0_ref[...]

    a_row = -jnp.exp(alog_ref[...])
    a_col = -jnp.exp(alogT_ref[...])
    q = SSD_CHUNK
    n_sub = xf_ref.shape[0] // q
    for k in range(n_sub):
        rows = slice(k * q, (k + 1) * q)
        yf_ref[rows, :] = _ssd_direction(xf_ref[rows, :], bf_ref[rows, :], cf_ref[rows, :],
                                         dtf_ref[rows, 0:SSM_HEADS], dtTf_ref[0:SSM_HEADS, rows],
                                         bias_ref[0:1, :], biasT_ref[:, 0:1], a_row[0:1, :], a_col[:, 0:1],
                                         st_ref.at[0], False)
    for k in reversed(range(n_sub)):
        rows = slice(k * q, (k + 1) * q)
        yb_ref[rows, :] = _ssd_direction(xb_ref[rows, :], bb_ref[rows, :], cb_ref[rows, :],
                                         dtb_ref[rows, SSM_HEADS:2 * SSM_HEADS],
                                         dtTb_ref[SSM_HEADS:2 * SSM_HEADS, rows],
                                         bias_ref[1:2, :], biasT_ref[:, 1:2], a_row[1:2, :], a_col[:, 1:2],
                                         st_ref.at[1], True)

    @pl.when(c == pl.num_programs(0) - 1)
    def _():
        stT_ref[...] = st_ref[...]


def ssd_scan(xbc, dt, dtT, dt_bias, a_log, st0):
    n_rows = xbc.shape[0]
    q = SSD_CHUNK * min(SSD_CHUNKS_PER_STEP, n_rows // SSD_CHUNK)
    nc = n_rows // q
    fwd = lambda c: c
    bwd = lambda c: nc - 1 - c

    def specs(ix):
        return [pl.BlockSpec((q, SSM_D_INNER), lambda c: (ix(c), 0)),
                pl.BlockSpec((q, 2 * SSM_D_STATE), lambda c: (ix(c), 2)),
                pl.BlockSpec((q, 2 * SSM_D_STATE), lambda c: (ix(c), 3)),
                pl.BlockSpec((q, 2 * SSM_HEADS), lambda c: (ix(c), 0)),
                pl.BlockSpec((2 * SSM_HEADS, q), lambda c: (0, ix(c)))]

    small = lambda shape: pl.BlockSpec(shape, lambda c: (0,) * len(shape))
    st_shape = (2, SSM_D_STATE, SSM_D_INNER)
    return pl.pallas_call(
        _ssd_kernel,
        grid=(nc,),
        in_specs=specs(fwd) + specs(bwd) + [small((2, SSM_HEADS)), small((SSM_HEADS, 2)),
                                            small((2, SSM_HEADS)), small((SSM_HEADS, 2)), small(st_shape)],
        out_specs=[pl.BlockSpec((q, SSM_D_INNER), lambda c: (c, 0)),
                   pl.BlockSpec((q, SSM_D_INNER), lambda c: (nc - 1 - c, 0)),
                   small(st_shape)],
        out_shape=[jax.ShapeDtypeStruct((n_rows, SSM_D_INNER), F32),
                   jax.ShapeDtypeStruct((n_rows, SSM_D_INNER), F32),
                   jax.ShapeDtypeStruct(st_shape, F32)],
        scratch_shapes=[pltpu.VMEM(st_shape, F32)],
        compiler_params=_cparams("arbitrary"),
        name="ssd_scan",
    )(xbc, xbc, xbc, dt, dtT, xbc, xbc, xbc, dt, dtT, dt_bias, dt_bias.T, a_log, a_log.T, st0)


def _ssd_finish_kernel(yf_ref, yb_ref, xs_ref, z_ref, d_ref, nw_ref, o_ref):
    hrow = lax.broadcasted_iota(jnp.int32, (SSM_HEADS, SSM_D_INNER), 0)
    hcol = lax.broadcasted_iota(jnp.int32, (SSM_HEADS, SSM_D_INNER), 1) // SSM_HEAD_DIM
    d_e = jnp.sum(jnp.where(hrow == hcol, d_ref[...], 0.0), axis=0, keepdims=True)
    y = yf_ref[...] + yb_ref[...] + d_e * xs_ref[...]
    y = y * _silu(z_ref[...])
    o_ref[...] = (y * lax.rsqrt(jnp.mean(y * y, axis=-1, keepdims=True) + NORM_EPS) * nw_ref[...]).astype(BF16)


def ssd_finish(yf, yb, xbc, p, z_col0, d_skip, norm_w):
    n_rows = yf.shape[0]
    tm = min(n_rows, ROWWISE_TM)
    zb = z_col0 // SSM_D_INNER
    blk = lambda cb: pl.BlockSpec((tm, SSM_D_INNER), lambda i: (i, cb))
    return pl.pallas_call(
        _ssd_finish_kernel,
        grid=(n_rows // tm,),
        in_specs=[blk(0), blk(0), blk(0), blk(zb),
                  pl.BlockSpec((SSM_HEADS, 1), lambda i: (0, 0)),
                  pl.BlockSpec((1, SSM_D_INNER), lambda i: (0, 0))],
        out_specs=blk(0),
        out_shape=jax.ShapeDtypeStruct((n_rows, SSM_D_INNER), BF16),
        compiler_params=_cparams("parallel"),
        name="ssd_finish",
    )(yf, yb, xbc, p, d_skip.reshape(SSM_HEADS, 1), norm_w.reshape(1, SSM_D_INNER))


LRU_X_COL = 2560
LRU_G_COL = 3072
SSM_Z_COL = 3584
SSM_XBC_COL = 4096


def _mixers(p, dt, qkv_q, qkv_c, qkv_l, lw, lam_init, lru_h0, ssd_st0):
    ya = diff_attention(qkv_q, qkv_c, qkv_l, lw["da_lambda"], lw["da_subln"], lam_init)
    yb = gqa_attention(qkv_q, qkv_c, qkv_l)
    yc, lru_hT = rglru(p, LRU_X_COL, LRU_G_COL, lw["lru_conv_w"], lw["lru_conv_b"], lw["lru_w_gates"],
                       lw["lru_b_gates"], lw["lru_lambda"], lru_h0)
    xbc = dwconv(p, SSM_XBC_COL, 2 * SSM_D_INNER, lw["ssm_conv_w"], lw["ssm_conv_b"], act=True)
    yf, ybk, ssd_stT = ssd_scan(xbc, dt, dt.T, lw["ssm_dt_bias"], lw["ssm_a_log"], ssd_st0)
    yd = ssd_finish(yf, ybk, xbc, p, SSM_Z_COL, lw["ssm_d"], lw["ssm_norm"])
    return [ya, yb, yc, yd], lru_hT, ssd_stT


def kernel(x, c, ctx, c_ctx, w_mod, b_mod, mix_norm, ffn_norm, w_in, w_out, da_lambda, da_subln, gqa_q_norm,
           gqa_k_norm, lru_conv_w, lru_conv_b, lru_w_gates, lru_b_gates, lru_lambda, ssm_conv_w, ssm_conv_b,
           ssm_dt_bias, ssm_a_log, ssm_d, ssm_norm, ffn_w_up, ffn_conv_w, ffn_conv_b, ffn_w_down, final_norm):
    depth = w_mod.shape[0]
    d = x.shape[-1]
    xl = x[0]
    xc = ctx[0]
    n_lat, n_ctx = xl.shape[0], xc.shape[0]

    cond8 = jnp.zeros((8, d), F32).at[0].set(c[0]).at[1].set(c_ctx)
    mods = modulation_all(cond8, w_mod, b_mod)
    tabs_l = _rope_tables(n_lat, DA_HEAD_DIM) + _rope_tables(n_lat, GQA_HEAD_DIM)
    tabs_c = _rope_tables(n_ctx, DA_HEAD_DIM) + _rope_tables(n_ctx, GQA_HEAD_DIM)
    lru_zero = jnp.zeros((2, LRU_WIDTH), F32)
    ssd_zero = jnp.zeros((2, SSM_D_STATE, SSM_D_INNER), F32)
    w_in_t = jnp.swapaxes(w_in, 1, 2)

    for layer in range(depth):
        need_ctx = layer < depth - 1
        lam_init = 0.8 - 0.6 * math.exp(-0.3 * layer)
        ml = [mods[layer, 0:1, k * d:(k + 1) * d] for k in range(6)]
        mc = [mods[layer, 1:2, k * d:(k + 1) * d] for k in range(6)]
        lw = dict(da_lambda=da_lambda[layer], da_subln=da_subln[layer],
                  lru_conv_w=lru_conv_w[layer], lru_conv_b=lru_conv_b[layer], lru_w_gates=lru_w_gates[layer],
                  lru_b_gates=lru_b_gates[layer], lru_lambda=lru_lambda[layer], ssm_conv_w=ssm_conv_w[layer],
                  ssm_conv_b=ssm_conv_b[layer], ssm_dt_bias=ssm_dt_bias[layer], ssm_a_log=ssm_a_log[layer],
                  ssm_d=ssm_d[layer], ssm_norm=ssm_norm[layer])
        nw_m = mix_norm[layer].reshape(1, d)
        nw_f = ffn_norm[layer].reshape(1, d)

        xn_c, dt_c = adaln(xc, nw_m, mc[0], mc[1], w_in_t, layer)
        xn_l, dt_l = adaln(xl, nw_m, ml[0], ml[1], w_in_t, layer)
        p_c, p_l = in_proj(xn_c, xn_l, w_in_t, layer)
        qkv_c = attn_prep(p_c, tabs_c, gqa_q_norm[layer], gqa_k_norm[layer], rope=False)
        qkv_l = attn_prep(p_l, tabs_l, gqa_q_norm[layer], gqa_k_norm[layer], rope=True)

        y_c, lru_h, ssd_st = _mixers(p_c, dt_c, qkv_c, qkv_c, None, lw, lam_init, lru_zero, ssd_zero)
        y_l, _, _ = _mixers(p_l, dt_l, qkv_l, qkv_c, qkv_l, lw, lam_init, lru_h, ssd_st)

        conv_w, conv_b = ffn_conv_w[layer], ffn_conv_b[layer]
        if need_ctx:
            xc, xl = resid_proj(w_out, layer, (y_l, xl, ml[2]), (y_c, xc, mc[2]))
            act_c, act_l = ffn_up(ffn_w_up, layer, conv_w, conv_b, adaln(xl, nw_f, ml[3], ml[4]),
                                  adaln(xc, nw_f, mc[3], mc[4]))
            xc, xl = resid_proj(ffn_w_down, layer, ([act_l], xl, ml[5]), ([act_c], xc, mc[5]))
        else:
            xl = resid_proj(w_out, layer, (y_l, xl, ml[2]))
            act_l = ffn_up(ffn_w_up, layer, conv_w, conv_b, adaln(xl, nw_f, ml[3], ml[4]))
            xl = resid_proj(ffn_w_down, layer, ([act_l], xl, ml[5]))

    return final_rmsnorm(xl, final_norm)[None]
```

```python
import functools
import math

import jax
import jax.numpy as jnp
from jax import lax
from jax.experimental import pallas as pl
from jax.experimental.pallas import tpu as pltpu

F32 = jnp.float32
BF16 = jnp.bfloat16

GRID_W = 64
GROUP_WIDTH = 512
DA_HEAD_DIM = 64
DA_HEADS = 4
GQA_HEAD_DIM = 128
GQA_HEADS = 4
GQA_KV_HEADS = 2
LRU_WIDTH = 512
LRU_BLOCKS = 4
RGLRU_C = 8.0
SSM_D_INNER = 512
SSM_HEAD_DIM = 64
SSM_HEADS = 8
SSM_GROUPS = 2
SSM_D_STATE = 128
SSD_CHUNK = 128
D_FF = 5632
ROPE_THETA = 10000.0
NORM_EPS = 1e-6
LOG2E = math.log2(math.e)

LANES = 128
SUBLANES = 8
HALO = 16
IN_COLS = 5136
IN_MAIN_COLS = 5120
N_DT = IN_COLS - IN_MAIN_COLS
ATTN_COLS = 2560
MIXER_COLS = IN_MAIN_COLS - ATTN_COLS
ATTN_GROUP_IN = 4 * LANES
ATTN_GROUP_OUT = 8 * LANES
N_ATTN_GROUPS = ATTN_COLS // ATTN_GROUP_IN
QA_BLK, QB_BLK, KD_BLK, VD_BLK = 0, 4, 8, 16
QG_BLK, KG_BLK, VG_BLK = 24, 32, 34
QKV_COLS = N_ATTN_GROUPS * ATTN_GROUP_OUT

VMEM_LIMIT = 56 * 1024 * 1024
PROJ_TM = 1024
IN_TN = 1024
MIXER_TN = 1280
RESID_TM = 512
RESID_TN_MAX = 1024
RESID_W_ELEMS = 2048 * 1024
ROWWISE_TM = 1024
FFN_TN = 512


def _cparams(*sem):
    return pltpu.CompilerParams(dimension_semantics=sem, vmem_limit_bytes=VMEM_LIMIT)


def _silu(x):
    return x * jax.nn.sigmoid(x)


def _norm_mod(x, nw, shift, scale):
    y = x * lax.rsqrt(jnp.mean(x * x, axis=-1, keepdims=True) + NORM_EPS) * nw
    return y * (1.0 + scale) + shift


def _mod_kernel(c_ref, w_ref, b_ref, o_ref):
    s = _silu(c_ref[...]).astype(BF16)
    o_ref[...] = jnp.dot(s, w_ref[...].astype(BF16), preferred_element_type=F32) + b_ref[...]


def modulation_all(cond8, w_mod, b_mod):
    depth, d, n = w_mod.shape
    tn = IN_TN
    return pl.pallas_call(
        _mod_kernel,
        grid=(depth, n // tn),
        in_specs=[
            pl.BlockSpec((8, d), lambda l, j: (0, 0)),
            pl.BlockSpec((None, d, tn), lambda l, j: (l, 0, j)),
            pl.BlockSpec((None, 1, tn), lambda l, j: (l, 0, j)),
        ],
        out_specs=pl.BlockSpec((None, 8, tn), lambda l, j: (l, 0, j)),
        out_shape=jax.ShapeDtypeStruct((depth, 8, n), F32),
        compiler_params=_cparams("parallel", "parallel"),
        name="modulation",
    )(cond8, w_mod, b_mod.reshape(depth, 1, n))


ADALN_ROWS = 16
_NT = (((1,), (1,)), ((), ()))


def _adaln_kernel(*refs, with_dt):
    if with_dt:
        x_ref, nw_ref, sh_ref, sc_ref, wdt_ref, o_ref, dt_ref = refs
    else:
        x_ref, nw_ref, sh_ref, sc_ref, o_ref = refs
    gain = nw_ref[...] * (1.0 + sc_ref[...])
    shift = sh_ref[...]
    d = x_ref.shape[1]

    def body(r, _):
        rows = pl.ds(pl.multiple_of(r * ADALN_ROWS, ADALN_ROWS), ADALN_ROWS)
        x = x_ref[rows, :]
        inv = lax.rsqrt(jnp.sum(x * x, axis=-1, keepdims=True) * (1.0 / d) + NORM_EPS)
        o_ref[rows, :] = (x * inv * gain + shift).astype(BF16)
        return 0

    lax.fori_loop(0, x_ref.shape[0] // ADALN_ROWS, body, 0, unroll=4)
    if with_dt:
        dt_ref[...] = lax.dot_general(o_ref[...], wdt_ref[...].astype(BF16), _NT, preferred_element_type=F32)


def adaln(x, nw, shift, scale, w_in_t=None, layer=None):
    m, d = x.shape
    tm = min(m, PROJ_TM)
    with_dt = w_in_t is not None
    vec = pl.BlockSpec((1, d), lambda i: (0, 0))
    rows = pl.BlockSpec((tm, d), lambda i: (i, 0))
    in_specs, args = [rows, vec, vec, vec], [x, nw, shift, scale]
    out_specs, out_shape = rows, jax.ShapeDtypeStruct((m, d), BF16)
    if with_dt:
        in_specs.append(pl.BlockSpec((None, N_DT, d), lambda i: (layer, IN_MAIN_COLS // N_DT, 0)))
        args.append(w_in_t)
        out_specs = [rows, pl.BlockSpec((tm, N_DT), lambda i: (i, 0))]
        out_shape = [out_shape, jax.ShapeDtypeStruct((m, N_DT), F32)]
    return pl.pallas_call(
        functools.partial(_adaln_kernel, with_dt=with_dt),
        grid=(m // tm,),
        in_specs=in_specs,
        out_specs=out_specs,
        out_shape=out_shape,
        compiler_params=_cparams("parallel"),
        name="adaln",
    )(*args)


def _lat_block(i):
    return jnp.maximum(i - 1, 0)


def _inproj_kernel(xc_ref, xl_ref, w_ref, oc_ref, ol_ref, wb_ref):
    i = pl.program_id(1)

    @pl.when(i == 0)
    def _():
        wb_ref[...] = w_ref[...].astype(BF16)
        oc_ref[...] = lax.dot_general(xc_ref[...], wb_ref[...], _NT, preferred_element_type=F32)
        ol_ref[...] = jnp.zeros_like(ol_ref)

    @pl.when(i > 0)
    def _():
        ol_ref[...] = lax.dot_general(xl_ref[...], wb_ref[...], _NT, preferred_element_type=F32)


def in_proj(xn_c, xn_l, w_in_t, layer):
    (mc, d), ml = xn_c.shape, xn_l.shape[0]
    tm = min(ml, PROJ_TM)
    first = ATTN_COLS // MIXER_TN
    return pl.pallas_call(
        _inproj_kernel,
        grid=(MIXER_COLS // MIXER_TN, 1 + ml // tm),
        in_specs=[pl.BlockSpec((mc, d), lambda j, i: (0, 0)),
                  pl.BlockSpec((tm, d), lambda j, i: (_lat_block(i), 0)),
                  pl.BlockSpec((None, MIXER_TN, d), lambda j, i: (layer, first + j, 0))],
        out_specs=[pl.BlockSpec((mc, MIXER_TN), lambda j, i: (0, j)),
                   pl.BlockSpec((tm, MIXER_TN), lambda j, i: (_lat_block(i), j))],
        out_shape=[jax.ShapeDtypeStruct((mc, MIXER_COLS), F32), jax.ShapeDtypeStruct((ml, MIXER_COLS), F32)],
        scratch_shapes=[pltpu.VMEM((MIXER_TN, d), BF16)],
        compiler_params=_cparams("parallel", "arbitrary"),
        name="in_proj",
    )(xn_c, xn_l, w_in_t)


def _resid_kernel(*refs, n_y, with_ctx):
    refs = list(refs)
    ctx_refs = None
    if with_ctx:
        ctx_refs, refs = refs[:n_y + 2], refs[n_y + 2:]
    lat_refs, refs = refs[:n_y + 2], refs[n_y + 2:]
    w_ref = refs.pop(0)
    oc_ref = refs.pop(0) if with_ctx else None
    ol_ref, wb_ref = refs
    i = pl.program_id(1)
    kk = wb_ref.shape[0] // n_y

    def project(stream, o_ref):
        ys, x_ref, g_ref = stream[:n_y], stream[n_y], stream[n_y + 1]
        acc = functools.reduce(jnp.add, [
            jnp.dot(ys[a][...], wb_ref[a * kk:(a + 1) * kk, :], preferred_element_type=F32) for a in range(n_y)])
        o_ref[...] = x_ref[...] + g_ref[...] * acc

    @pl.when(i == 0)
    def _():
        wb_ref[...] = w_ref[...].astype(BF16)
        if with_ctx:
            project(ctx_refs, oc_ref)
            ol_ref[...] = jnp.zeros_like(ol_ref)
        else:
            project(lat_refs, ol_ref)

    @pl.when(i > 0)
    def _():
        project(lat_refs, ol_ref)


def resid_proj(w_all, layer, lat, ctx=None):
    ys_l, x_l, gate_l = lat
    ml = x_l.shape[0]
    _, k, n = w_all.shape
    n_y = len(ys_l)
    kk = k // n_y
    tm = min(ml, RESID_TM)
    tn = RESID_TN_MAX if k * RESID_TN_MAX <= RESID_W_ELEMS else RESID_TN_MAX // 2
    with_ctx = ctx is not None
    lat_blk = _lat_block if with_ctx else (lambda i: i)

    def stream_specs(rows, blk):
        return ([pl.BlockSpec((rows, kk), lambda j, i: (blk(i), 0)) for _ in range(n_y)]
                + [pl.BlockSpec((rows, tn), lambda j, i: (blk(i), j)), pl.BlockSpec((1, tn), lambda j, i: (0, j))])

    in_specs = stream_specs(tm, lat_blk) + [pl.BlockSpec((None, k, tn), lambda j, i: (layer, 0, j))]
    args = [*ys_l, x_l, gate_l, w_all]
    out_specs = [pl.BlockSpec((tm, tn), lambda j, i: (lat_blk(i), j))]
    out_shape = [jax.ShapeDtypeStruct((ml, n), F32)]
    if with_ctx:
        ys_c, x_c, gate_c = ctx
        mc = x_c.shape[0]
        in_specs = stream_specs(mc, lambda i: 0) + in_specs
        args = [*ys_c, x_c, gate_c] + args
        out_specs = [pl.BlockSpec((mc, tn), lambda j, i: (0, j))] + out_specs
        out_shape = [jax.ShapeDtypeStruct((mc, n), F32)] + out_shape
    out = pl.pallas_call(
        functools.partial(_resid_kernel, n_y=n_y, with_ctx=with_ctx),
        grid=(n // tn, ml // tm + int(with_ctx)),
        in_specs=in_specs,
        out_specs=out_specs,
        out_shape=out_shape,
        scratch_shapes=[pltpu.VMEM((k, tn), BF16)],
        compiler_params=_cparams("parallel", "arbitrary"),
        name="resid_proj",
    )(*args)
    return tuple(out) if with_ctx else out[0]


def _ffn_up_kernel(*refs, n_lat_blocks, with_ctx):
    refs = list(refs)
    xc_ref = refs.pop(0) if with_ctx else None
    xp_ref, x_ref, xx_ref, wg_ref, wu_ref, cw_ref, cb_ref = refs[:7]
    del refs[:7]
    oc_ref = refs.pop(0) if with_ctx else None
    o_ref, xn_ref, g_ref, wb_ref = refs
    i = pl.program_id(1)
    blk = (i - 1) if with_ctx else i

    def up_block(prev, cur_ref, nxt, out_ref):
        n = cur_ref.shape[0]
        xn_ref[pl.ds(0, HALO), :] = prev
        xn_ref[pl.ds(HALO, n), :] = cur_ref[...]
        xn_ref[pl.ds(HALO + n, HALO), :] = nxt
        g_ref[pl.ds(0, n + 2 * HALO), :] = jnp.dot(xn_ref[pl.ds(0, n + 2 * HALO), :], wb_ref[0],
                                                   preferred_element_type=F32)
        up = jnp.dot(cur_ref[...], wb_ref[1], preferred_element_type=F32)
        cw = cw_ref[...]
        gc = (cb_ref[...] + cw[0:1] * g_ref[pl.ds(HALO - 1, n), :] + cw[1:2] * g_ref[pl.ds(HALO, n), :]
              + cw[2:3] * g_ref[pl.ds(HALO + 1, n), :])
        out_ref[...] = (_silu(gc) * up).astype(BF16)

    def latent_block():
        zeros = jnp.zeros_like(xp_ref)
        up_block(jnp.where(blk == 0, zeros, xp_ref[...]), x_ref,
                 jnp.where(blk == n_lat_blocks - 1, zeros, xx_ref[...]), o_ref)

    @pl.when(i == 0)
    def _():
        wb_ref[0] = wg_ref[...].astype(BF16)
        wb_ref[1] = wu_ref[...].astype(BF16)
        if with_ctx:
            zeros = jnp.zeros_like(xp_ref)
            up_block(zeros, xc_ref, zeros, oc_ref)
            o_ref[...] = jnp.zeros_like(o_ref)
        else:
            latent_block()

    @pl.when(i > 0)
    def _():
        latent_block()


def ffn_up(w_up_all, layer, conv_w, conv_b, xn_l, xn_c=None):
    ml, d = xn_l.shape
    dff = w_up_all.shape[2] // 2
    tm = min(ml, PROJ_TM)
    tn = FFN_TN
    nrb = ml // tm
    hb = tm // HALO
    nhb = ml // HALO
    with_ctx = xn_c is not None
    blk = _lat_block if with_ctx else (lambda i: i)
    in_specs = [pl.BlockSpec((HALO, d), lambda j, i: (jnp.maximum(blk(i) * hb - 1, 0), 0)),
                pl.BlockSpec((tm, d), lambda j, i: (blk(i), 0)),
                pl.BlockSpec((HALO, d), lambda j, i: (jnp.minimum((blk(i) + 1) * hb, nhb - 1), 0)),
                pl.BlockSpec((None, d, tn), lambda j, i: (layer, 0, j)),
                pl.BlockSpec((None, d, tn), lambda j, i: (layer, 0, j + dff // tn)),
                pl.BlockSpec((3, tn), lambda j, i: (0, j)),
                pl.BlockSpec((1, tn), lambda j, i: (0, j))]
    args = [xn_l, xn_l, xn_l, w_up_all, w_up_all, conv_w, conv_b.reshape(1, dff)]
    out_specs = [pl.BlockSpec((tm, tn), lambda j, i: (blk(i), j))]
    out_shape = [jax.ShapeDtypeStruct((ml, dff), BF16)]
    if with_ctx:
        mc = xn_c.shape[0]
        in_specs = [pl.BlockSpec((mc, d), lambda j, i: (0, 0))] + in_specs
        args = [xn_c] + args
        out_specs = [pl.BlockSpec((mc, tn), lambda j, i: (0, j))] + out_specs
        out_shape = [jax.ShapeDtypeStruct((mc, dff), BF16)] + out_shape
    out = pl.pallas_call(
        functools.partial(_ffn_up_kernel, n_lat_blocks=nrb, with_ctx=with_ctx),
        grid=(dff // tn, nrb + int(with_ctx)),
        in_specs=in_specs,
        out_specs=out_specs,
        out_shape=out_shape,
        scratch_shapes=[pltpu.VMEM((tm + 2 * HALO, d), BF16), pltpu.VMEM((tm + 2 * HALO, tn), F32),
                        pltpu.VMEM((2, d, tn), BF16)],
        compiler_params=_cparams("parallel", "arbitrary"),
        name="ffn_up",
    )(*args)
    return tuple(out) if with_ctx else out[0]


def _rmsnorm_kernel(x_ref, w_ref, o_ref):
    x = x_ref[...]
    o_ref[...] = x * lax.rsqrt(jnp.mean(x * x, axis=-1, keepdims=True) + NORM_EPS) * w_ref[...]


def final_rmsnorm(x, w):
    m, d = x.shape
    tm = min(m, ROWWISE_TM)
    return pl.pallas_call(
        _rmsnorm_kernel,
        grid=(m // tm,),
        in_specs=[pl.BlockSpec((tm, d), lambda i: (i, 0)), pl.BlockSpec((1, d), lambda i: (0, 0))],
        out_specs=pl.BlockSpec((tm, d), lambda i: (i, 0)),
        out_shape=jax.ShapeDtypeStruct((m, d), F32),
        compiler_params=_cparams("parallel"),
        name="final_norm",
    )(x, w.reshape(1, d))


def _rope_tables(n_tok, head_dim):
    q = head_dim // 4
    lane = jnp.arange(LANES, dtype=jnp.int32)
    u = lane % head_dim
    region = u // q
    freqs = jnp.power(ROPE_THETA, -(u % q).astype(F32) / q)[None, :]
    ang_r = jnp.arange(n_tok // GRID_W, dtype=jnp.int32).astype(F32)[:, None] * freqs
    ang_c = jnp.arange(GRID_W, dtype=jnp.int32).astype(F32)[:, None] * freqs
    by_row = (region < 2)[None, None, :]

    def per_token(f):
        return jnp.where(by_row, f(ang_r)[:, None, :], f(ang_c)[None, :, :]).reshape(n_tok, LANES)

    cos, sin = per_token(jnp.cos), per_token(jnp.sin)
    first = (region % 2 == 0)[None, :]
    return cos, jnp.where(first, -sin, 0.0), jnp.where(first, 0.0, sin)


def _prep_group(group, blocks, o_ref, tables, qn_ref, kn_ref, rope):
    cd_ref, ad_ref, bd_ref, cg_ref, ag_ref, bg_ref = tables
    lo = lax.broadcasted_iota(jnp.int32, (1, LANES), 1) < DA_HEAD_DIM
    qd = DA_HEAD_DIM // 4
    qg = GQA_HEAD_DIM // 4

    def rot(x, cos, sa, sb, quarter):
        if not rope:
            return x
        return (x * cos[...] + pltpu.roll(x, LANES - quarter, 1) * sa[...]
                + pltpu.roll(x, quarter, 1) * sb[...])

    def rms(x, w):
        return x * lax.rsqrt(jnp.mean(x * x, axis=-1, keepdims=True) + NORM_EPS) * w[...]

    ones = jnp.ones_like(blocks[0])
    zero = jnp.zeros_like(blocks[0])
    if group == 0:
        qs = [rot(b, cd_ref, ad_ref, bd_ref, qd) * (DA_HEAD_DIM ** -0.5 * LOG2E) for b in blocks]
        out = [jnp.where(lo, q, 0.0) for q in qs] + [jnp.where(lo, 0.0, q) for q in qs]
    elif group == 1:
        out = [rot(b, cd_ref, ad_ref, bd_ref, qd) for b in blocks] + [zero] * 4
    elif group == 2:
        out = [v for b in blocks for v in (b, ones)]
    elif group == 3:
        out = [rot(rms(b, qn_ref), cg_ref, ag_ref, bg_ref, qg) * (GQA_HEAD_DIM ** -0.5 * LOG2E)
               for b in blocks] + [zero] * 4
    else:
        out = ([rot(rms(b, kn_ref), cg_ref, ag_ref, bg_ref, qg) for b in blocks[:GQA_KV_HEADS]]
               + [v for b in blocks[GQA_KV_HEADS:] for v in (b, ones)] + [zero] * 2)
    for t, v in enumerate(out):
        o_ref[:, t * LANES:(t + 1) * LANES] = v.astype(BF16)


def _inproj_attn_kernel(xc_ref, xl_ref, w_ref, cd_ref, ad_ref, bd_ref, cg_ref, ag_ref, bg_ref, qn_ref, kn_ref,
                        oc_ref, ol_ref, wb_ref):
    group = pl.program_id(0)
    i = pl.program_id(1)
    tables = (cd_ref, ad_ref, bd_ref, cg_ref, ag_ref, bg_ref)

    def project(x_ref, o_ref, rope):
        p = lax.dot_general(x_ref[...], wb_ref[...], _NT, preferred_element_type=F32)
        blocks = [p[:, t * LANES:(t + 1) * LANES] for t in range(ATTN_GROUP_IN // LANES)]
        for g in range(N_ATTN_GROUPS):
            @pl.when(group == g)
            def _():
                _prep_group(g, blocks, o_ref, tables, qn_ref, kn_ref, rope)

    @pl.when(i == 0)
    def _():
        wb_ref[...] = w_ref[...].astype(BF16)
        project(xc_ref, oc_ref, False)
        ol_ref[...] = jnp.zeros_like(ol_ref)

    @pl.when(i > 0)
    def _():
        project(xl_ref, ol_ref, True)


def in_proj_attn(xn_c, xn_l, w_in_t, layer, tables, q_norm, k_norm):
    (mc, d), ml = xn_c.shape, xn_l.shape[0]
    tm = min(ml, PROJ_TM)
    tab = pl.BlockSpec((tm, LANES), lambda g, i: (_lat_block(i), 0))
    vec = pl.BlockSpec((1, LANES), lambda g, i: (0, 0))
    return pl.pallas_call(
        _inproj_attn_kernel,
        grid=(N_ATTN_GROUPS, 1 + ml // tm),
        in_specs=[pl.BlockSpec((mc, d), lambda g, i: (0, 0)),
                  pl.BlockSpec((tm, d), lambda g, i: (_lat_block(i), 0)),
                  pl.BlockSpec((None, ATTN_GROUP_IN, d), lambda g, i: (layer, g, 0))] + [tab] * 6 + [vec, vec],
        out_specs=[pl.BlockSpec((mc, ATTN_GROUP_OUT), lambda g, i: (0, g)),
                   pl.BlockSpec((tm, ATTN_GROUP_OUT), lambda g, i: (_lat_block(i), g))],
        out_shape=[jax.ShapeDtypeStruct((mc, QKV_COLS), BF16), jax.ShapeDtypeStruct((ml, QKV_COLS), BF16)],
        scratch_shapes=[pltpu.VMEM((ATTN_GROUP_IN, d), BF16)],
        compiler_params=_cparams("parallel", "arbitrary"),
        name="in_proj_attn",
    )(xn_c, xn_l, w_in_t, *tables, q_norm.reshape(1, LANES), k_norm.reshape(1, LANES))


def _attn_kernel(*refs, tk, n_kv, diff, lam_init):
    refs = list(refs)
    qa_ref, qb_ref, kc_ref, vc_ref = refs[:4]
    del refs[:4]
    if n_kv:
        kl_ref, vl_ref = refs[:2]
        del refs[:2]
    if diff:
        lam_ref, sw_ref = refs[:2]
        del refs[:2]
    o_ref = refs.pop(0)
    if n_kv:
        s_ref = refs.pop(0)
    sc_ref, m_ref, acc_ref = refs
    tq = qa_ref.shape[0]
    nt = (((1,), (1,)), ((), ()))
    q2 = jnp.concatenate([qa_ref[...], qb_ref[...]], axis=0)

    def lane_tiles(x):
        return [x[:, t * LANES:(t + 1) * LANES] for t in range(x.shape[1] // LANES)]

    def key_rows(ref, c):
        return ref[pl.ds(pl.multiple_of(c * tk, tk), tk), :]

    def chunks(body):
        if n_kv:
            lax.fori_loop(0, n_kv, lambda c, _: body(c) or 0, 0)

    sc = lax.dot_general(q2, kc_ref[...], nt, preferred_element_type=F32)
    sc_ref[...] = sc
    m_ref[...] = functools.reduce(jnp.maximum, lane_tiles(sc))

    def score_chunk(c):
        s = lax.dot_general(q2, key_rows(kl_ref, c), nt, preferred_element_type=F32)
        s_ref[c] = s
        m_ref[...] = functools.reduce(jnp.maximum, lane_tiles(s), m_ref[...])

    chunks(score_chunk)
    m_ref[...] = jnp.broadcast_to(jnp.max(m_ref[...], axis=-1, keepdims=True), (2 * tq, LANES))

    def weigh(s, vs):
        m = m_ref[...]
        p = jnp.concatenate([jnp.exp2(st - m).astype(BF16) for st in lane_tiles(s)], axis=1)
        return jnp.dot(p, vs, preferred_element_type=F32)

    acc_ref[...] = weigh(sc_ref[...], vc_ref[...])

    def weigh_chunk(c):
        acc_ref[...] += weigh(s_ref[c], key_rows(vl_ref, c))

    chunks(weigh_chunk)
    o = acc_ref[:, 0:LANES] / acc_ref[:, LANES:2 * LANES]
    oa = o[0:tq]
    ob = o[tq:2 * tq]
    if diff:
        lv = lam_ref[...]
        lam = (jnp.exp(jnp.sum(lv[0:1] * lv[1:2], keepdims=True))
               - jnp.exp(jnp.sum(lv[2:3] * lv[3:4], keepdims=True)) + lam_init)
        o = oa - lam * ob
        o = o * lax.rsqrt(jnp.mean(o * o, axis=-1, keepdims=True) + NORM_EPS) * sw_ref[...]
        o_ref[...] = (o * (1.0 - lam_init)).astype(BF16)
    else:
        o_ref[:, 0:LANES] = oa.astype(BF16)
        o_ref[:, LANES:2 * LANES] = ob.astype(BF16)


ATTN_TQ = 512
ATTN_TK = (4096, 2048, 1024, 512, 256, 128)


def _attention(qkv_q, qkv_c, qkv_l, extra, extra_specs, cols, n_heads, out_width, name, **static):
    qa_col, qb_col, k_col, v_col = cols
    sq, n_ctx = qkv_q.shape[0], qkv_c.shape[0]
    tq = min(sq, ATTN_TQ)
    n_lat = 0 if qkv_l is None else qkv_l.shape[0]
    tk = next((t for t in ATTN_TK if n_lat and n_lat % t == 0), 0)
    n_kv = n_lat // tk if n_lat else 0

    def kv_specs(rows):
        return [pl.BlockSpec((rows, LANES), lambda h, i: (0, k_col(h))),
                pl.BlockSpec((rows, 2 * LANES), lambda h, i: (0, v_col(h)))]

    in_specs = [pl.BlockSpec((tq, LANES), lambda h, i: (i, qa_col(h))),
                pl.BlockSpec((tq, LANES), lambda h, i: (i, qb_col(h)))] + kv_specs(n_ctx)
    args = [qkv_q, qkv_q, qkv_c, qkv_c]
    scratch = [pltpu.VMEM((2 * tq, n_ctx), F32), pltpu.VMEM((2 * tq, LANES), F32),
               pltpu.VMEM((2 * tq, 2 * LANES), F32)]
    if n_kv:
        in_specs += kv_specs(n_lat)
        args += [qkv_l, qkv_l]
        scratch = [pltpu.VMEM((n_kv, 2 * tq, tk), F32)] + scratch
    return pl.pallas_call(
        functools.partial(_attn_kernel, tk=tk, n_kv=n_kv, **static),
        grid=(n_heads, sq // tq),
        in_specs=in_specs + extra_specs,
        out_specs=pl.BlockSpec((tq, out_width), lambda h, i: (i, h)),
        out_shape=jax.ShapeDtypeStruct((sq, GROUP_WIDTH), BF16),
        scratch_shapes=scratch,
        compiler_params=_cparams("parallel", "parallel"),
        name=name,
    )(*args, *extra)


def diff_attention(qkv_q, qkv_c, qkv_l, da_lambda, subln_w, lam_init):
    cols = (lambda h: QA_BLK + h, lambda h: QB_BLK + h, lambda h: KD_BLK + h, lambda h: VD_BLK // 2 + h)
    extra_specs = [pl.BlockSpec((4, DA_HEAD_DIM), lambda h, i: (0, 0)), pl.BlockSpec((1, LANES), lambda h, i: (0, 0))]
    return _attention(qkv_q, qkv_c, qkv_l, [da_lambda, subln_w.reshape(1, LANES)], extra_specs, cols,
                      DA_HEADS, LANES, "diff_attn", diff=True, lam_init=lam_init)


def gqa_attention(qkv_q, qkv_c, qkv_l):
    cols = (lambda h: QG_BLK + 2 * h, lambda h: QG_BLK + 2 * h + 1, lambda h: KG_BLK + h,
            lambda h: VG_BLK // 2 + h)
    return _attention(qkv_q, qkv_c, qkv_l, [], [], cols, GQA_KV_HEADS, 2 * LANES, "gqa_attn",
                      diff=False, lam_init=0.0)


CONV_CHUNK = 256


def _conv_chunk(x_ref, base, n_rows, chunk, cw, cb):
    cur = x_ref[pl.ds(base, chunk), :]
    prev = x_ref[pl.ds(pl.multiple_of(jnp.maximum(base - 8, 0), 8), 8), :]
    nxt = x_ref[pl.ds(pl.multiple_of(jnp.minimum(base + chunk, n_rows - 8), 8), 8), :]
    prev = jnp.where(base == 0, 0.0, prev)
    nxt = jnp.where(base + chunk >= n_rows, 0.0, nxt)
    cat = jnp.concatenate([prev, cur, nxt], axis=0)
    n = chunk + 16
    xm2 = pltpu.roll(cat, 2, 0)[8:8 + chunk]
    xm1 = pltpu.roll(cat, 1, 0)[8:8 + chunk]
    xp1 = pltpu.roll(cat, n - 1, 0)[8:8 + chunk]
    return cb + cw[0:1] * xm2 + cw[1:2] * xm1 + cw[2:3] * cur + cw[3:4] * xp1


def _dwconv_kernel(x_ref, w_ref, b_ref, o_ref, *, n_rows, chunk, act):
    cw = w_ref[...]
    cb = b_ref[...]

    def body(c, _):
        base = pl.multiple_of(c * chunk, chunk)
        y = _conv_chunk(x_ref, base, n_rows, chunk, cw, cb)
        o_ref[pl.ds(base, chunk), :] = _silu(y) if act else y
        return 0

    lax.fori_loop(0, n_rows // chunk, body, 0)


def dwconv(p, col0, width, conv_w, conv_b, act):
    n_rows = p.shape[0]
    chunk = min(n_rows, CONV_CHUNK)
    b0 = col0 // LANES
    return pl.pallas_call(
        functools.partial(_dwconv_kernel, n_rows=n_rows, chunk=chunk, act=act),
        grid=(width // LANES,),
        in_specs=[pl.BlockSpec((n_rows, LANES), lambda j: (0, b0 + j)),
                  pl.BlockSpec((4, LANES), lambda j: (0, j)),
                  pl.BlockSpec((1, LANES), lambda j: (0, j))],
        out_specs=pl.BlockSpec((n_rows, LANES), lambda j: (0, j)),
        out_shape=jax.ShapeDtypeStruct((n_rows, width), F32),
        compiler_params=_cparams("parallel"),
        name="dwconv",
    )(p, conv_w, conv_b.reshape(1, width))


LRU_CHUNK = 256


def _lru_scan_chunk(a, u, h_in, reverse):
    n = a.shape[0]
    row = lax.broadcasted_iota(jnp.int32, (n, 1), 0)
    for k in (1, 2, 4):
        if reverse:
            keep = row < n - k
            a_s = jnp.where(keep, pltpu.roll(a, n - k, 0), 1.0)
            u_s = jnp.where(keep, pltpu.roll(u, n - k, 0), 0.0)
        else:
            keep = row >= k
            a_s = jnp.where(keep, pltpu.roll(a, k, 0), 1.0)
            u_s = jnp.where(keep, pltpu.roll(u, k, 0), 0.0)
        u = u + a * u_s
        a = a * a_s
    n_tiles = n // SUBLANES
    order = range(n_tiles - 1, -1, -1) if reverse else range(n_tiles)
    tiles = [None] * n_tiles
    h = h_in
    for i in order:
        sl = slice(i * SUBLANES, (i + 1) * SUBLANES)
        h = u[sl] + a[sl] * h
        tiles[i] = h
    return jnp.concatenate(tiles, axis=0)


def _lru_kernel(x_ref, g_ref, cw_ref, cb_ref, w_ref, b_ref, lam_ref, h0_ref, y_ref, hT_ref, hf_ref, hb_ref,
                *, n_rows, chunk):
    n_chunks = n_rows // chunk
    cw = cw_ref[...]
    cb = cb_ref[...]

    def gates(d):
        lam = lam_ref[d:d + 1, :]
        log_sig = jnp.minimum(lam, 0.0) - jnp.log1p(jnp.exp(-jnp.abs(lam)))
        return (w_ref[d, 0].astype(BF16), w_ref[d, 1].astype(BF16), b_ref[d, 0:1, :], b_ref[d, 1:2, :], log_sig)

    def sweep(base, h, params, reverse):
        w_r, w_i, b_r, b_i, log_sig = params
        x = _conv_chunk(x_ref, base, n_rows, chunk, cw, cb)
        xb = x.astype(BF16)
        r = jax.nn.sigmoid(jnp.dot(xb, w_r, preferred_element_type=F32) + b_r)
        i = jax.nn.sigmoid(jnp.dot(xb, w_i, preferred_element_type=F32) + b_i)
        log_a = RGLRU_C * r * log_sig
        a = jnp.exp(log_a)
        u = jnp.sqrt(1.0 - a * a) * (i * x)
        return _lru_scan_chunk(a, u, h, reverse)

    fwd, bwd = gates(0), gates(1)

    def body(c, carry):
        hf, hb = carry
        base_f = pl.multiple_of(c * chunk, chunk)
        base_b = pl.multiple_of((n_chunks - 1 - c) * chunk, chunk)
        hs_f = sweep(base_f, hf, fwd, False)
        hs_b = sweep(base_b, hb, bwd, True)
        hf_ref[pl.ds(base_f, chunk), :] = hs_f
        hb_ref[pl.ds(base_b, chunk), :] = hs_b
        return hs_f[chunk - 1:chunk], hs_b[0:1]

    hT_ref[0:1, :], hT_ref[1:2, :] = lax.fori_loop(0, n_chunks, body, (h0_ref[0:1, :], h0_ref[1:2, :]))

    def emit(c, _):
        rows = pl.ds(pl.multiple_of(c * chunk, chunk), chunk)
        y_ref[rows, :] = ((hf_ref[rows, :] + hb_ref[rows, :]) * jax.nn.gelu(g_ref[rows, :])).astype(BF16)
        return 0

    lax.fori_loop(0, n_chunks, emit, 0)


def rglru(p, x_col0, g_col0, conv_w, conv_b, w_gates, b_gates, lam, h0):
    n_rows = p.shape[0]
    chunk = min(n_rows, LRU_CHUNK)
    xb = x_col0 // LANES
    gb = g_col0 // LANES
    seq = pltpu.VMEM((n_rows, LANES), F32)
    return pl.pallas_call(
        functools.partial(_lru_kernel, n_rows=n_rows, chunk=chunk),
        grid=(LRU_BLOCKS,),
        in_specs=[pl.BlockSpec((n_rows, LANES), lambda j: (0, xb + j)),
                  pl.BlockSpec((n_rows, LANES), lambda j: (0, gb + j)),
                  pl.BlockSpec((4, LANES), lambda j: (0, j)),
                  pl.BlockSpec((1, LANES), lambda j: (0, j)),
                  pl.BlockSpec((2, 2, None, LANES, LANES), lambda j: (0, 0, j, 0, 0)),
                  pl.BlockSpec((2, 2, LANES), lambda j: (0, 0, j)),
                  pl.BlockSpec((2, LANES), lambda j: (0, j)),
                  pl.BlockSpec((2, LANES), lambda j: (0, j))],
        out_specs=[pl.BlockSpec((n_rows, LANES), lambda j: (0, j)),
                   pl.BlockSpec((2, LANES), lambda j: (0, j))],
        out_shape=[jax.ShapeDtypeStruct((n_rows, LRU_WIDTH), BF16),
                   jax.ShapeDtypeStruct((2, LRU_WIDTH), F32)],
        scratch_shapes=[seq, seq],
        compiler_params=_cparams("parallel"),
        name="rglru",
    )(p, p, conv_w, conv_b.reshape(1, LRU_WIDTH), w_gates, b_gates, lam, h0)


SSD_CHUNKS_PER_STEP = 4


def _ssd_direction(xs, bm, cm, dt_raw, dtT_raw, bias_row, bias_col, a_row, a_col, st_ref, reverse):
    q = SSD_CHUNK
    hi = lax.Precision.HIGHEST
    li = lax.broadcasted_iota(jnp.int32, (q, q), 0)
    si = lax.broadcasted_iota(jnp.int32, (q, q), 1)
    causal = (si >= li) if reverse else (si <= li)
    tri = causal.astype(F32)
    dt = jax.nn.softplus(dt_raw + bias_row)
    dtT = jax.nn.softplus(dtT_raw + bias_col)
    adt = dt * a_row
    adtT = dtT * a_col
    cum = jnp.dot(tri, adt, preferred_element_type=F32, precision=hi)
    cumT = lax.dot_general(adtT, tri, (((1,), (1,)), ((), ())), preferred_element_type=F32,
                           precision=hi)
    edge = cum[0:1] if reverse else cum[q - 1:q]

    def lanes(v):
        return [jnp.broadcast_to(v[:, h:h + 1], (v.shape[0], LANES)) for h in range(SSM_HEADS)]

    def per_channel(cols):
        first = lax.broadcasted_iota(jnp.int32, cols[0].shape, 1) < SSM_HEAD_DIM
        return jnp.concatenate([jnp.where(first, cols[2 * j], cols[2 * j + 1]) for j in range(SSM_HEADS // 2)],
                               axis=1)

    edgeT = cumT[:, 0:1] if reverse else cumT[:, q - 1:q]
    wT = dtT * jnp.exp(edgeT - cumT)
    cum_l = lanes(cum)
    grow = jnp.exp(per_channel(cum_l))
    keep = jnp.exp(per_channel(lanes(edge)))
    xs_b = xs.astype(BF16)
    per_group = SSM_HEADS // SSM_GROUPS
    first_head = lax.broadcasted_iota(jnp.int32, (1, LANES), 1) < SSM_HEAD_DIM
    ys = []
    for g in range(SSM_GROUPS):
        gsl = slice(g * per_group * SSM_HEAD_DIM, (g + 1) * per_group * SSM_HEAD_DIM)
        b_gt = bm[:, g * SSM_D_STATE:(g + 1) * SSM_D_STATE].T
        c_g = cm[:, g * SSM_D_STATE:(g + 1) * SSM_D_STATE].astype(BF16)
        gram = jnp.dot(c_g, b_gt.astype(BF16), preferred_element_type=F32)
        st = st_ref[:, gsl]
        y_off = grow[:, gsl] * jnp.dot(c_g, st.astype(BF16), preferred_element_type=F32)
        upd = []
        for pp in range(per_group // 2):
            psl = slice(pp * LANES, (pp + 1) * LANES)
            x_p = xs_b[:, gsl][:, psl]
            y_pair, u_pair = [], []
            for h in (g * per_group + 2 * pp, g * per_group + 2 * pp + 1):
                seg = cum_l[h] - cumT[h:h + 1, :]
                decay = jnp.exp(jnp.where(causal, seg, -jnp.inf))
                m = (gram * (decay * dtT[h:h + 1, :])).astype(BF16)
                y_pair.append(jnp.dot(m, x_p, preferred_element_type=F32))
                u_pair.append(jnp.dot((b_gt * wT[h:h + 1, :]).astype(BF16), x_p, preferred_element_type=F32))
            ys.append(jnp.where(first_head, y_pair[0], y_pair[1]) + y_off[:, psl])
            upd.append(jnp.where(first_head, u_pair[0], u_pair[1]))
        st_ref[:, gsl] = keep[:, gsl] * st + jnp.concatenate(upd, axis=1)
    return jnp.concatenate(ys, axis=1)


def _ssd_kernel(xf_ref, bf_ref, cf_ref, dtf_ref, dtTf_ref, xb_ref, bb_ref, cb_ref, dtb_ref, dtTb_ref,
                bias_ref, biasT_ref, alog_ref, alogT_ref, st0_ref, yf_ref, yb_ref, stT_ref, st_ref):
    c = pl.program_id(0)

    @pl.when(c == 0)
    def _():
        st_ref[...] = st0_ref[...]

    a_row = -jnp.exp(alog_ref[...])
    a_col = -jnp.exp(alogT_ref[...])
    q = SSD_CHUNK
    n_sub = xf_ref.shape[0] // q
    for k in range(n_sub):
        rows = slice(k * q, (k + 1) * q)
        yf_ref[rows, :] = _ssd_direction(xf_ref[rows, :], bf_ref[rows, :], cf_ref[rows, :],
                                         dtf_ref[rows, 0:SSM_HEADS], dtTf_ref[0:SSM_HEADS, rows],
                                         bias_ref[0:1, :], biasT_ref[:, 0:1], a_row[0:1, :], a_col[:, 0:1],
                                         st_ref.at[0], False)
    for k in reversed(range(n_sub)):
        rows = slice(k * q, (k + 1) * q)
        yb_ref[rows, :] = _ssd_direction(xb_ref[rows, :], bb_ref[rows, :], cb_ref[rows, :],
                                         dtb_ref[rows, SSM_HEADS:2 * SSM_HEADS],
                                         dtTb_ref[SSM_HEADS:2 * SSM_HEADS, rows],
                                         bias_ref[1:2, :], biasT_ref[:, 1:2], a_row[1:2, :], a_col[:, 1:2],
                                         st_ref.at[1], True)

    @pl.when(c == pl.num_programs(0) - 1)
    def _():
        stT_ref[...] = st_ref[...]


def ssd_scan(xbc, dt, dtT, dt_bias, a_log, st0):
    n_rows = xbc.shape[0]
    q = SSD_CHUNK * min(SSD_CHUNKS_PER_STEP, n_rows // SSD_CHUNK)
    nc = n_rows // q
    fwd = lambda c: c
    bwd = lambda c: nc - 1 - c

    def specs(ix):
        return [pl.BlockSpec((q, SSM_D_INNER), lambda c: (ix(c), 0)),
                pl.BlockSpec((q, 2 * SSM_D_STATE), lambda c: (ix(c), 2)),
                pl.BlockSpec((q, 2 * SSM_D_STATE), lambda c: (ix(c), 3)),
                pl.BlockSpec((q, 2 * SSM_HEADS), lambda c: (ix(c), 0)),
                pl.BlockSpec((2 * SSM_HEADS, q), lambda c: (0, ix(c)))]

    small = lambda shape: pl.BlockSpec(shape, lambda c: (0,) * len(shape))
    st_shape = (2, SSM_D_STATE, SSM_D_INNER)
    return pl.pallas_call(
        _ssd_kernel,
        grid=(nc,),
        in_specs=specs(fwd) + specs(bwd) + [small((2, SSM_HEADS)), small((SSM_HEADS, 2)),
                                            small((2, SSM_HEADS)), small((SSM_HEADS, 2)), small(st_shape)],
        out_specs=[pl.BlockSpec((q, SSM_D_INNER), lambda c: (c, 0)),
                   pl.BlockSpec((q, SSM_D_INNER), lambda c: (nc - 1 - c, 0)),
                   small(st_shape)],
        out_shape=[jax.ShapeDtypeStruct((n_rows, SSM_D_INNER), F32),
                   jax.ShapeDtypeStruct((n_rows, SSM_D_INNER), F32),
                   jax.ShapeDtypeStruct(st_shape, F32)],
        scratch_shapes=[pltpu.VMEM(st_shape, F32)],
        compiler_params=_cparams("arbitrary"),
        name="ssd_scan",
    )(xbc, xbc, xbc, dt, dtT, xbc, xbc, xbc, dt, dtT, dt_bias, dt_bias.T, a_log, a_log.T, st0)


def _ssd_finish_kernel(yf_ref, yb_ref, xs_ref, z_ref, d_ref, nw_ref, o_ref):
    hrow = lax.broadcasted_iota(jnp.int32, (SSM_HEADS, SSM_D_INNER), 0)
    hcol = lax.broadcasted_iota(jnp.int32, (SSM_HEADS, SSM_D_INNER), 1) // SSM_HEAD_DIM
    d_e = jnp.sum(jnp.where(hrow == hcol, d_ref[...], 0.0), axis=0, keepdims=True)
    y = yf_ref[...] + yb_ref[...] + d_e * xs_ref[...]
    y = y * _silu(z_ref[...])
    o_ref[...] = (y * lax.rsqrt(jnp.mean(y * y, axis=-1, keepdims=True) + NORM_EPS) * nw_ref[...]).astype(BF16)


def ssd_finish(yf, yb, xbc, p, z_col0, d_skip, norm_w):
    n_rows = yf.shape[0]
    tm = min(n_rows, ROWWISE_TM)
    zb = z_col0 // SSM_D_INNER
    blk = lambda cb: pl.BlockSpec((tm, SSM_D_INNER), lambda i: (i, cb))
    return pl.pallas_call(
        _ssd_finish_kernel,
        grid=(n_rows // tm,),
        in_specs=[blk(0), blk(0), blk(0), blk(zb),
                  pl.BlockSpec((SSM_HEADS, 1), lambda i: (0, 0)),
                  pl.BlockSpec((1, SSM_D_INNER), lambda i: (0, 0))],
        out_specs=blk(0),
        out_shape=jax.ShapeDtypeStruct((n_rows, SSM_D_INNER), BF16),
        compiler_params=_cparams("parallel"),
        name="ssd_finish",
    )(yf, yb, xbc, p, d_skip.reshape(SSM_HEADS, 1), norm_w.reshape(1, SSM_D_INNER))


LRU_X_COL = 0
LRU_G_COL = 512
SSM_Z_COL = 1024
SSM_XBC_COL = 1536


def _mixers(p, dt, qkv_q, qkv_c, qkv_l, lw, lam_init, lru_h0, ssd_st0):
    ya = diff_attention(qkv_q, qkv_c, qkv_l, lw["da_lambda"], lw["da_subln"], lam_init)
    yb = gqa_attention(qkv_q, qkv_c, qkv_l)
    yc, lru_hT = rglru(p, LRU_X_COL, LRU_G_COL, lw["lru_conv_w"], lw["lru_conv_b"], lw["lru_w_gates"],
                       lw["lru_b_gates"], lw["lru_lambda"], lru_h0)
    xbc = dwconv(p, SSM_XBC_COL, 2 * SSM_D_INNER, lw["ssm_conv_w"], lw["ssm_conv_b"], act=True)
    yf, ybk, ssd_stT = ssd_scan(xbc, dt, dt.T, lw["ssm_dt_bias"], lw["ssm_a_log"], ssd_st0)
    yd = ssd_finish(yf, ybk, xbc, p, SSM_Z_COL, lw["ssm_d"], lw["ssm_norm"])
    return [ya, yb, yc, yd], lru_hT, ssd_stT


def kernel(x, c, ctx, c_ctx, w_mod, b_mod, mix_norm, ffn_norm, w_in, w_out, da_lambda, da_subln, gqa_q_norm,
           gqa_k_norm, lru_conv_w, lru_conv_b, lru_w_gates, lru_b_gates, lru_lambda, ssm_conv_w, ssm_conv_b,
           ssm_dt_bias, ssm_a_log, ssm_d, ssm_norm, ffn_w_up, ffn_conv_w, ffn_conv_b, ffn_w_down, final_norm):
    depth = w_mod.shape[0]
    d = x.shape[-1]
    xl = x[0]
    xc = ctx[0]
    n_lat = xl.shape[0]

    cond8 = jnp.zeros((8, d), F32).at[0].set(c[0]).at[1].set(c_ctx)
    mods = modulation_all(cond8, w_mod, b_mod)
    tabs_l = _rope_tables(n_lat, DA_HEAD_DIM) + _rope_tables(n_lat, GQA_HEAD_DIM)
    lru_zero = jnp.zeros((2, LRU_WIDTH), F32)
    ssd_zero = jnp.zeros((2, SSM_D_STATE, SSM_D_INNER), F32)
    w_in_t = jnp.swapaxes(w_in, 1, 2)

    for layer in range(depth):
        need_ctx = layer < depth - 1
        lam_init = 0.8 - 0.6 * math.exp(-0.3 * layer)
        ml = [mods[layer, 0:1, k * d:(k + 1) * d] for k in range(6)]
        mc = [mods[layer, 1:2, k * d:(k + 1) * d] for k in range(6)]
        lw = dict(da_lambda=da_lambda[layer], da_subln=da_subln[layer],
                  lru_conv_w=lru_conv_w[layer], lru_conv_b=lru_conv_b[layer], lru_w_gates=lru_w_gates[layer],
                  lru_b_gates=lru_b_gates[layer], lru_lambda=lru_lambda[layer], ssm_conv_w=ssm_conv_w[layer],
                  ssm_conv_b=ssm_conv_b[layer], ssm_dt_bias=ssm_dt_bias[layer], ssm_a_log=ssm_a_log[layer],
                  ssm_d=ssm_d[layer], ssm_norm=ssm_norm[layer])
        nw_m = mix_norm[layer].reshape(1, d)
        nw_f = ffn_norm[layer].reshape(1, d)

        xn_c, dt_c = adaln(xc, nw_m, mc[0], mc[1], w_in_t, layer)
        xn_l, dt_l = adaln(xl, nw_m, ml[0], ml[1], w_in_t, layer)
        p_c, p_l = in_proj(xn_c, xn_l, w_in_t, layer)
        qkv_c, qkv_l = in_proj_attn(xn_c, xn_l, w_in_t, layer, tabs_l, gqa_q_norm[layer], gqa_k_norm[layer])

        y_c, lru_h, ssd_st = _mixers(p_c, dt_c, qkv_c, qkv_c, None, lw, lam_init, lru_zero, ssd_zero)
        y_l, _, _ = _mixers(p_l, dt_l, qkv_l, qkv_c, qkv_l, lw, lam_init, lru_h, ssd_st)

        conv_w, conv_b = ffn_conv_w[layer], ffn_conv_b[layer]
        if need_ctx:
            xc, xl = resid_proj(w_out, layer, (y_l, xl, ml[2]), (y_c, xc, mc[2]))
            act_c, act_l = ffn_up(ffn_w_up, layer, conv_w, conv_b, adaln(xl, nw_f, ml[3], ml[4]),
                                  adaln(xc, nw_f, mc[3], mc[4]))
            xc, xl = resid_proj(ffn_w_down, layer, ([act_l], xl, ml[5]), ([act_c], xc, mc[5]))
        else:
            xl = resid_proj(w_out, layer, (y_l, xl, ml[2]))
            act_l = ffn_up(ffn_w_up, layer, conv_w, conv_b, adaln(xl, nw_f, ml[3], ml[4]))
            xl = resid_proj(ffn_w_down, layer, ([act_l], xl, ml[5]))

    return final_rmsnorm(xl, final_norm)[None]
```

```python
import functools
import math

import jax
import jax.numpy as jnp
from jax import lax
from jax.experimental import pallas as pl
from jax.experimental.pallas import tpu as pltpu

F32 = jnp.float32
BF16 = jnp.bfloat16

GRID_W = 64
GROUP_WIDTH = 512
DA_HEAD_DIM = 64
DA_HEADS = 4
GQA_HEAD_DIM = 128
GQA_HEADS = 4
GQA_KV_HEADS = 2
LRU_WIDTH = 512
LRU_BLOCKS = 4
RGLRU_C = 8.0
SSM_D_INNER = 512
SSM_HEAD_DIM = 64
SSM_HEADS = 8
SSM_GROUPS = 2
SSM_D_STATE = 128
SSD_CHUNK = 128
D_FF = 5632
ROPE_THETA = 10000.0
NORM_EPS = 1e-6
LOG2E = math.log2(math.e)

LANES = 128
SUBLANES = 8
HALO = 16
IN_COLS = 5136
IN_MAIN_COLS = 5120
N_DT = IN_COLS - IN_MAIN_COLS
ATTN_COLS = 2560
QA_BLK, QB_BLK, KD_BLK, VD_BLK = 0, 4, 8, 12
QG_BLK, KG_BLK, VG_BLK = 20, 24, 26
QKV_COLS = 30 * LANES
P_QD, P_KD, P_VD, P_QG, P_KG, P_VG = 0, 4, 8, 12, 16, 18

VMEM_LIMIT = 56 * 1024 * 1024
PROJ_TM = 1024
IN_TN = 1024
RESID_TM = 512
RESID_TN_MAX = 1024
RESID_W_ELEMS = 2048 * 1024
ROWWISE_TM = 1024
FFN_TN = 512


def _cparams(*sem):
    return pltpu.CompilerParams(dimension_semantics=sem, vmem_limit_bytes=VMEM_LIMIT)


def _silu(x):
    return x * jax.nn.sigmoid(x)


def _norm_mod(x, nw, shift, scale):
    y = x * lax.rsqrt(jnp.mean(x * x, axis=-1, keepdims=True) + NORM_EPS) * nw
    return y * (1.0 + scale) + shift


def _mod_kernel(c_ref, w_ref, b_ref, o_ref):
    s = _silu(c_ref[...]).astype(BF16)
    o_ref[...] = jnp.dot(s, w_ref[...].astype(BF16), preferred_element_type=F32) + b_ref[...]


def modulation_all(cond8, w_mod, b_mod):
    depth, d, n = w_mod.shape
    tn = IN_TN
    return pl.pallas_call(
        _mod_kernel,
        grid=(depth, n // tn),
        in_specs=[
            pl.BlockSpec((8, d), lambda l, j: (0, 0)),
            pl.BlockSpec((None, d, tn), lambda l, j: (l, 0, j)),
            pl.BlockSpec((None, 1, tn), lambda l, j: (l, 0, j)),
        ],
        out_specs=pl.BlockSpec((None, 8, tn), lambda l, j: (l, 0, j)),
        out_shape=jax.ShapeDtypeStruct((depth, 8, n), F32),
        compiler_params=_cparams("parallel", "parallel"),
        name="modulation",
    )(cond8, w_mod, b_mod.reshape(depth, 1, n))


ADALN_ROWS = 16
_NT = (((1,), (1,)), ((), ()))


def _adaln_kernel(*refs, with_dt):
    if with_dt:
        x_ref, nw_ref, sh_ref, sc_ref, wdt_ref, o_ref, dt_ref = refs
    else:
        x_ref, nw_ref, sh_ref, sc_ref, o_ref = refs
    gain = nw_ref[...] * (1.0 + sc_ref[...])
    shift = sh_ref[...]
    d = x_ref.shape[1]

    def body(r, _):
        rows = pl.ds(pl.multiple_of(r * ADALN_ROWS, ADALN_ROWS), ADALN_ROWS)
        x = x_ref[rows, :]
        inv = lax.rsqrt(jnp.sum(x * x, axis=-1, keepdims=True) * (1.0 / d) + NORM_EPS)
        o_ref[rows, :] = (x * inv * gain + shift).astype(BF16)
        return 0

    lax.fori_loop(0, x_ref.shape[0] // ADALN_ROWS, body, 0, unroll=4)
    if with_dt:
        dt_ref[...] = lax.dot_general(o_ref[...], wdt_ref[...].astype(BF16), _NT, preferred_element_type=F32)


def adaln(x, nw, shift, scale, w_in_t=None, layer=None):
    m, d = x.shape
    tm = min(m, PROJ_TM)
    with_dt = w_in_t is not None
    vec = pl.BlockSpec((1, d), lambda i: (0, 0))
    rows = pl.BlockSpec((tm, d), lambda i: (i, 0))
    in_specs, args = [rows, vec, vec, vec], [x, nw, shift, scale]
    out_specs, out_shape = rows, jax.ShapeDtypeStruct((m, d), BF16)
    if with_dt:
        in_specs.append(pl.BlockSpec((None, N_DT, d), lambda i: (layer, IN_MAIN_COLS // N_DT, 0)))
        args.append(w_in_t)
        out_specs = [rows, pl.BlockSpec((tm, N_DT), lambda i: (i, 0))]
        out_shape = [out_shape, jax.ShapeDtypeStruct((m, N_DT), F32)]
    return pl.pallas_call(
        functools.partial(_adaln_kernel, with_dt=with_dt),
        grid=(m // tm,),
        in_specs=in_specs,
        out_specs=out_specs,
        out_shape=out_shape,
        compiler_params=_cparams("parallel"),
        name="adaln",
    )(*args)


def _lat_block(i):
    return jnp.maximum(i - 1, 0)


def _inproj_kernel(xc_ref, xl_ref, w_ref, oc_ref, ol_ref, wb_ref):
    i = pl.program_id(1)

    @pl.when(i == 0)
    def _():
        wb_ref[...] = w_ref[...].astype(BF16)
        oc_ref[...] = lax.dot_general(xc_ref[...], wb_ref[...], _NT, preferred_element_type=F32)
        ol_ref[...] = jnp.zeros_like(ol_ref)

    @pl.when(i > 0)
    def _():
        ol_ref[...] = lax.dot_general(xl_ref[...], wb_ref[...], _NT, preferred_element_type=F32)


def in_proj(xn_c, xn_l, w_in_t, layer):
    (mc, d), ml = xn_c.shape, xn_l.shape[0]
    tm = min(ml, PROJ_TM)
    return pl.pallas_call(
        _inproj_kernel,
        grid=(IN_MAIN_COLS // IN_TN, 1 + ml // tm),
        in_specs=[pl.BlockSpec((mc, d), lambda j, i: (0, 0)),
                  pl.BlockSpec((tm, d), lambda j, i: (_lat_block(i), 0)),
                  pl.BlockSpec((None, IN_TN, d), lambda j, i: (layer, j, 0))],
        out_specs=[pl.BlockSpec((mc, IN_TN), lambda j, i: (0, j)),
                   pl.BlockSpec((tm, IN_TN), lambda j, i: (_lat_block(i), j))],
        out_shape=[jax.ShapeDtypeStruct((mc, IN_MAIN_COLS), F32), jax.ShapeDtypeStruct((ml, IN_MAIN_COLS), F32)],
        scratch_shapes=[pltpu.VMEM((IN_TN, d), BF16)],
        compiler_params=_cparams("parallel", "arbitrary"),
        name="in_proj",
    )(xn_c, xn_l, w_in_t)


def _resid_kernel(*refs, n_y, with_ctx):
    refs = list(refs)
    ctx_refs = None
    if with_ctx:
        ctx_refs, refs = refs[:n_y + 2], refs[n_y + 2:]
    lat_refs, refs = refs[:n_y + 2], refs[n_y + 2:]
    w_ref = refs.pop(0)
    oc_ref = refs.pop(0) if with_ctx else None
    ol_ref, wb_ref = refs
    i = pl.program_id(1)
    kk = wb_ref.shape[0] // n_y

    def project(stream, o_ref):
        ys, x_ref, g_ref = stream[:n_y], stream[n_y], stream[n_y + 1]
        acc = functools.reduce(jnp.add, [
            jnp.dot(ys[a][...], wb_ref[a * kk:(a + 1) * kk, :], preferred_element_type=F32) for a in range(n_y)])
        o_ref[...] = x_ref[...] + g_ref[...] * acc

    @pl.when(i == 0)
    def _():
        wb_ref[...] = w_ref[...].astype(BF16)
        if with_ctx:
            project(ctx_refs, oc_ref)
            ol_ref[...] = jnp.zeros_like(ol_ref)
        else:
            project(lat_refs, ol_ref)

    @pl.when(i > 0)
    def _():
        project(lat_refs, ol_ref)


def resid_proj(w_all, layer, lat, ctx=None):
    ys_l, x_l, gate_l = lat
    ml = x_l.shape[0]
    _, k, n = w_all.shape
    n_y = len(ys_l)
    kk = k // n_y
    small_w = k * RESID_TN_MAX <= RESID_W_ELEMS
    tm = min(ml, 2 * RESID_TM if small_w else RESID_TM)
    tn = RESID_TN_MAX if small_w else RESID_TN_MAX // 2
    with_ctx = ctx is not None
    lat_blk = _lat_block if with_ctx else (lambda i: i)

    def stream_specs(rows, blk):
        return ([pl.BlockSpec((rows, kk), lambda j, i: (blk(i), 0)) for _ in range(n_y)]
                + [pl.BlockSpec((rows, tn), lambda j, i: (blk(i), j)), pl.BlockSpec((1, tn), lambda j, i: (0, j))])

    in_specs = stream_specs(tm, lat_blk) + [pl.BlockSpec((None, k, tn), lambda j, i: (layer, 0, j))]
    args = [*ys_l, x_l, gate_l, w_all]
    out_specs = [pl.BlockSpec((tm, tn), lambda j, i: (lat_blk(i), j))]
    out_shape = [jax.ShapeDtypeStruct((ml, n), F32)]
    if with_ctx:
        ys_c, x_c, gate_c = ctx
        mc = x_c.shape[0]
        in_specs = stream_specs(mc, lambda i: 0) + in_specs
        args = [*ys_c, x_c, gate_c] + args
        out_specs = [pl.BlockSpec((mc, tn), lambda j, i: (0, j))] + out_specs
        out_shape = [jax.ShapeDtypeStruct((mc, n), F32)] + out_shape
    out = pl.pallas_call(
        functools.partial(_resid_kernel, n_y=n_y, with_ctx=with_ctx),
        grid=(n // tn, ml // tm + int(with_ctx)),
        in_specs=in_specs,
        out_specs=out_specs,
        out_shape=out_shape,
        scratch_shapes=[pltpu.VMEM((k, tn), BF16)],
        compiler_params=_cparams("parallel", "arbitrary"),
        name="resid_proj",
    )(*args)
    return tuple(out) if with_ctx else out[0]


def _ffn_up_kernel(*refs, n_lat_blocks, with_ctx):
    refs = list(refs)
    xc_ref = refs.pop(0) if with_ctx else None
    xp_ref, x_ref, xx_ref, wg_ref, wu_ref, cw_ref, cb_ref = refs[:7]
    del refs[:7]
    oc_ref = refs.pop(0) if with_ctx else None
    o_ref, xn_ref, g_ref, wb_ref = refs
    i = pl.program_id(1)
    blk = (i - 1) if with_ctx else i

    def up_block(prev, cur_ref, nxt, out_ref):
        n = cur_ref.shape[0]
        xn_ref[pl.ds(0, HALO), :] = prev
        xn_ref[pl.ds(HALO, n), :] = cur_ref[...]
        xn_ref[pl.ds(HALO + n, HALO), :] = nxt
        g_ref[pl.ds(0, n + 2 * HALO), :] = jnp.dot(xn_ref[pl.ds(0, n + 2 * HALO), :], wb_ref[0],
                                                   preferred_element_type=F32)
        up = jnp.dot(cur_ref[...], wb_ref[1], preferred_element_type=F32)
        cw = cw_ref[...]
        gc = (cb_ref[...] + cw[0:1] * g_ref[pl.ds(HALO - 1, n), :] + cw[1:2] * g_ref[pl.ds(HALO, n), :]
              + cw[2:3] * g_ref[pl.ds(HALO + 1, n), :])
        out_ref[...] = (_silu(gc) * up).astype(BF16)

    def latent_block():
        zeros = jnp.zeros_like(xp_ref)
        up_block(jnp.where(blk == 0, zeros, xp_ref[...]), x_ref,
                 jnp.where(blk == n_lat_blocks - 1, zeros, xx_ref[...]), o_ref)

    @pl.when(i == 0)
    def _():
        wb_ref[0] = wg_ref[...].astype(BF16)
        wb_ref[1] = wu_ref[...].astype(BF16)
        if with_ctx:
            zeros = jnp.zeros_like(xp_ref)
            up_block(zeros, xc_ref, zeros, oc_ref)
            o_ref[...] = jnp.zeros_like(o_ref)
        else:
            latent_block()

    @pl.when(i > 0)
    def _():
        latent_block()


def ffn_up(w_up_all, layer, conv_w, conv_b, xn_l, xn_c=None):
    ml, d = xn_l.shape
    dff = w_up_all.shape[2] // 2
    tm = min(ml, PROJ_TM)
    tn = FFN_TN
    nrb = ml // tm
    hb = tm // HALO
    nhb = ml // HALO
    with_ctx = xn_c is not None
    blk = _lat_block if with_ctx else (lambda i: i)
    in_specs = [pl.BlockSpec((HALO, d), lambda j, i: (jnp.maximum(blk(i) * hb - 1, 0), 0)),
                pl.BlockSpec((tm, d), lambda j, i: (blk(i), 0)),
                pl.BlockSpec((HALO, d), lambda j, i: (jnp.minimum((blk(i) + 1) * hb, nhb - 1), 0)),
                pl.BlockSpec((None, d, tn), lambda j, i: (layer, 0, j)),
                pl.BlockSpec((None, d, tn), lambda j, i: (layer, 0, j + dff // tn)),
                pl.BlockSpec((3, tn), lambda j, i: (0, j)),
                pl.BlockSpec((1, tn), lambda j, i: (0, j))]
    args = [xn_l, xn_l, xn_l, w_up_all, w_up_all, conv_w, conv_b.reshape(1, dff)]
    out_specs = [pl.BlockSpec((tm, tn), lambda j, i: (blk(i), j))]
    out_shape = [jax.ShapeDtypeStruct((ml, dff), BF16)]
    if with_ctx:
        mc = xn_c.shape[0]
        in_specs = [pl.BlockSpec((mc, d), lambda j, i: (0, 0))] + in_specs
        args = [xn_c] + args
        out_specs = [pl.BlockSpec((mc, tn), lambda j, i: (0, j))] + out_specs
        out_shape = [jax.ShapeDtypeStruct((mc, dff), BF16)] + out_shape
    out = pl.pallas_call(
        functools.partial(_ffn_up_kernel, n_lat_blocks=nrb, with_ctx=with_ctx),
        grid=(dff // tn, nrb + int(with_ctx)),
        in_specs=in_specs,
        out_specs=out_specs,
        out_shape=out_shape,
        scratch_shapes=[pltpu.VMEM((tm + 2 * HALO, d), BF16), pltpu.VMEM((tm + 2 * HALO, tn), F32),
                        pltpu.VMEM((2, d, tn), BF16)],
        compiler_params=_cparams("parallel", "arbitrary"),
        name="ffn_up",
    )(*args)
    return tuple(out) if with_ctx else out[0]


def _rmsnorm_kernel(x_ref, w_ref, o_ref):
    x = x_ref[...]
    o_ref[...] = x * lax.rsqrt(jnp.mean(x * x, axis=-1, keepdims=True) + NORM_EPS) * w_ref[...]


def final_rmsnorm(x, w):
    m, d = x.shape
    tm = min(m, ROWWISE_TM)
    return pl.pallas_call(
        _rmsnorm_kernel,
        grid=(m // tm,),
        in_specs=[pl.BlockSpec((tm, d), lambda i: (i, 0)), pl.BlockSpec((1, d), lambda i: (0, 0))],
        out_specs=pl.BlockSpec((tm, d), lambda i: (i, 0)),
        out_shape=jax.ShapeDtypeStruct((m, d), F32),
        compiler_params=_cparams("parallel"),
        name="final_norm",
    )(x, w.reshape(1, d))


def _rope_tables(n_tok, head_dim):
    q = head_dim // 4
    lane = jnp.arange(LANES, dtype=jnp.int32)
    u = lane % head_dim
    region = u // q
    freqs = jnp.power(ROPE_THETA, -(u % q).astype(F32) / q)[None, :]
    ang_r = jnp.arange(n_tok // GRID_W, dtype=jnp.int32).astype(F32)[:, None] * freqs
    ang_c = jnp.arange(GRID_W, dtype=jnp.int32).astype(F32)[:, None] * freqs
    by_row = (region < 2)[None, None, :]

    def per_token(f):
        return jnp.where(by_row, f(ang_r)[:, None, :], f(ang_c)[None, :, :]).reshape(n_tok, LANES)

    cos, sin = per_token(jnp.cos), per_token(jnp.sin)
    first = (region % 2 == 0)[None, :]
    return cos, jnp.where(first, -sin, 0.0), jnp.where(first, 0.0, sin)


def _prep_kernel(p_ref, cd_ref, ad_ref, bd_ref, cg_ref, ag_ref, bg_ref, qn_ref, kn_ref, o_ref, *, rope):
    lane = lax.broadcasted_iota(jnp.int32, (1, LANES), 1)
    lo = lane < DA_HEAD_DIM

    def blk(b):
        return p_ref[:, b * LANES:(b + 1) * LANES]

    def put(b, v):
        o_ref[:, b * LANES:(b + 1) * LANES] = v.astype(BF16)

    def rot(x, cos, sa, sb, quarter):
        if not rope:
            return x
        return (x * cos[...] + pltpu.roll(x, LANES - quarter, 1) * sa[...]
                + pltpu.roll(x, quarter, 1) * sb[...])

    def rms(x, w):
        return x * lax.rsqrt(jnp.mean(x * x, axis=-1, keepdims=True) + NORM_EPS) * w[...]

    qd = DA_HEAD_DIM // 4
    qg = GQA_HEAD_DIM // 4
    ones = jnp.ones((p_ref.shape[0], LANES), F32)
    for h in range(DA_HEADS):
        q = rot(blk(P_QD + h), cd_ref, ad_ref, bd_ref, qd) * (DA_HEAD_DIM ** -0.5 * LOG2E)
        put(QA_BLK + h, jnp.where(lo, q, 0.0))
        put(QB_BLK + h, jnp.where(lo, 0.0, q))
        put(KD_BLK + h, rot(blk(P_KD + h), cd_ref, ad_ref, bd_ref, qd))
        put(VD_BLK + 2 * h, blk(P_VD + h))
        put(VD_BLK + 2 * h + 1, ones)
    for h in range(GQA_HEADS):
        q = rot(rms(blk(P_QG + h), qn_ref), cg_ref, ag_ref, bg_ref, qg) * (GQA_HEAD_DIM ** -0.5 * LOG2E)
        put(QG_BLK + h, q)
    for h in range(GQA_KV_HEADS):
        put(KG_BLK + h, rot(rms(blk(P_KG + h), kn_ref), cg_ref, ag_ref, bg_ref, qg))
        put(VG_BLK + 2 * h, blk(P_VG + h))
        put(VG_BLK + 2 * h + 1, ones)


def attn_prep(p, tables, q_norm, k_norm, rope):
    m = p.shape[0]
    tm = min(m, ROWWISE_TM)
    tab = pl.BlockSpec((tm, LANES), lambda i: (i, 0))
    vec = pl.BlockSpec((1, LANES), lambda i: (0, 0))
    return pl.pallas_call(
        functools.partial(_prep_kernel, rope=rope),
        grid=(m // tm,),
        in_specs=[pl.BlockSpec((tm, ATTN_COLS), lambda i: (i, 0))] + [tab] * 6 + [vec, vec],
        out_specs=pl.BlockSpec((tm, QKV_COLS), lambda i: (i, 0)),
        out_shape=jax.ShapeDtypeStruct((m, QKV_COLS), BF16),
        compiler_params=_cparams("parallel"),
        name="attn_prep",
    )(p, *tables, q_norm.reshape(1, LANES), k_norm.reshape(1, LANES))


def _attn_kernel(*refs, tk, n_kv, diff, lam_init):
    refs = list(refs)
    qa_ref, qb_ref, kc_ref, vc_ref = refs[:4]
    del refs[:4]
    if n_kv:
        kl_ref, vl_ref = refs[:2]
        del refs[:2]
    if diff:
        lam_ref, sw_ref = refs[:2]
        del refs[:2]
    o_ref = refs.pop(0)
    if n_kv:
        s_ref = refs.pop(0)
    sc_ref, m_ref, acc_ref = refs
    tq = qa_ref.shape[0]
    nt = (((1,), (1,)), ((), ()))
    q2 = jnp.concatenate([qa_ref[...], qb_ref[...]], axis=0)

    def lane_tiles(x):
        return [x[:, t * LANES:(t + 1) * LANES] for t in range(x.shape[1] // LANES)]

    def key_rows(ref, c):
        return ref[pl.ds(pl.multiple_of(c * tk, tk), tk), :]

    def chunks(body):
        if n_kv:
            lax.fori_loop(0, n_kv, lambda c, _: body(c) or 0, 0)

    sc = lax.dot_general(q2, kc_ref[...], nt, preferred_element_type=F32)
    sc_ref[...] = sc
    m_ref[...] = functools.reduce(jnp.maximum, lane_tiles(sc))

    def score_chunk(c):
        s = lax.dot_general(q2, key_rows(kl_ref, c), nt, preferred_element_type=F32)
        s_ref[c] = s
        m_ref[...] = functools.reduce(jnp.maximum, lane_tiles(s), m_ref[...])

    chunks(score_chunk)
    m_ref[...] = jnp.broadcast_to(jnp.max(m_ref[...], axis=-1, keepdims=True), (2 * tq, LANES))

    def weigh(s, vs):
        m = m_ref[...]
        p = jnp.concatenate([jnp.exp2(st - m).astype(BF16) for st in lane_tiles(s)], axis=1)
        return jnp.dot(p, vs, preferred_element_type=F32)

    acc_ref[...] = weigh(sc_ref[...], vc_ref[...])

    def weigh_chunk(c):
        acc_ref[...] += weigh(s_ref[c], key_rows(vl_ref, c))

    chunks(weigh_chunk)
    o = acc_ref[:, 0:LANES] / acc_ref[:, LANES:2 * LANES]
    oa = o[0:tq]
    ob = o[tq:2 * tq]
    if diff:
        lv = lam_ref[...]
        lam = (jnp.exp(jnp.sum(lv[0:1] * lv[1:2], keepdims=True))
               - jnp.exp(jnp.sum(lv[2:3] * lv[3:4], keepdims=True)) + lam_init)
        o = oa - lam * ob
        o = o * lax.rsqrt(jnp.mean(o * o, axis=-1, keepdims=True) + NORM_EPS) * sw_ref[...]
        o_ref[...] = (o * (1.0 - lam_init)).astype(BF16)
    else:
        o_ref[:, 0:LANES] = oa.astype(BF16)
        o_ref[:, LANES:2 * LANES] = ob.astype(BF16)


ATTN_TQ = 512
ATTN_TK = (4096, 2048, 1024, 512, 256, 128)


def _attention(qkv_q, qkv_c, qkv_l, extra, extra_specs, cols, n_heads, out_width, name, **static):
    qa_col, qb_col, k_col, v_col = cols
    sq, n_ctx = qkv_q.shape[0], qkv_c.shape[0]
    tq = min(sq, ATTN_TQ)
    n_lat = 0 if qkv_l is None else qkv_l.shape[0]
    tk = next((t for t in ATTN_TK if n_lat and n_lat % t == 0), 0)
    n_kv = n_lat // tk if n_lat else 0

    def kv_specs(rows):
        return [pl.BlockSpec((rows, LANES), lambda h, i: (0, k_col(h))),
                pl.BlockSpec((rows, 2 * LANES), lambda h, i: (0, v_col(h)))]

    in_specs = [pl.BlockSpec((tq, LANES), lambda h, i: (i, qa_col(h))),
                pl.BlockSpec((tq, LANES), lambda h, i: (i, qb_col(h)))] + kv_specs(n_ctx)
    args = [qkv_q, qkv_q, qkv_c, qkv_c]
    scratch = [pltpu.VMEM((2 * tq, n_ctx), F32), pltpu.VMEM((2 * tq, LANES), F32),
               pltpu.VMEM((2 * tq, 2 * LANES), F32)]
    if n_kv:
        in_specs += kv_specs(n_lat)
        args += [qkv_l, qkv_l]
        scratch = [pltpu.VMEM((n_kv, 2 * tq, tk), F32)] + scratch
    return pl.pallas_call(
        functools.partial(_attn_kernel, tk=tk, n_kv=n_kv, **static),
        grid=(n_heads, sq // tq),
        in_specs=in_specs + extra_specs,
        out_specs=pl.BlockSpec((tq, out_width), lambda h, i: (i, h)),
        out_shape=jax.ShapeDtypeStruct((sq, GROUP_WIDTH), BF16),
        scratch_shapes=scratch,
        compiler_params=_cparams("parallel", "parallel"),
        name=name,
    )(*args, *extra)


def diff_attention(qkv_q, qkv_c, qkv_l, da_lambda, subln_w, lam_init):
    cols = (lambda h: QA_BLK + h, lambda h: QB_BLK + h, lambda h: KD_BLK + h, lambda h: VD_BLK // 2 + h)
    extra_specs = [pl.BlockSpec((4, DA_HEAD_DIM), lambda h, i: (0, 0)), pl.BlockSpec((1, LANES), lambda h, i: (0, 0))]
    return _attention(qkv_q, qkv_c, qkv_l, [da_lambda, subln_w.reshape(1, LANES)], extra_specs, cols,
                      DA_HEADS, LANES, "diff_attn", diff=True, lam_init=lam_init)


def gqa_attention(qkv_q, qkv_c, qkv_l):
    cols = (lambda h: QG_BLK + 2 * h, lambda h: QG_BLK + 2 * h + 1, lambda h: KG_BLK + h,
            lambda h: VG_BLK // 2 + h)
    return _attention(qkv_q, qkv_c, qkv_l, [], [], cols, GQA_KV_HEADS, 2 * LANES, "gqa_attn",
                      diff=False, lam_init=0.0)


CONV_CHUNK = 256


def _conv_chunk(x_ref, base, n_rows, chunk, cw, cb):
    cur = x_ref[pl.ds(base, chunk), :]
    prev = x_ref[pl.ds(pl.multiple_of(jnp.maximum(base - 8, 0), 8), 8), :]
    nxt = x_ref[pl.ds(pl.multiple_of(jnp.minimum(base + chunk, n_rows - 8), 8), 8), :]
    prev = jnp.where(base == 0, 0.0, prev)
    nxt = jnp.where(base + chunk >= n_rows, 0.0, nxt)
    cat = jnp.concatenate([prev, cur, nxt], axis=0)
    n = chunk + 16
    xm2 = pltpu.roll(cat, 2, 0)[8:8 + chunk]
    xm1 = pltpu.roll(cat, 1, 0)[8:8 + chunk]
    xp1 = pltpu.roll(cat, n - 1, 0)[8:8 + chunk]
    return cb + cw[0:1] * xm2 + cw[1:2] * xm1 + cw[2:3] * cur + cw[3:4] * xp1


def _dwconv_kernel(x_ref, w_ref, b_ref, o_ref, *, n_rows, chunk, act):
    cw = w_ref[...]
    cb = b_ref[...]

    def body(c, _):
        base = pl.multiple_of(c * chunk, chunk)
        y = _conv_chunk(x_ref, base, n_rows, chunk, cw, cb)
        o_ref[pl.ds(base, chunk), :] = _silu(y) if act else y
        return 0

    lax.fori_loop(0, n_rows // chunk, body, 0)


def dwconv(p, col0, width, conv_w, conv_b, act):
    n_rows = p.shape[0]
    chunk = min(n_rows, CONV_CHUNK)
    b0 = col0 // LANES
    return pl.pallas_call(
        functools.partial(_dwconv_kernel, n_rows=n_rows, chunk=chunk, act=act),
        grid=(width // LANES,),
        in_specs=[pl.BlockSpec((n_rows, LANES), lambda j: (0, b0 + j)),
                  pl.BlockSpec((4, LANES), lambda j: (0, j)),
                  pl.BlockSpec((1, LANES), lambda j: (0, j))],
        out_specs=pl.BlockSpec((n_rows, LANES), lambda j: (0, j)),
        out_shape=jax.ShapeDtypeStruct((n_rows, width), F32),
        compiler_params=_cparams("parallel"),
        name="dwconv",
    )(p, conv_w, conv_b.reshape(1, width))


LRU_CHUNK = 256


def _lru_scan_chunk(a, u, h_in, reverse):
    n = a.shape[0]
    row = lax.broadcasted_iota(jnp.int32, (n, 1), 0)
    for k in (1, 2, 4):
        if reverse:
            keep = row < n - k
            a_s = jnp.where(keep, pltpu.roll(a, n - k, 0), 1.0)
            u_s = jnp.where(keep, pltpu.roll(u, n - k, 0), 0.0)
        else:
            keep = row >= k
            a_s = jnp.where(keep, pltpu.roll(a, k, 0), 1.0)
            u_s = jnp.where(keep, pltpu.roll(u, k, 0), 0.0)
        u = u + a * u_s
        a = a * a_s
    n_tiles = n // SUBLANES
    order = range(n_tiles - 1, -1, -1) if reverse else range(n_tiles)
    tiles = [None] * n_tiles
    h = h_in
    for i in order:
        sl = slice(i * SUBLANES, (i + 1) * SUBLANES)
        h = u[sl] + a[sl] * h
        tiles[i] = h
    return jnp.concatenate(tiles, axis=0)


def _lru_kernel(x_ref, g_ref, cw_ref, cb_ref, w_ref, b_ref, lam_ref, h0_ref, y_ref, hT_ref, hf_ref, hb_ref,
                *, n_rows, chunk):
    n_chunks = n_rows // chunk
    cw = cw_ref[...]
    cb = cb_ref[...]

    def gates(d):
        lam = lam_ref[d:d + 1, :]
        log_sig = jnp.minimum(lam, 0.0) - jnp.log1p(jnp.exp(-jnp.abs(lam)))
        return (w_ref[d, 0].astype(BF16), w_ref[d, 1].astype(BF16), b_ref[d, 0:1, :], b_ref[d, 1:2, :], log_sig)

    def sweep(base, h, params, reverse):
        w_r, w_i, b_r, b_i, log_sig = params
        x = _conv_chunk(x_ref, base, n_rows, chunk, cw, cb)
        xb = x.astype(BF16)
        r = jax.nn.sigmoid(jnp.dot(xb, w_r, preferred_element_type=F32) + b_r)
        i = jax.nn.sigmoid(jnp.dot(xb, w_i, preferred_element_type=F32) + b_i)
        log_a = RGLRU_C * r * log_sig
        a = jnp.exp(log_a)
        u = jnp.sqrt(1.0 - a * a) * (i * x)
        return _lru_scan_chunk(a, u, h, reverse)

    fwd, bwd = gates(0), gates(1)

    def body(c, carry):
        hf, hb = carry
        base_f = pl.multiple_of(c * chunk, chunk)
        base_b = pl.multiple_of((n_chunks - 1 - c) * chunk, chunk)
        hs_f = sweep(base_f, hf, fwd, False)
        hs_b = sweep(base_b, hb, bwd, True)
        hf_ref[pl.ds(base_f, chunk), :] = hs_f
        hb_ref[pl.ds(base_b, chunk), :] = hs_b
        return hs_f[chunk - 1:chunk], hs_b[0:1]

    hT_ref[0:1, :], hT_ref[1:2, :] = lax.fori_loop(0, n_chunks, body, (h0_ref[0:1, :], h0_ref[1:2, :]))

    def emit(c, _):
        rows = pl.ds(pl.multiple_of(c * chunk, chunk), chunk)
        y_ref[rows, :] = ((hf_ref[rows, :] + hb_ref[rows, :]) * jax.nn.gelu(g_ref[rows, :])).astype(BF16)
        return 0

    lax.fori_loop(0, n_chunks, emit, 0)


def rglru(p, x_col0, g_col0, conv_w, conv_b, w_gates, b_gates, lam, h0):
    n_rows = p.shape[0]
    chunk = min(n_rows, LRU_CHUNK)
    xb = x_col0 // LANES
    gb = g_col0 // LANES
    seq = pltpu.VMEM((n_rows, LANES), F32)
    return pl.pallas_call(
        functools.partial(_lru_kernel, n_rows=n_rows, chunk=chunk),
        grid=(LRU_BLOCKS,),
        in_specs=[pl.BlockSpec((n_rows, LANES), lambda j: (0, xb + j)),
                  pl.BlockSpec((n_rows, LANES), lambda j: (0, gb + j)),
                  pl.BlockSpec((4, LANES), lambda j: (0, j)),
                  pl.BlockSpec((1, LANES), lambda j: (0, j)),
                  pl.BlockSpec((2, 2, None, LANES, LANES), lambda j: (0, 0, j, 0, 0)),
                  pl.BlockSpec((2, 2, LANES), lambda j: (0, 0, j)),
                  pl.BlockSpec((2, LANES), lambda j: (0, j)),
                  pl.BlockSpec((2, LANES), lambda j: (0, j))],
        out_specs=[pl.BlockSpec((n_rows, LANES), lambda j: (0, j)),
                   pl.BlockSpec((2, LANES), lambda j: (0, j))],
        out_shape=[jax.ShapeDtypeStruct((n_rows, LRU_WIDTH), BF16),
                   jax.ShapeDtypeStruct((2, LRU_WIDTH), F32)],
        scratch_shapes=[seq, seq],
        compiler_params=_cparams("parallel"),
        name="rglru",
    )(p, p, conv_w, conv_b.reshape(1, LRU_WIDTH), w_gates, b_gates, lam, h0)


SSD_CHUNKS_PER_STEP = 8


def _ssd_direction(xs, bm, cm, dt_raw, dtT_raw, bias_row, bias_col, a_row, a_col, st_ref, reverse):
    q = SSD_CHUNK
    hi = lax.Precision.HIGHEST
    li = lax.broadcasted_iota(jnp.int32, (q, q), 0)
    si = lax.broadcasted_iota(jnp.int32, (q, q), 1)
    causal = (si >= li) if reverse else (si <= li)
    tri = causal.astype(F32)
    dt = jax.nn.softplus(dt_raw + bias_row)
    dtT = jax.nn.softplus(dtT_raw + bias_col)
    adt = dt * a_row
    adtT = dtT * a_col
    cum = jnp.dot(tri, adt, preferred_element_type=F32, precision=hi)
    cumT = lax.dot_general(adtT, tri, (((1,), (1,)), ((), ())), preferred_element_type=F32,
                           precision=hi)
    edge = cum[0:1] if reverse else cum[q - 1:q]

    def lanes(v):
        return [jnp.broadcast_to(v[:, h:h + 1], (v.shape[0], LANES)) for h in range(SSM_HEADS)]

    def per_channel(cols):
        first = lax.broadcasted_iota(jnp.int32, cols[0].shape, 1) < SSM_HEAD_DIM
        return jnp.concatenate([jnp.where(first, cols[2 * j], cols[2 * j + 1]) for j in range(SSM_HEADS // 2)],
                               axis=1)

    edgeT = cumT[:, 0:1] if reverse else cumT[:, q - 1:q]
    wT = dtT * jnp.exp(edgeT - cumT)
    cum_l = lanes(cum)
    grow = jnp.exp(per_channel(cum_l))
    keep = jnp.exp(per_channel(lanes(edge)))
    xs_b = xs.astype(BF16)
    per_group = SSM_HEADS // SSM_GROUPS
    first_head = lax.broadcasted_iota(jnp.int32, (1, LANES), 1) < SSM_HEAD_DIM
    ys = []
    for g in range(SSM_GROUPS):
        gsl = slice(g * per_group * SSM_HEAD_DIM, (g + 1) * per_group * SSM_HEAD_DIM)
        b_gt = bm[:, g * SSM_D_STATE:(g + 1) * SSM_D_STATE].T
        c_g = cm[:, g * SSM_D_STATE:(g + 1) * SSM_D_STATE].astype(BF16)
        gram = jnp.dot(c_g, b_gt.astype(BF16), preferred_element_type=F32)
        st = st_ref[:, gsl]
        y_off = grow[:, gsl] * jnp.dot(c_g, st.astype(BF16), preferred_element_type=F32)
        upd = []
        for pp in range(per_group // 2):
            psl = slice(pp * LANES, (pp + 1) * LANES)
            x_p = xs_b[:, gsl][:, psl]
            y_pair, u_pair = [], []
            for h in (g * per_group + 2 * pp, g * per_group + 2 * pp + 1):
                seg = cum_l[h] - cumT[h:h + 1, :]
                decay = jnp.exp(jnp.where(causal, seg, -jnp.inf))
                m = (gram * (decay * dtT[h:h + 1, :])).astype(BF16)
                y_pair.append(jnp.dot(m, x_p, preferred_element_type=F32))
                u_pair.append(jnp.dot((b_gt * wT[h:h + 1, :]).astype(BF16), x_p, preferred_element_type=F32))
            ys.append(jnp.where(first_head, y_pair[0], y_pair[1]) + y_off[:, psl])
            upd.append(jnp.where(first_head, u_pair[0], u_pair[1]))
        st_ref[:, gsl] = keep[:, gsl] * st + jnp.concatenate(upd, axis=1)
    return jnp.concatenate(ys, axis=1)


def _ssd_kernel(xf_ref, bf_ref, cf_ref, dtf_ref, dtTf_ref, xb_ref, bb_ref, cb_ref, dtb_ref, dtTb_ref,
                bias_ref, biasT_ref, alog_ref, alogT_ref, st0_ref, yf_ref, yb_ref, stT_ref, st_ref):
    c = pl.program_id(0)

    @pl.when(c == 0)
    def _():
        st_ref[...] = st0_ref[...]

    a_row = -jnp.exp(alog_ref[...])
    a_col = -jnp.exp(alogT_ref[...])
    q = SSD_CHUNK
    n_sub = xf_ref.shape[0] // q
    for k in range(n_sub):
        rows = slice(k * q, (k + 1) * q)
        yf_ref[rows, :] = _ssd_direction(xf_ref[rows, :], bf_ref[rows, :], cf_ref[rows, :],
                                         dtf_ref[rows, 0:SSM_HEADS], dtTf_ref[0:SSM_HEADS, rows],
                                         bias_ref[0:1, :], biasT_ref[:, 0:1], a_row[0:1, :], a_col[:, 0:1],
                                         st_ref.at[0], False)
    for k in reversed(range(n_sub)):
        rows = slice(k * q, (k + 1) * q)
        yb_ref[rows, :] = _ssd_direction(xb_ref[rows, :], bb_ref[rows, :], cb_ref[rows, :],
                                         dtb_ref[rows, SSM_HEADS:2 * SSM_HEADS],
                                         dtTb_ref[SSM_HEADS:2 * SSM_HEADS, rows],
                                         bias_ref[1:2, :], biasT_ref[:, 1:2], a_row[1:2, :], a_col[:, 1:2],
                                         st_ref.at[1], True)

    @pl.when(c == pl.num_programs(0) - 1)
    def _():
        stT_ref[...] = st_ref[...]


def ssd_scan(xbc, dt, dtT, dt_bias, a_log, st0):
    n_rows = xbc.shape[0]
    q = SSD_CHUNK * min(SSD_CHUNKS_PER_STEP, n_rows // SSD_CHUNK)
    nc = n_rows // q
    fwd = lambda c: c
    bwd = lambda c: nc - 1 - c

    def specs(ix):
        return [pl.BlockSpec((q, SSM_D_INNER), lambda c: (ix(c), 0)),
                pl.BlockSpec((q, 2 * SSM_D_STATE), lambda c: (ix(c), 2)),
                pl.BlockSpec((q, 2 * SSM_D_STATE), lambda c: (ix(c), 3)),
                pl.BlockSpec((q, 2 * SSM_HEADS), lambda c: (ix(c), 0)),
                pl.BlockSpec((2 * SSM_HEADS, q), lambda c: (0, ix(c)))]

    small = lambda shape: pl.BlockSpec(shape, lambda c: (0,) * len(shape))
    st_shape = (2, SSM_D_STATE, SSM_D_INNER)
    return pl.pallas_call(
        _ssd_kernel,
        grid=(nc,),
        in_specs=specs(fwd) + specs(bwd) + [small((2, SSM_HEADS)), small((SSM_HEADS, 2)),
                                            small((2, SSM_HEADS)), small((SSM_HEADS, 2)), small(st_shape)],
        out_specs=[pl.BlockSpec((q, SSM_D_INNER), lambda c: (c, 0)),
                   pl.BlockSpec((q, SSM_D_INNER), lambda c: (nc - 1 - c, 0)),
                   small(st_shape)],
        out_shape=[jax.ShapeDtypeStruct((n_rows, SSM_D_INNER), F32),
                   jax.ShapeDtypeStruct((n_rows, SSM_D_INNER), F32),
                   jax.ShapeDtypeStruct(st_shape, F32)],
        scratch_shapes=[pltpu.VMEM(st_shape, F32)],
        compiler_params=_cparams("arbitrary"),
        name="ssd_scan",
    )(xbc, xbc, xbc, dt, dtT, xbc, xbc, xbc, dt, dtT, dt_bias, dt_bias.T, a_log, a_log.T, st0)


def _ssd_finish_kernel(yf_ref, yb_ref, xs_ref, z_ref, d_ref, nw_ref, o_ref):
    hrow = lax.broadcasted_iota(jnp.int32, (SSM_HEADS, SSM_D_INNER), 0)
    hcol = lax.broadcasted_iota(jnp.int32, (SSM_HEADS, SSM_D_INNER), 1) // SSM_HEAD_DIM
    d_e = jnp.sum(jnp.where(hrow == hcol, d_ref[...], 0.0), axis=0, keepdims=True)
    y = yf_ref[...] + yb_ref[...] + d_e * xs_ref[...]
    y = y * _silu(z_ref[...])
    o_ref[...] = (y * lax.rsqrt(jnp.mean(y * y, axis=-1, keepdims=True) + NORM_EPS) * nw_ref[...]).astype(BF16)


def ssd_finish(yf, yb, xbc, p, z_col0, d_skip, norm_w):
    n_rows = yf.shape[0]
    tm = min(n_rows, ROWWISE_TM)
    zb = z_col0 // SSM_D_INNER
    blk = lambda cb: pl.BlockSpec((tm, SSM_D_INNER), lambda i: (i, cb))
    return pl.pallas_call(
        _ssd_finish_kernel,
        grid=(n_rows // tm,),
        in_specs=[blk(0), blk(0), blk(0), blk(zb),
                  pl.BlockSpec((SSM_HEADS, 1), lambda i: (0, 0)),
                  pl.BlockSpec((1, SSM_D_INNER), lambda i: (0, 0))],
        out_specs=blk(0),
        out_shape=jax.ShapeDtypeStruct((n_rows, SSM_D_INNER), BF16),
        compiler_params=_cparams("parallel"),
        name="ssd_finish",
    )(yf, yb, xbc, p, d_skip.reshape(SSM_HEADS, 1), norm_w.reshape(1, SSM_D_INNER))


LRU_X_COL = 2560
LRU_G_COL = 3072
SSM_Z_COL = 3584
SSM_XBC_COL = 4096


def _mixers(p, dt, qkv_q, qkv_c, qkv_l, lw, lam_init, lru_h0, ssd_st0):
    ya = diff_attention(qkv_q, qkv_c, qkv_l, lw["da_lambda"], lw["da_subln"], lam_init)
    yb = gqa_attention(qkv_q, qkv_c, qkv_l)
    yc, lru_hT = rglru(p, LRU_X_COL, LRU_G_COL, lw["lru_conv_w"], lw["lru_conv_b"], lw["lru_w_gates"],
                       lw["lru_b_gates"], lw["lru_lambda"], lru_h0)
    xbc = dwconv(p, SSM_XBC_COL, 2 * SSM_D_INNER, lw["ssm_conv_w"], lw["ssm_conv_b"], act=True)
    yf, ybk, ssd_stT = ssd_scan(xbc, dt, dt.T, lw["ssm_dt_bias"], lw["ssm_a_log"], ssd_st0)
    yd = ssd_finish(yf, ybk, xbc, p, SSM_Z_COL, lw["ssm_d"], lw["ssm_norm"])
    return [ya, yb, yc, yd], lru_hT, ssd_stT


def kernel(x, c, ctx, c_ctx, w_mod, b_mod, mix_norm, ffn_norm, w_in, w_out, da_lambda, da_subln, gqa_q_norm,
           gqa_k_norm, lru_conv_w, lru_conv_b, lru_w_gates, lru_b_gates, lru_lambda, ssm_conv_w, ssm_conv_b,
           ssm_dt_bias, ssm_a_log, ssm_d, ssm_norm, ffn_w_up, ffn_conv_w, ffn_conv_b, ffn_w_down, final_norm):
    depth = w_mod.shape[0]
    d = x.shape[-1]
    xl = x[0]
    xc = ctx[0]
    n_lat, n_ctx = xl.shape[0], xc.shape[0]

    cond8 = jnp.zeros((8, d), F32).at[0].set(c[0]).at[1].set(c_ctx)
    mods = modulation_all(cond8, w_mod, b_mod)
    tabs_l = _rope_tables(n_lat, DA_HEAD_DIM) + _rope_tables(n_lat, GQA_HEAD_DIM)
    tabs_c = _rope_tables(n_ctx, DA_HEAD_DIM) + _rope_tables(n_ctx, GQA_HEAD_DIM)
    lru_zero = jnp.zeros((2, LRU_WIDTH), F32)
    ssd_zero = jnp.zeros((2, SSM_D_STATE, SSM_D_INNER), F32)
    w_in_t = jnp.swapaxes(w_in, 1, 2)

    for layer in range(depth):
        need_ctx = layer < depth - 1
        lam_init = 0.8 - 0.6 * math.exp(-0.3 * layer)
        ml = [mods[layer, 0:1, k * d:(k + 1) * d] for k in range(6)]
        mc = [mods[layer, 1:2, k * d:(k + 1) * d] for k in range(6)]
        lw = dict(da_lambda=da_lambda[layer], da_subln=da_subln[layer],
                  lru_conv_w=lru_conv_w[layer], lru_conv_b=lru_conv_b[layer], lru_w_gates=lru_w_gates[layer],
                  lru_b_gates=lru_b_gates[layer], lru_lambda=lru_lambda[layer], ssm_conv_w=ssm_conv_w[layer],
                  ssm_conv_b=ssm_conv_b[layer], ssm_dt_bias=ssm_dt_bias[layer], ssm_a_log=ssm_a_log[layer],
                  ssm_d=ssm_d[layer], ssm_norm=ssm_norm[layer])
        nw_m = mix_norm[layer].reshape(1, d)
        nw_f = ffn_norm[layer].reshape(1, d)

        xn_c, dt_c = adaln(xc, nw_m, mc[0], mc[1], w_in_t, layer)
        xn_l, dt_l = adaln(xl, nw_m, ml[0], ml[1], w_in_t, layer)
        p_c, p_l = in_proj(xn_c, xn_l, w_in_t, layer)
        qkv_c = attn_prep(p_c, tabs_c, gqa_q_norm[layer], gqa_k_norm[layer], rope=False)
        qkv_l = attn_prep(p_l, tabs_l, gqa_q_norm[layer], gqa_k_norm[layer], rope=True)

        y_c, lru_h, ssd_st = _mixers(p_c, dt_c, qkv_c, qkv_c, None, lw, lam_init, lru_zero, ssd_zero)
        y_l, _, _ = _mixers(p_l, dt_l, qkv_l, qkv_c, qkv_l, lw, lam_init, lru_h, ssd_st)

        conv_w, conv_b = ffn_conv_w[layer], ffn_conv_b[layer]
        if need_ctx:
            xc, xl = resid_proj(w_out, layer, (y_l, xl, ml[2]), (y_c, xc, mc[2]))
            act_c, act_l = ffn_up(ffn_w_up, layer, conv_w, conv_b, adaln(xl, nw_f, ml[3], ml[4]),
                                  adaln(xc, nw_f, mc[3], mc[4]))
            xc, xl = resid_proj(ffn_w_down, layer, ([act_l], xl, ml[5]), ([act_c], xc, mc[5]))
        else:
            xl = resid_proj(w_out, layer, (y_l, xl, ml[2]))
            act_l = ffn_up(ffn_w_up, layer, conv_w, conv_b, adaln(xl, nw_f, ml[3], ml[4]))
            xl = resid_proj(ffn_w_down, layer, ([act_l], xl, ml[5]))

    return final_rmsnorm(xl, final_norm)[None]
```

```python
import functools
import math

import jax
import jax.numpy as jnp
from jax import lax
from jax.experimental import pallas as pl
from jax.experimental.pallas import tpu as pltpu

F32 = jnp.float32
BF16 = jnp.bfloat16

GRID_W = 64
GROUP_WIDTH = 512
DA_HEAD_DIM = 64
DA_HEADS = 4
GQA_HEAD_DIM = 128
GQA_HEADS = 4
GQA_KV_HEADS = 2
LRU_WIDTH = 512
LRU_BLOCKS = 4
RGLRU_C = 8.0
SSM_D_INNER = 512
SSM_HEAD_DIM = 64
SSM_HEADS = 8
SSM_GROUPS = 2
SSM_D_STATE = 128
SSD_CHUNK = 128
D_FF = 5632
ROPE_THETA = 10000.0
NORM_EPS = 1e-6
LOG2E = math.log2(math.e)

LANES = 128
SUBLANES = 8
HALO = 16
IN_COLS = 5136
IN_MAIN_COLS = 5120
N_DT = IN_COLS - IN_MAIN_COLS
ATTN_COLS = 2560
QA_BLK, QB_BLK, KD_BLK, VD_BLK = 0, 4, 8, 12
QG_BLK, KG_BLK, VG_BLK = 20, 24, 26
QKV_COLS = 30 * LANES
P_QD, P_KD, P_VD, P_QG, P_KG, P_VG = 0, 4, 8, 12, 16, 18

VMEM_LIMIT = 56 * 1024 * 1024
PROJ_TM = 1024
IN_TN = 1024
RESID_TM = 512
RESID_TN_MAX = 1024
RESID_W_ELEMS = 2048 * 1024
ROWWISE_TM = 1024
FFN_TN = 512


def _cparams(*sem):
    return pltpu.CompilerParams(dimension_semantics=sem, vmem_limit_bytes=VMEM_LIMIT)


def _silu(x):
    return x * jax.nn.sigmoid(x)


def _norm_mod(x, nw, shift, scale):
    y = x * lax.rsqrt(jnp.mean(x * x, axis=-1, keepdims=True) + NORM_EPS) * nw
    return y * (1.0 + scale) + shift


def _mod_kernel(c_ref, w_ref, b_ref, o_ref):
    s = _silu(c_ref[...]).astype(BF16)
    o_ref[...] = jnp.dot(s, w_ref[...].astype(BF16), preferred_element_type=F32) + b_ref[...]


def modulation_all(cond8, w_mod, b_mod):
    depth, d, n = w_mod.shape
    tn = IN_TN
    return pl.pallas_call(
        _mod_kernel,
        grid=(depth, n // tn),
        in_specs=[
            pl.BlockSpec((8, d), lambda l, j: (0, 0)),
            pl.BlockSpec((None, d, tn), lambda l, j: (l, 0, j)),
            pl.BlockSpec((None, 1, tn), lambda l, j: (l, 0, j)),
        ],
        out_specs=pl.BlockSpec((None, 8, tn), lambda l, j: (l, 0, j)),
        out_shape=jax.ShapeDtypeStruct((depth, 8, n), F32),
        compiler_params=_cparams("parallel", "parallel"),
        name="modulation",
    )(cond8, w_mod, b_mod.reshape(depth, 1, n))


ADALN_ROWS = 16
_NT = (((1,), (1,)), ((), ()))


def _adaln_kernel(*refs, with_dt):
    if with_dt:
        x_ref, nw_ref, sh_ref, sc_ref, wdt_ref, o_ref, dt_ref = refs
    else:
        x_ref, nw_ref, sh_ref, sc_ref, o_ref = refs
    gain = nw_ref[...] * (1.0 + sc_ref[...])
    shift = sh_ref[...]
    d = x_ref.shape[1]

    def body(r, _):
        rows = pl.ds(pl.multiple_of(r * ADALN_ROWS, ADALN_ROWS), ADALN_ROWS)
        x = x_ref[rows, :]
        inv = lax.rsqrt(jnp.sum(x * x, axis=-1, keepdims=True) * (1.0 / d) + NORM_EPS)
        o_ref[rows, :] = (x * inv * gain + shift).astype(BF16)
        return 0

    lax.fori_loop(0, x_ref.shape[0] // ADALN_ROWS, body, 0, unroll=4)
    if with_dt:
        dt_ref[...] = lax.dot_general(o_ref[...], wdt_ref[...].astype(BF16), _NT, preferred_element_type=F32)


ADALN_CHUNK = 512
ADALN_IN_BUFS = 3
ADALN_OUT_BUFS = 2


def _adaln_stream_kernel(*refs, with_dt, n_chunks):
    if with_dt:
        x_hbm, nw_ref, sh_ref, sc_ref, wdt_ref, o_hbm, dt_ref, xbuf, obuf, in_sem, out_sem = refs
    else:
        x_hbm, nw_ref, sh_ref, sc_ref, o_hbm, xbuf, obuf, in_sem, out_sem = refs
    gain = nw_ref[...] * (1.0 + sc_ref[...])
    shift = sh_ref[...]
    d = xbuf.shape[2]

    def rows_of(c):
        return pl.ds(pl.multiple_of(c * ADALN_CHUNK, ADALN_CHUNK), ADALN_CHUNK)

    def read(c, slot):
        return pltpu.make_async_copy(x_hbm.at[rows_of(c), :], xbuf.at[slot], in_sem.at[slot])

    def write(c, slot):
        return pltpu.make_async_copy(obuf.at[slot], o_hbm.at[rows_of(c), :], out_sem.at[slot])

    for c in range(min(ADALN_IN_BUFS, n_chunks)):
        read(c, c).start()

    def chunk(c, _):
        slot = c % ADALN_IN_BUFS
        oslot = c % ADALN_OUT_BUFS
        read(c, slot).wait()

        @pl.when(c >= ADALN_OUT_BUFS)
        def _():
            write(c - ADALN_OUT_BUFS, oslot).wait()

        def body(r, _):
            rows = pl.ds(pl.multiple_of(r * ADALN_ROWS, ADALN_ROWS), ADALN_ROWS)
            x = xbuf[slot, rows, :]
            inv = lax.rsqrt(jnp.sum(x * x, axis=-1, keepdims=True) * (1.0 / d) + NORM_EPS)
            obuf[oslot, rows, :] = (x * inv * gain + shift).astype(BF16)
            return 0

        lax.fori_loop(0, ADALN_CHUNK // ADALN_ROWS, body, 0, unroll=4)
        if with_dt:
            dt_ref[rows_of(c), :] = lax.dot_general(obuf[oslot], wdt_ref[...].astype(BF16), _NT,
                                                    preferred_element_type=F32)
        write(c, oslot).start()

        @pl.when(c + ADALN_IN_BUFS < n_chunks)
        def _():
            read(c + ADALN_IN_BUFS, slot).start()

        return 0

    lax.fori_loop(0, n_chunks, chunk, 0)
    for c in range(max(n_chunks - ADALN_OUT_BUFS, 0), n_chunks):
        write(c, c % ADALN_OUT_BUFS).wait()


def _adaln_stream(x, nw, shift, scale, w_in_t, layer):
    m, d = x.shape
    with_dt = w_in_t is not None
    n_chunks = m // ADALN_CHUNK
    vec = pl.BlockSpec((1, d), lambda i: (0, 0))
    hbm = pl.BlockSpec(memory_space=pl.ANY)
    in_specs, args = [hbm, vec, vec, vec], [x, nw, shift, scale]
    out_specs, out_shape = hbm, jax.ShapeDtypeStruct((m, d), BF16)
    if with_dt:
        in_specs.append(pl.BlockSpec((None, N_DT, d), lambda i: (layer, IN_MAIN_COLS // N_DT, 0)))
        args.append(w_in_t)
        out_specs = [hbm, pl.BlockSpec((m, N_DT), lambda i: (0, 0))]
        out_shape = [out_shape, jax.ShapeDtypeStruct((m, N_DT), F32)]
    return pl.pallas_call(
        functools.partial(_adaln_stream_kernel, with_dt=with_dt, n_chunks=n_chunks),
        grid=(1,),
        in_specs=in_specs,
        out_specs=out_specs,
        out_shape=out_shape,
        scratch_shapes=[pltpu.VMEM((ADALN_IN_BUFS, ADALN_CHUNK, d), F32),
                        pltpu.VMEM((ADALN_OUT_BUFS, ADALN_CHUNK, d), BF16),
                        pltpu.SemaphoreType.DMA((ADALN_IN_BUFS,)),
                        pltpu.SemaphoreType.DMA((ADALN_OUT_BUFS,))],
        compiler_params=_cparams("arbitrary"),
        name="adaln_stream",
    )(*args)


def adaln(x, nw, shift, scale, w_in_t=None, layer=None):
    m, d = x.shape
    if m > PROJ_TM:
        return _adaln_stream(x, nw, shift, scale, w_in_t, layer)
    tm = min(m, PROJ_TM)
    with_dt = w_in_t is not None
    vec = pl.BlockSpec((1, d), lambda i: (0, 0))
    rows = pl.BlockSpec((tm, d), lambda i: (i, 0))
    in_specs, args = [rows, vec, vec, vec], [x, nw, shift, scale]
    out_specs, out_shape = rows, jax.ShapeDtypeStruct((m, d), BF16)
    if with_dt:
        in_specs.append(pl.BlockSpec((None, N_DT, d), lambda i: (layer, IN_MAIN_COLS // N_DT, 0)))
        args.append(w_in_t)
        out_specs = [rows, pl.BlockSpec((tm, N_DT), lambda i: (i, 0))]
        out_shape = [out_shape, jax.ShapeDtypeStruct((m, N_DT), F32)]
    return pl.pallas_call(
        functools.partial(_adaln_kernel, with_dt=with_dt),
        grid=(m // tm,),
        in_specs=in_specs,
        out_specs=out_specs,
        out_shape=out_shape,
        compiler_params=_cparams("parallel"),
        name="adaln",
    )(*args)


def _lat_block(i):
    return jnp.maximum(i - 1, 0)


def _inproj_kernel(xc_ref, xl_ref, w_ref, oc_ref, ol_ref, wb_ref):
    i = pl.program_id(1)

    @pl.when(i == 0)
    def _():
        wb_ref[...] = w_ref[...].astype(BF16)
        oc_ref[...] = lax.dot_general(xc_ref[...], wb_ref[...], _NT, preferred_element_type=F32)
        ol_ref[...] = jnp.zeros_like(ol_ref)

    @pl.when(i > 0)
    def _():
        ol_ref[...] = lax.dot_general(xl_ref[...], wb_ref[...], _NT, preferred_element_type=F32)


def in_proj(xn_c, xn_l, w_in_t, layer):
    (mc, d), ml = xn_c.shape, xn_l.shape[0]
    tm = min(ml, PROJ_TM)
    return pl.pallas_call(
        _inproj_kernel,
        grid=(IN_MAIN_COLS // IN_TN, 1 + ml // tm),
        in_specs=[pl.BlockSpec((mc, d), lambda j, i: (0, 0)),
                  pl.BlockSpec((tm, d), lambda j, i: (_lat_block(i), 0)),
                  pl.BlockSpec((None, IN_TN, d), lambda j, i: (layer, j, 0))],
        out_specs=[pl.BlockSpec((mc, IN_TN), lambda j, i: (0, j)),
                   pl.BlockSpec((tm, IN_TN), lambda j, i: (_lat_block(i), j))],
        out_shape=[jax.ShapeDtypeStruct((mc, IN_MAIN_COLS), F32), jax.ShapeDtypeStruct((ml, IN_MAIN_COLS), F32)],
        scratch_shapes=[pltpu.VMEM((IN_TN, d), BF16)],
        compiler_params=_cparams("parallel", "arbitrary"),
        name="in_proj",
    )(xn_c, xn_l, w_in_t)


def _resid_kernel(*refs, n_y, with_ctx):
    refs = list(refs)
    ctx_refs = None
    if with_ctx:
        ctx_refs, refs = refs[:n_y + 2], refs[n_y + 2:]
    lat_refs, refs = refs[:n_y + 2], refs[n_y + 2:]
    w_ref = refs.pop(0)
    oc_ref = refs.pop(0) if with_ctx else None
    ol_ref, wb_ref = refs
    i = pl.program_id(1)
    kk = wb_ref.shape[0] // n_y

    def project(stream, o_ref):
        ys, x_ref, g_ref = stream[:n_y], stream[n_y], stream[n_y + 1]
        acc = functools.reduce(jnp.add, [
            jnp.dot(ys[a][...], wb_ref[a * kk:(a + 1) * kk, :], preferred_element_type=F32) for a in range(n_y)])
        o_ref[...] = x_ref[...] + g_ref[...] * acc

    @pl.when(i == 0)
    def _():
        wb_ref[...] = w_ref[...].astype(BF16)
        if with_ctx:
            project(ctx_refs, oc_ref)
            ol_ref[...] = jnp.zeros_like(ol_ref)
        else:
            project(lat_refs, ol_ref)

    @pl.when(i > 0)
    def _():
        project(lat_refs, ol_ref)


def resid_proj(w_all, layer, lat, ctx=None):
    ys_l, x_l, gate_l = lat
    ml = x_l.shape[0]
    _, k, n = w_all.shape
    n_y = len(ys_l)
    kk = k // n_y
    tm = min(ml, RESID_TM)
    tn = RESID_TN_MAX if k * RESID_TN_MAX <= RESID_W_ELEMS else RESID_TN_MAX // 2
    with_ctx = ctx is not None
    lat_blk = _lat_block if with_ctx else (lambda i: i)

    def stream_specs(rows, blk):
        return ([pl.BlockSpec((rows, kk), lambda j, i: (blk(i), 0)) for _ in range(n_y)]
                + [pl.BlockSpec((rows, tn), lambda j, i: (blk(i), j)), pl.BlockSpec((1, tn), lambda j, i: (0, j))])

    in_specs = stream_specs(tm, lat_blk) + [pl.BlockSpec((None, k, tn), lambda j, i: (layer, 0, j))]
    args = [*ys_l, x_l, gate_l, w_all]
    out_specs = [pl.BlockSpec((tm, tn), lambda j, i: (lat_blk(i), j))]
    out_shape = [jax.ShapeDtypeStruct((ml, n), F32)]
    if with_ctx:
        ys_c, x_c, gate_c = ctx
        mc = x_c.shape[0]
        in_specs = stream_specs(mc, lambda i: 0) + in_specs
        args = [*ys_c, x_c, gate_c] + args
        out_specs = [pl.BlockSpec((mc, tn), lambda j, i: (0, j))] + out_specs
        out_shape = [jax.ShapeDtypeStruct((mc, n), F32)] + out_shape
    out = pl.pallas_call(
        functools.partial(_resid_kernel, n_y=n_y, with_ctx=with_ctx),
        grid=(n // tn, ml // tm + int(with_ctx)),
        in_specs=in_specs,
        out_specs=out_specs,
        out_shape=out_shape,
        scratch_shapes=[pltpu.VMEM((k, tn), BF16)],
        compiler_params=_cparams("parallel", "arbitrary"),
        name="resid_proj",
    )(*args)
    return tuple(out) if with_ctx else out[0]


def _ffn_up_kernel(*refs, n_lat_blocks, with_ctx):
    refs = list(refs)
    xc_ref = refs.pop(0) if with_ctx else None
    xp_ref, x_ref, xx_ref, wg_ref, wu_ref, cw_ref, cb_ref = refs[:7]
    del refs[:7]
    oc_ref = refs.pop(0) if with_ctx else None
    o_ref, xn_ref, g_ref, wb_ref = refs
    i = pl.program_id(1)
    blk = (i - 1) if with_ctx else i

    def up_block(prev, cur_ref, nxt, out_ref):
        n = cur_ref.shape[0]
        xn_ref[pl.ds(0, HALO), :] = prev
        xn_ref[pl.ds(HALO, n), :] = cur_ref[...]
        xn_ref[pl.ds(HALO + n, HALO), :] = nxt
        g_ref[pl.ds(0, n + 2 * HALO), :] = jnp.dot(xn_ref[pl.ds(0, n + 2 * HALO), :], wb_ref[0],
                                                   preferred_element_type=F32)
        up = jnp.dot(cur_ref[...], wb_ref[1], preferred_element_type=F32)
        cw = cw_ref[...]
        gc = (cb_ref[...] + cw[0:1] * g_ref[pl.ds(HALO - 1, n), :] + cw[1:2] * g_ref[pl.ds(HALO, n), :]
              + cw[2:3] * g_ref[pl.ds(HALO + 1, n), :])
        out_ref[...] = (_silu(gc) * up).astype(BF16)

    def latent_block():
        zeros = jnp.zeros_like(xp_ref)
        up_block(jnp.where(blk == 0, zeros, xp_ref[...]), x_ref,
                 jnp.where(blk == n_lat_blocks - 1, zeros, xx_ref[...]), o_ref)

    @pl.when(i == 0)
    def _():
        wb_ref[0] = wg_ref[...].astype(BF16)
        wb_ref[1] = wu_ref[...].astype(BF16)
        if with_ctx:
            zeros = jnp.zeros_like(xp_ref)
            up_block(zeros, xc_ref, zeros, oc_ref)
            o_ref[...] = jnp.zeros_like(o_ref)
        else:
            latent_block()

    @pl.when(i > 0)
    def _():
        latent_block()


def ffn_up(w_up_all, layer, conv_w, conv_b, xn_l, xn_c=None):
    ml, d = xn_l.shape
    dff = w_up_all.shape[2] // 2
    tm = min(ml, PROJ_TM)
    tn = FFN_TN
    nrb = ml // tm
    hb = tm // HALO
    nhb = ml // HALO
    with_ctx = xn_c is not None
    blk = _lat_block if with_ctx else (lambda i: i)
    in_specs = [pl.BlockSpec((HALO, d), lambda j, i: (jnp.maximum(blk(i) * hb - 1, 0), 0)),
                pl.BlockSpec((tm, d), lambda j, i: (blk(i), 0)),
                pl.BlockSpec((HALO, d), lambda j, i: (jnp.minimum((blk(i) + 1) * hb, nhb - 1), 0)),
                pl.BlockSpec((None, d, tn), lambda j, i: (layer, 0, j)),
                pl.BlockSpec((None, d, tn), lambda j, i: (layer, 0, j + dff // tn)),
                pl.BlockSpec((3, tn), lambda j, i: (0, j)),
                pl.BlockSpec((1, tn), lambda j, i: (0, j))]
    args = [xn_l, xn_l, xn_l, w_up_all, w_up_all, conv_w, conv_b.reshape(1, dff)]
    out_specs = [pl.BlockSpec((tm, tn), lambda j, i: (blk(i), j))]
    out_shape = [jax.ShapeDtypeStruct((ml, dff), BF16)]
    if with_ctx:
        mc = xn_c.shape[0]
        in_specs = [pl.BlockSpec((mc, d), lambda j, i: (0, 0))] + in_specs
        args = [xn_c] + args
        out_specs = [pl.BlockSpec((mc, tn), lambda j, i: (0, j))] + out_specs
        out_shape = [jax.ShapeDtypeStruct((mc, dff), BF16)] + out_shape
    out = pl.pallas_call(
        functools.partial(_ffn_up_kernel, n_lat_blocks=nrb, with_ctx=with_ctx),
        grid=(dff // tn, nrb + int(with_ctx)),
        in_specs=in_specs,
        out_specs=out_specs,
        out_shape=out_shape,
        scratch_shapes=[pltpu.VMEM((tm + 2 * HALO, d), BF16), pltpu.VMEM((tm + 2 * HALO, tn), F32),
                        pltpu.VMEM((2, d, tn), BF16)],
        compiler_params=_cparams("parallel", "arbitrary"),
        name="ffn_up",
    )(*args)
    return tuple(out) if with_ctx else out[0]


def _rmsnorm_kernel(x_ref, w_ref, o_ref):
    x = x_ref[...]
    o_ref[...] = x * lax.rsqrt(jnp.mean(x * x, axis=-1, keepdims=True) + NORM_EPS) * w_ref[...]


def final_rmsnorm(x, w):
    m, d = x.shape
    tm = min(m, ROWWISE_TM)
    return pl.pallas_call(
        _rmsnorm_kernel,
        grid=(m // tm,),
        in_specs=[pl.BlockSpec((tm, d), lambda i: (i, 0)), pl.BlockSpec((1, d), lambda i: (0, 0))],
        out_specs=pl.BlockSpec((tm, d), lambda i: (i, 0)),
        out_shape=jax.ShapeDtypeStruct((m, d), F32),
        compiler_params=_cparams("parallel"),
        name="final_norm",
    )(x, w.reshape(1, d))


def _rope_tables(n_tok, head_dim):
    q = head_dim // 4
    lane = jnp.arange(LANES, dtype=jnp.int32)
    u = lane % head_dim
    region = u // q
    freqs = jnp.power(ROPE_THETA, -(u % q).astype(F32) / q)[None, :]
    ang_r = jnp.arange(n_tok // GRID_W, dtype=jnp.int32).astype(F32)[:, None] * freqs
    ang_c = jnp.arange(GRID_W, dtype=jnp.int32).astype(F32)[:, None] * freqs
    by_row = (region < 2)[None, None, :]

    def per_token(f):
        return jnp.where(by_row, f(ang_r)[:, None, :], f(ang_c)[None, :, :]).reshape(n_tok, LANES)

    cos, sin = per_token(jnp.cos), per_token(jnp.sin)
    first = (region % 2 == 0)[None, :]
    return cos, jnp.where(first, -sin, 0.0), jnp.where(first, 0.0, sin)


def _prep_kernel(p_ref, cd_ref, ad_ref, bd_ref, cg_ref, ag_ref, bg_ref, qn_ref, kn_ref, o_ref, *, rope):
    lane = lax.broadcasted_iota(jnp.int32, (1, LANES), 1)
    lo = lane < DA_HEAD_DIM

    def blk(b):
        return p_ref[:, b * LANES:(b + 1) * LANES]

    def put(b, v):
        o_ref[:, b * LANES:(b + 1) * LANES] = v.astype(BF16)

    def rot(x, cos, sa, sb, quarter):
        if not rope:
            return x
        return (x * cos[...] + pltpu.roll(x, LANES - quarter, 1) * sa[...]
                + pltpu.roll(x, quarter, 1) * sb[...])

    def rms(x, w):
        return x * lax.rsqrt(jnp.mean(x * x, axis=-1, keepdims=True) + NORM_EPS) * w[...]

    qd = DA_HEAD_DIM // 4
    qg = GQA_HEAD_DIM // 4
    ones = jnp.ones((p_ref.shape[0], LANES), F32)
    for h in range(DA_HEADS):
        q = rot(blk(P_QD + h), cd_ref, ad_ref, bd_ref, qd) * (DA_HEAD_DIM ** -0.5 * LOG2E)
        put(QA_BLK + h, jnp.where(lo, q, 0.0))
        put(QB_BLK + h, jnp.where(lo, 0.0, q))
        put(KD_BLK + h, rot(blk(P_KD + h), cd_ref, ad_ref, bd_ref, qd))
        put(VD_BLK + 2 * h, blk(P_VD + h))
        put(VD_BLK + 2 * h + 1, ones)
    for h in range(GQA_HEADS):
        q = rot(rms(blk(P_QG + h), qn_ref), cg_ref, ag_ref, bg_ref, qg) * (GQA_HEAD_DIM ** -0.5 * LOG2E)
        put(QG_BLK + h, q)
    for h in range(GQA_KV_HEADS):
        put(KG_BLK + h, rot(rms(blk(P_KG + h), kn_ref), cg_ref, ag_ref, bg_ref, qg))
        put(VG_BLK + 2 * h, blk(P_VG + h))
        put(VG_BLK + 2 * h + 1, ones)


def attn_prep(p, tables, q_norm, k_norm, rope):
    m = p.shape[0]
    tm = min(m, ROWWISE_TM)
    tab = pl.BlockSpec((tm, LANES), lambda i: (i, 0))
    vec = pl.BlockSpec((1, LANES), lambda i: (0, 0))
    return pl.pallas_call(
        functools.partial(_prep_kernel, rope=rope),
        grid=(m // tm,),
        in_specs=[pl.BlockSpec((tm, ATTN_COLS), lambda i: (i, 0))] + [tab] * 6 + [vec, vec],
        out_specs=pl.BlockSpec((tm, QKV_COLS), lambda i: (i, 0)),
        out_shape=jax.ShapeDtypeStruct((m, QKV_COLS), BF16),
        compiler_params=_cparams("parallel"),
        name="attn_prep",
    )(p, *tables, q_norm.reshape(1, LANES), k_norm.reshape(1, LANES))


def _attn_kernel(*refs, tk, n_kv, diff, lam_init):
    refs = list(refs)
    qa_ref, qb_ref, kc_ref, vc_ref = refs[:4]
    del refs[:4]
    if n_kv:
        kl_ref, vl_ref = refs[:2]
        del refs[:2]
    if diff:
        lam_ref, sw_ref = refs[:2]
        del refs[:2]
    o_ref = refs.pop(0)
    if n_kv:
        s_ref = refs.pop(0)
    sc_ref, m_ref, acc_ref = refs
    tq = qa_ref.shape[0]
    nt = (((1,), (1,)), ((), ()))
    q2 = jnp.concatenate([qa_ref[...], qb_ref[...]], axis=0)

    def lane_tiles(x):
        return [x[:, t * LANES:(t + 1) * LANES] for t in range(x.shape[1] // LANES)]

    def key_rows(ref, c):
        return ref[pl.ds(pl.multiple_of(c * tk, tk), tk), :]

    def chunks(body):
        if n_kv:
            lax.fori_loop(0, n_kv, lambda c, _: body(c) or 0, 0)

    sc = lax.dot_general(q2, kc_ref[...], nt, preferred_element_type=F32)
    sc_ref[...] = sc
    m_ref[...] = functools.reduce(jnp.maximum, lane_tiles(sc))

    def score_chunk(c):
        s = lax.dot_general(q2, key_rows(kl_ref, c), nt, preferred_element_type=F32)
        s_ref[c] = s
        m_ref[...] = functools.reduce(jnp.maximum, lane_tiles(s), m_ref[...])

    chunks(score_chunk)
    m_ref[...] = jnp.broadcast_to(jnp.max(m_ref[...], axis=-1, keepdims=True), (2 * tq, LANES))

    def weigh(s, vs):
        m = m_ref[...]
        p = jnp.concatenate([jnp.exp2(st - m).astype(BF16) for st in lane_tiles(s)], axis=1)
        return jnp.dot(p, vs, preferred_element_type=F32)

    acc_ref[...] = weigh(sc_ref[...], vc_ref[...])

    def weigh_chunk(c):
        acc_ref[...] += weigh(s_ref[c], key_rows(vl_ref, c))

    chunks(weigh_chunk)
    o = acc_ref[:, 0:LANES] / acc_ref[:, LANES:2 * LANES]
    oa = o[0:tq]
    ob = o[tq:2 * tq]
    if diff:
        lv = lam_ref[...]
        lam = (jnp.exp(jnp.sum(lv[0:1] * lv[1:2], keepdims=True))
               - jnp.exp(jnp.sum(lv[2:3] * lv[3:4], keepdims=True)) + lam_init)
        o = oa - lam * ob
        o = o * lax.rsqrt(jnp.mean(o * o, axis=-1, keepdims=True) + NORM_EPS) * sw_ref[...]
        o_ref[...] = (o * (1.0 - lam_init)).astype(BF16)
    else:
        o_ref[:, 0:LANES] = oa.astype(BF16)
        o_ref[:, LANES:2 * LANES] = ob.astype(BF16)


ATTN_TQ = 512
ATTN_TK = (4096, 2048, 1024, 512, 256, 128)


def _attention(qkv_q, qkv_c, qkv_l, extra, extra_specs, cols, n_heads, out_width, name, **static):
    qa_col, qb_col, k_col, v_col = cols
    sq, n_ctx = qkv_q.shape[0], qkv_c.shape[0]
    tq = min(sq, ATTN_TQ)
    n_lat = 0 if qkv_l is None else qkv_l.shape[0]
    tk = next((t for t in ATTN_TK if n_lat and n_lat % t == 0), 0)
    n_kv = n_lat // tk if n_lat else 0

    def kv_specs(rows):
        return [pl.BlockSpec((rows, LANES), lambda h, i: (0, k_col(h))),
                pl.BlockSpec((rows, 2 * LANES), lambda h, i: (0, v_col(h)))]

    in_specs = [pl.BlockSpec((tq, LANES), lambda h, i: (i, qa_col(h))),
                pl.BlockSpec((tq, LANES), lambda h, i: (i, qb_col(h)))] + kv_specs(n_ctx)
    args = [qkv_q, qkv_q, qkv_c, qkv_c]
    scratch = [pltpu.VMEM((2 * tq, n_ctx), F32), pltpu.VMEM((2 * tq, LANES), F32),
               pltpu.VMEM((2 * tq, 2 * LANES), F32)]
    if n_kv:
        in_specs += kv_specs(n_lat)
        args += [qkv_l, qkv_l]
        scratch = [pltpu.VMEM((n_kv, 2 * tq, tk), F32)] + scratch
    return pl.pallas_call(
        functools.partial(_attn_kernel, tk=tk, n_kv=n_kv, **static),
        grid=(n_heads, sq // tq),
        in_specs=in_specs + extra_specs,
        out_specs=pl.BlockSpec((tq, out_width), lambda h, i: (i, h)),
        out_shape=jax.ShapeDtypeStruct((sq, GROUP_WIDTH), BF16),
        scratch_shapes=scratch,
        compiler_params=_cparams("parallel", "parallel"),
        name=name,
    )(*args, *extra)


def diff_attention(qkv_q, qkv_c, qkv_l, da_lambda, subln_w, lam_init):
    cols = (lambda h: QA_BLK + h, lambda h: QB_BLK + h, lambda h: KD_BLK + h, lambda h: VD_BLK // 2 + h)
    extra_specs = [pl.BlockSpec((4, DA_HEAD_DIM), lambda h, i: (0, 0)), pl.BlockSpec((1, LANES), lambda h, i: (0, 0))]
    return _attention(qkv_q, qkv_c, qkv_l, [da_lambda, subln_w.reshape(1, LANES)], extra_specs, cols,
                      DA_HEADS, LANES, "diff_attn", diff=True, lam_init=lam_init)


def gqa_attention(qkv_q, qkv_c, qkv_l):
    cols = (lambda h: QG_BLK + 2 * h, lambda h: QG_BLK + 2 * h + 1, lambda h: KG_BLK + h,
            lambda h: VG_BLK // 2 + h)
    return _attention(qkv_q, qkv_c, qkv_l, [], [], cols, GQA_KV_HEADS, 2 * LANES, "gqa_attn",
                      diff=False, lam_init=0.0)


CONV_CHUNK = 256


def _conv_chunk(x_ref, base, n_rows, chunk, cw, cb):
    cur = x_ref[pl.ds(base, chunk), :]
    prev = x_ref[pl.ds(pl.multiple_of(jnp.maximum(base - 8, 0), 8), 8), :]
    nxt = x_ref[pl.ds(pl.multiple_of(jnp.minimum(base + chunk, n_rows - 8), 8), 8), :]
    prev = jnp.where(base == 0, 0.0, prev)
    nxt = jnp.where(base + chunk >= n_rows, 0.0, nxt)
    cat = jnp.concatenate([prev, cur, nxt], axis=0)
    n = chunk + 16
    xm2 = pltpu.roll(cat, 2, 0)[8:8 + chunk]
    xm1 = pltpu.roll(cat, 1, 0)[8:8 + chunk]
    xp1 = pltpu.roll(cat, n - 1, 0)[8:8 + chunk]
    return cb + cw[0:1] * xm2 + cw[1:2] * xm1 + cw[2:3] * cur + cw[3:4] * xp1


def _dwconv_kernel(x_ref, w_ref, b_ref, o_ref, *, n_rows, chunk, act):
    cw = w_ref[...]
    cb = b_ref[...]

    def body(c, _):
        base = pl.multiple_of(c * chunk, chunk)
        y = _conv_chunk(x_ref, base, n_rows, chunk, cw, cb)
        o_ref[pl.ds(base, chunk), :] = _silu(y) if act else y
        return 0

    lax.fori_loop(0, n_rows // chunk, body, 0)


def dwconv(p, col0, width, conv_w, conv_b, act):
    n_rows = p.shape[0]
    chunk = min(n_rows, CONV_CHUNK)
    b0 = col0 // LANES
    return pl.pallas_call(
        functools.partial(_dwconv_kernel, n_rows=n_rows, chunk=chunk, act=act),
        grid=(width // LANES,),
        in_specs=[pl.BlockSpec((n_rows, LANES), lambda j: (0, b0 + j)),
                  pl.BlockSpec((4, LANES), lambda j: (0, j)),
                  pl.BlockSpec((1, LANES), lambda j: (0, j))],
        out_specs=pl.BlockSpec((n_rows, LANES), lambda j: (0, j)),
        out_shape=jax.ShapeDtypeStruct((n_rows, width), F32),
        compiler_params=_cparams("parallel"),
        name="dwconv",
    )(p, conv_w, conv_b.reshape(1, width))


LRU_CHUNK = 256


def _lru_scan_chunk(a, u, h_in, reverse):
    n = a.shape[0]
    row = lax.broadcasted_iota(jnp.int32, (n, 1), 0)
    for k in (1, 2, 4):
        if reverse:
            keep = row < n - k
            a_s = jnp.where(keep, pltpu.roll(a, n - k, 0), 1.0)
            u_s = jnp.where(keep, pltpu.roll(u, n - k, 0), 0.0)
        else:
            keep = row >= k
            a_s = jnp.where(keep, pltpu.roll(a, k, 0), 1.0)
            u_s = jnp.where(keep, pltpu.roll(u, k, 0), 0.0)
        u = u + a * u_s
        a = a * a_s
    n_tiles = n // SUBLANES
    order = range(n_tiles - 1, -1, -1) if reverse else range(n_tiles)
    tiles = [None] * n_tiles
    h = h_in
    for i in order:
        sl = slice(i * SUBLANES, (i + 1) * SUBLANES)
        h = u[sl] + a[sl] * h
        tiles[i] = h
    return jnp.concatenate(tiles, axis=0)


def _lru_kernel(x_ref, g_ref, cw_ref, cb_ref, w_ref, b_ref, lam_ref, h0_ref, y_ref, hT_ref, hf_ref, hb_ref,
                *, n_rows, chunk):
    n_chunks = n_rows // chunk
    cw = cw_ref[...]
    cb = cb_ref[...]

    def gates(d):
        lam = lam_ref[d:d + 1, :]
        log_sig = jnp.minimum(lam, 0.0) - jnp.log1p(jnp.exp(-jnp.abs(lam)))
        return (w_ref[d, 0].astype(BF16), w_ref[d, 1].astype(BF16), b_ref[d, 0:1, :], b_ref[d, 1:2, :], log_sig)

    def sweep(base, h, params, reverse):
        w_r, w_i, b_r, b_i, log_sig = params
        x = _conv_chunk(x_ref, base, n_rows, chunk, cw, cb)
        xb = x.astype(BF16)
        r = jax.nn.sigmoid(jnp.dot(xb, w_r, preferred_element_type=F32) + b_r)
        i = jax.nn.sigmoid(jnp.dot(xb, w_i, preferred_element_type=F32) + b_i)
        log_a = RGLRU_C * r * log_sig
        a = jnp.exp(log_a)
        u = jnp.sqrt(1.0 - a * a) * (i * x)
        return _lru_scan_chunk(a, u, h, reverse)

    fwd, bwd = gates(0), gates(1)

    def body(c, carry):
        hf, hb = carry
        base_f = pl.multiple_of(c * chunk, chunk)
        base_b = pl.multiple_of((n_chunks - 1 - c) * chunk, chunk)
        hs_f = sweep(base_f, hf, fwd, False)
        hs_b = sweep(base_b, hb, bwd, True)
        hf_ref[pl.ds(base_f, chunk), :] = hs_f
        hb_ref[pl.ds(base_b, chunk), :] = hs_b
        return hs_f[chunk - 1:chunk], hs_b[0:1]

    hT_ref[0:1, :], hT_ref[1:2, :] = lax.fori_loop(0, n_chunks, body, (h0_ref[0:1, :], h0_ref[1:2, :]))

    def emit(c, _):
        rows = pl.ds(pl.multiple_of(c * chunk, chunk), chunk)
        y_ref[rows, :] = ((hf_ref[rows, :] + hb_ref[rows, :]) * jax.nn.gelu(g_ref[rows, :])).astype(BF16)
        return 0

    lax.fori_loop(0, n_chunks, emit, 0)


def rglru(p, x_col0, g_col0, conv_w, conv_b, w_gates, b_gates, lam, h0):
    n_rows = p.shape[0]
    chunk = min(n_rows, LRU_CHUNK)
    xb = x_col0 // LANES
    gb = g_col0 // LANES
    seq = pltpu.VMEM((n_rows, LANES), F32)
    return pl.pallas_call(
        functools.partial(_lru_kernel, n_rows=n_rows, chunk=chunk),
        grid=(LRU_BLOCKS,),
        in_specs=[pl.BlockSpec((n_rows, LANES), lambda j: (0, xb + j)),
                  pl.BlockSpec((n_rows, LANES), lambda j: (0, gb + j)),
                  pl.BlockSpec((4, LANES), lambda j: (0, j)),
                  pl.BlockSpec((1, LANES), lambda j: (0, j)),
                  pl.BlockSpec((2, 2, None, LANES, LANES), lambda j: (0, 0, j, 0, 0)),
                  pl.BlockSpec((2, 2, LANES), lambda j: (0, 0, j)),
                  pl.BlockSpec((2, LANES), lambda j: (0, j)),
                  pl.BlockSpec((2, LANES), lambda j: (0, j))],
        out_specs=[pl.BlockSpec((n_rows, LANES), lambda j: (0, j)),
                   pl.BlockSpec((2, LANES), lambda j: (0, j))],
        out_shape=[jax.ShapeDtypeStruct((n_rows, LRU_WIDTH), BF16),
                   jax.ShapeDtypeStruct((2, LRU_WIDTH), F32)],
        scratch_shapes=[seq, seq],
        compiler_params=_cparams("parallel"),
        name="rglru",
    )(p, p, conv_w, conv_b.reshape(1, LRU_WIDTH), w_gates, b_gates, lam, h0)


SSD_CHUNKS_PER_STEP = 4


def _ssd_direction(xs, bm, cm, dt_raw, dtT_raw, bias_row, bias_col, a_row, a_col, st_ref, reverse):
    q = SSD_CHUNK
    hi = lax.Precision.HIGHEST
    li = lax.broadcasted_iota(jnp.int32, (q, q), 0)
    si = lax.broadcasted_iota(jnp.int32, (q, q), 1)
    causal = (si >= li) if reverse else (si <= li)
    tri = causal.astype(F32)
    dt = jax.nn.softplus(dt_raw + bias_row)
    dtT = jax.nn.softplus(dtT_raw + bias_col)
    adt = dt * a_row
    adtT = dtT * a_col
    cum = jnp.dot(tri, adt, preferred_element_type=F32, precision=hi)
    cumT = lax.dot_general(adtT, tri, (((1,), (1,)), ((), ())), preferred_element_type=F32,
                           precision=hi)
    edge = cum[0:1] if reverse else cum[q - 1:q]

    def lanes(v):
        return [jnp.broadcast_to(v[:, h:h + 1], (v.shape[0], LANES)) for h in range(SSM_HEADS)]

    def per_channel(cols):
        first = lax.broadcasted_iota(jnp.int32, cols[0].shape, 1) < SSM_HEAD_DIM
        return jnp.concatenate([jnp.where(first, cols[2 * j], cols[2 * j + 1]) for j in range(SSM_HEADS // 2)],
                               axis=1)

    edgeT = cumT[:, 0:1] if reverse else cumT[:, q - 1:q]
    wT = dtT * jnp.exp(edgeT - cumT)
    cum_l = lanes(cum)
    grow = jnp.exp(per_channel(cum_l))
    keep = jnp.exp(per_channel(lanes(edge)))
    xs_b = xs.astype(BF16)
    per_group = SSM_HEADS // SSM_GROUPS
    first_head = lax.broadcasted_iota(jnp.int32, (1, LANES), 1) < SSM_HEAD_DIM
    ys = []
    for g in range(SSM_GROUPS):
        gsl = slice(g * per_group * SSM_HEAD_DIM, (g + 1) * per_group * SSM_HEAD_DIM)
        b_gt = bm[:, g * SSM_D_STATE:(g + 1) * SSM_D_STATE].T
        c_g = cm[:, g * SSM_D_STATE:(g + 1) * SSM_D_STATE].astype(BF16)
        gram = jnp.dot(c_g, b_gt.astype(BF16), preferred_element_type=F32)
        st = st_ref[:, gsl]
        y_off = grow[:, gsl] * jnp.dot(c_g, st.astype(BF16), preferred_element_type=F32)
        upd = []
        for pp in range(per_group // 2):
            psl = slice(pp * LANES, (pp + 1) * LANES)
            x_p = xs_b[:, gsl][:, psl]
            y_pair, u_pair = [], []
            for h in (g * per_group + 2 * pp, g * per_group + 2 * pp + 1):
                seg = cum_l[h] - cumT[h:h + 1, :]
                decay = jnp.exp(jnp.where(causal, seg, -jnp.inf))
                m = (gram * (decay * dtT[h:h + 1, :])).astype(BF16)
                y_pair.append(jnp.dot(m, x_p, preferred_element_type=F32))
                u_pair.append(jnp.dot((b_gt * wT[h:h + 1, :]).astype(BF16), x_p, preferred_element_type=F32))
            ys.append(jnp.where(first_head, y_pair[0], y_pair[1]) + y_off[:, psl])
            upd.append(jnp.where(first_head, u_pair[0], u_pair[1]))
        st_ref[:, gsl] = keep[:, gsl] * st + jnp.concatenate(upd, axis=1)
    return jnp.concatenate(ys, axis=1)


def _ssd_kernel(xf_ref, bf_ref, cf_ref, dtf_ref, dtTf_ref, xb_ref, bb_ref, cb_ref, dtb_ref, dtTb_ref,
                bias_ref, biasT_ref, alog_ref, alogT_ref, st0_ref, yf_ref, yb_ref, stT_ref, st_ref):
    c = pl.program_id(0)

    @pl.when(c == 0)
    def _():
        st_ref[...] = st0_ref[...]

    a_row = -jnp.exp(alog_ref[...])
    a_col = -jnp.exp(alogT_ref[...])
    q = SSD_CHUNK
    n_sub = xf_ref.shape[0] // q
    for k in range(n_sub):
        rows = slice(k * q, (k + 1) * q)
        yf_ref[rows, :] = _ssd_direction(xf_ref[rows, :], bf_ref[rows, :], cf_ref[rows, :],
                                         dtf_ref[rows, 0:SSM_HEADS], dtTf_ref[0:SSM_HEADS, rows],
                                         bias_ref[0:1, :], biasT_ref[:, 0:1], a_row[0:1, :], a_col[:, 0:1],
                                         st_ref.at[0], False)
    for k in reversed(range(n_sub)):
        rows = slice(k * q, (k + 1) * q)
        yb_ref[rows, :] = _ssd_direction(xb_ref[rows, :], bb_ref[rows, :], cb_ref[rows, :],
                                         dtb_ref[rows, SSM_HEADS:2 * SSM_HEADS],
                                         dtTb_ref[SSM_HEADS:2 * SSM_HEADS, rows],
                                         bias_ref[1:2, :], biasT_ref[:, 1:2], a_row[1:2, :], a_col[:, 1:2],
                                         st_ref.at[1], True)

    @pl.when(c == pl.num_programs(0) - 1)
    def _():
        stT_ref[...] = st_ref[...]


def ssd_scan(xbc, dt, dtT, dt_bias, a_log, st0):
    n_rows = xbc.shape[0]
    q = SSD_CHUNK * min(SSD_CHUNKS_PER_STEP, n_rows // SSD_CHUNK)
    nc = n_rows // q
    fwd = lambda c: c
    bwd = lambda c: nc - 1 - c

    def specs(ix):
        return [pl.BlockSpec((q, SSM_D_INNER), lambda c: (ix(c), 0)),
                pl.BlockSpec((q, 2 * SSM_D_STATE), lambda c: (ix(c), 2)),
                pl.BlockSpec((q, 2 * SSM_D_STATE), lambda c: (ix(c), 3)),
                pl.BlockSpec((q, 2 * SSM_HEADS), lambda c: (ix(c), 0)),
                pl.BlockSpec((2 * SSM_HEADS, q), lambda c: (0, ix(c)))]

    small = lambda shape: pl.BlockSpec(shape, lambda c: (0,) * len(shape))
    st_shape = (2, SSM_D_STATE, SSM_D_INNER)
    return pl.pallas_call(
        _ssd_kernel,
        grid=(nc,),
        in_specs=specs(fwd) + specs(bwd) + [small((2, SSM_HEADS)), small((SSM_HEADS, 2)),
                                            small((2, SSM_HEADS)), small((SSM_HEADS, 2)), small(st_shape)],
        out_specs=[pl.BlockSpec((q, SSM_D_INNER), lambda c: (c, 0)),
                   pl.BlockSpec((q, SSM_D_INNER), lambda c: (nc - 1 - c, 0)),
                   small(st_shape)],
        out_shape=[jax.ShapeDtypeStruct((n_rows, SSM_D_INNER), F32),
                   jax.ShapeDtypeStruct((n_rows, SSM_D_INNER), F32),
                   jax.ShapeDtypeStruct(st_shape, F32)],
        scratch_shapes=[pltpu.VMEM(st_shape, F32)],
        compiler_params=_cparams("arbitrary"),
        name="ssd_scan",
    )(xbc, xbc, xbc, dt, dtT, xbc, xbc, xbc, dt, dtT, dt_bias, dt_bias.T, a_log, a_log.T, st0)


def _ssd_finish_kernel(yf_ref, yb_ref, xs_ref, z_ref, d_ref, nw_ref, o_ref):
    hrow = lax.broadcasted_iota(jnp.int32, (SSM_HEADS, SSM_D_INNER), 0)
    hcol = lax.broadcasted_iota(jnp.int32, (SSM_HEADS, SSM_D_INNER), 1) // SSM_HEAD_DIM
    d_e = jnp.sum(jnp.where(hrow == hcol, d_ref[...], 0.0), axis=0, keepdims=True)
    y = yf_ref[...] + yb_ref[...] + d_e * xs_ref[...]
    y = y * _silu(z_ref[...])
    o_ref[...] = (y * lax.rsqrt(jnp.mean(y * y, axis=-1, keepdims=True) + NORM_EPS) * nw_ref[...]).astype(BF16)


def ssd_finish(yf, yb, xbc, p, z_col0, d_skip, norm_w):
    n_rows = yf.shape[0]
    tm = min(n_rows, ROWWISE_TM)
    zb = z_col0 // SSM_D_INNER
    blk = lambda cb: pl.BlockSpec((tm, SSM_D_INNER), lambda i: (i, cb))
    return pl.pallas_call(
        _ssd_finish_kernel,
        grid=(n_rows // tm,),
        in_specs=[blk(0), blk(0), blk(0), blk(zb),
                  pl.BlockSpec((SSM_HEADS, 1), lambda i: (0, 0)),
                  pl.BlockSpec((1, SSM_D_INNER), lambda i: (0, 0))],
        out_specs=blk(0),
        out_shape=jax.ShapeDtypeStruct((n_rows, SSM_D_INNER), BF16),
        compiler_params=_cparams("parallel"),
        name="ssd_finish",
    )(yf, yb, xbc, p, d_skip.reshape(SSM_HEADS, 1), norm_w.reshape(1, SSM_D_INNER))


LRU_X_COL = 2560
LRU_G_COL = 3072
SSM_Z_COL = 3584
SSM_XBC_COL = 4096


def _mixers(p, dt, qkv_q, qkv_c, qkv_l, lw, lam_init, lru_h0, ssd_st0):
    ya = diff_attention(qkv_q, qkv_c, qkv_l, lw["da_lambda"], lw["da_subln"], lam_init)
    yb = gqa_attention(qkv_q, qkv_c, qkv_l)
    yc, lru_hT = rglru(p, LRU_X_COL, LRU_G_COL, lw["lru_conv_w"], lw["lru_conv_b"], lw["lru_w_gates"],
                       lw["lru_b_gates"], lw["lru_lambda"], lru_h0)
    xbc = dwconv(p, SSM_XBC_COL, 2 * SSM_D_INNER, lw["ssm_conv_w"], lw["ssm_conv_b"], act=True)
    yf, ybk, ssd_stT = ssd_scan(xbc, dt, dt.T, lw["ssm_dt_bias"], lw["ssm_a_log"], ssd_st0)
    yd = ssd_finish(yf, ybk, xbc, p, SSM_Z_COL, lw["ssm_d"], lw["ssm_norm"])
    return [ya, yb, yc, yd], lru_hT, ssd_stT


def kernel(x, c, ctx, c_ctx, w_mod, b_mod, mix_norm, ffn_norm, w_in, w_out, da_lambda, da_subln, gqa_q_norm,
           gqa_k_norm, lru_conv_w, lru_conv_b, lru_w_gates, lru_b_gates, lru_lambda, ssm_conv_w, ssm_conv_b,
           ssm_dt_bias, ssm_a_log, ssm_d, ssm_norm, ffn_w_up, ffn_conv_w, ffn_conv_b, ffn_w_down, final_norm):
    depth = w_mod.shape[0]
    d = x.shape[-1]
    xl = x[0]
    xc = ctx[0]
    n_lat, n_ctx = xl.shape[0], xc.shape[0]

    cond8 = jnp.zeros((8, d), F32).at[0].set(c[0]).at[1].set(c_ctx)
    mods = modulation_all(cond8, w_mod, b_mod)
    tabs_l = _rope_tables(n_lat, DA_HEAD_DIM) + _rope_tables(n_lat, GQA_HEAD_DIM)
    tabs_c = _rope_tables(n_ctx, DA_HEAD_DIM) + _rope_tables(n_ctx, GQA_HEAD_DIM)
    lru_zero = jnp.zeros((2, LRU_WIDTH), F32)
    ssd_zero = jnp.zeros((2, SSM_D_STATE, SSM_D_INNER), F32)
    w_in_t = jnp.swapaxes(w_in, 1, 2)

    for layer in range(depth):
        need_ctx = layer < depth - 1
        lam_init = 0.8 - 0.6 * math.exp(-0.3 * layer)
        ml = [mods[layer, 0:1, k * d:(k + 1) * d] for k in range(6)]
        mc = [mods[layer, 1:2, k * d:(k + 1) * d] for k in range(6)]
        lw = dict(da_lambda=da_lambda[layer], da_subln=da_subln[layer],
                  lru_conv_w=lru_conv_w[layer], lru_conv_b=lru_conv_b[layer], lru_w_gates=lru_w_gates[layer],
                  lru_b_gates=lru_b_gates[layer], lru_lambda=lru_lambda[layer], ssm_conv_w=ssm_conv_w[layer],
                  ssm_conv_b=ssm_conv_b[layer], ssm_dt_bias=ssm_dt_bias[layer], ssm_a_log=ssm_a_log[layer],
                  ssm_d=ssm_d[layer], ssm_norm=ssm_norm[layer])
        nw_m = mix_norm[layer].reshape(1, d)
        nw_f = ffn_norm[layer].reshape(1, d)

        xn_c, dt_c = adaln(xc, nw_m, mc[0], mc[1], w_in_t, layer)
        xn_l, dt_l = adaln(xl, nw_m, ml[0], ml[1], w_in_t, layer)
        p_c, p_l = in_proj(xn_c, xn_l, w_in_t, layer)
        qkv_c = attn_prep(p_c, tabs_c, gqa_q_norm[layer], gqa_k_norm[layer], rope=False)
        qkv_l = attn_prep(p_l, tabs_l, gqa_q_norm[layer], gqa_k_norm[layer], rope=True)

        y_c, lru_h, ssd_st = _mixers(p_c, dt_c, qkv_c, qkv_c, None, lw, lam_init, lru_zero, ssd_zero)
        y_l, _, _ = _mixers(p_l, dt_l, qkv_l, qkv_c, qkv_l, lw, lam_init, lru_h, ssd_st)

        conv_w, conv_b = ffn_conv_w[layer], ffn_conv_b[layer]
        if need_ctx:
            xc, xl = resid_proj(w_out, layer, (y_l, xl, ml[2]), (y_c, xc, mc[2]))
            act_c, act_l = ffn_up(ffn_w_up, layer, conv_w, conv_b, adaln(xl, nw_f, ml[3], ml[4]),
                                  adaln(xc, nw_f, mc[3], mc[4]))
            xc, xl = resid_proj(ffn_w_down, layer, ([act_l], xl, ml[5]), ([act_c], xc, mc[5]))
        else:
            xl = resid_proj(w_out, layer, (y_l, xl, ml[2]))
            act_l = ffn_up(ffn_w_up, layer, conv_w, conv_b, adaln(xl, nw_f, ml[3], ml[4]))
            xl = resid_proj(ffn_w_down, layer, ([act_l], xl, ml[5]))

    return final_rmsnorm(xl, final_norm)[None]
```
